```python
import jax, jax.numpy as jnp
from jax import lax
import numpy as np

D_MODEL = 2048
BATCH = 2
SEQ = 4096
DEPTH = 4

SB_HEADS = 8
SB_HEAD_DIM = 128
SB_WIDTH = SB_HEADS * SB_HEAD_DIM
MLA_HEADS = 8
MLA_NOPE_DIM = 128
MLA_ROPE_DIM = 64
MLA_V_DIM = 128
MLA_QK_DIM = MLA_NOPE_DIM + MLA_ROPE_DIM
MLA_WIDTH = MLA_HEADS * MLA_V_DIM
Q_LORA_RANK = 512
KV_LORA_RANK = 512
ROPE_THETA = 10000.0
IN_COLS = 3 * SB_WIDTH + Q_LORA_RANK + KV_LORA_RANK + MLA_ROPE_DIM + 2 * D_MODEL
D_FF = ((8 * D_MODEL // 3 + 255) // 256) * 256
N_EXPERTS = 8
TOP_K = 2
N_DENSE = (DEPTH + 1) // 2
N_MOE = DEPTH // 2
Q_BLOCK = 128
NORM_EPS = 1e-6
N_MOD = 6

kernel_name = "hybrid_stickbreak_mla_moe_adaln"


def _rms_norm(x, g):
    xf = x.astype(jnp.float32)
    y = xf * lax.rsqrt(jnp.mean(xf * xf, axis=-1, keepdims=True) + NORM_EPS)
    return (y * g.astype(jnp.float32)).astype(x.dtype)


def _rope_tables(positions, dtype):
    inv_freq = 1.0 / (ROPE_THETA ** (jnp.arange(0, MLA_ROPE_DIM, 2, dtype=jnp.float32) / MLA_ROPE_DIM))
    ang = positions.astype(jnp.float32)[..., None] * inv_freq
    return jnp.cos(ang).astype(dtype), jnp.sin(ang).astype(dtype)


def _apply_rope(x, cos, sin):
    half = x.shape[-1] // 2
    x1, x2 = x[..., :half], x[..., half:]
    return jnp.concatenate([x1 * cos - x2 * sin, x2 * cos + x1 * sin], axis=-1)


def _query_block_sweep(q, block_fn):
    B, H, S, d = q.shape
    nb = S // Q_BLOCK
    qb = q.reshape(B, H, nb, Q_BLOCK, d).transpose(2, 0, 1, 3, 4)
    starts = jnp.arange(nb, dtype=jnp.int32) * Q_BLOCK
    out = lax.map(block_fn, (qb, starts))
    return out.transpose(1, 2, 0, 3, 4).reshape(B, H, S, out.shape[-1])


def _stick_breaking_attention(q, k, v):
    S = k.shape[2]
    scale = q.shape[-1] ** -0.5
    key_idx = jnp.arange(S, dtype=jnp.int32)

    def block(args):
        qb, start = args
        t_idx = start + jnp.arange(Q_BLOCK, dtype=jnp.int32)
        z = jnp.einsum('bhqd,bhkd->bhqk', qb, k, preferred_element_type=jnp.float32) * scale
        causal = key_idx[None, :] < t_idx[:, None]
        sp = jnp.where(causal, jax.nn.softplus(z), 0.0)
        suffix = lax.cumsum(sp, axis=sp.ndim - 1, reverse=True) - sp
        log_a = z - sp - suffix
        a = jnp.exp(jnp.where(causal, log_a, -jnp.inf))
        return jnp.einsum('bhqk,bhkd->bhqd', a.astype(v.dtype), v)

    return _query_block_sweep(q, block)


def _causal_softmax_attention(q, k, v):
    S = k.shape[2]
    scale = q.shape[-1] ** -0.5
    key_idx = jnp.arange(S, dtype=jnp.int32)

    def block(args):
        qb, start = args
        t_idx = start + jnp.arange(Q_BLOCK, dtype=jnp.int32)
        s = jnp.einsum('bhqd,bhkd->bhqk', qb, k, preferred_element_type=jnp.float32) * scale
        s = jnp.where(key_idx[None, :] <= t_idx[:, None], s, -jnp.inf)
        p = jax.nn.softmax(s, axis=-1)
        return jnp.einsum('bhqk,bhkd->bhqd', p.astype(v.dtype), v)

    return _query_block_sweep(q, block)


def _hybrid_mixer(h, cos, sin, w_in, q_norm_g, kv_norm_g, w_uq, w_ukv, w_sb_up, w_mla_up, w_o):
    B, S, _ = h.shape
    proj = h @ w_in
    sizes = [SB_WIDTH, SB_WIDTH, SB_WIDTH, Q_LORA_RANK, KV_LORA_RANK, MLA_ROPE_DIM, D_MODEL]
    offs, o = [], 0
    for sz in sizes:
        o += sz
        offs.append(o)
    q_sb, k_sb, v_sb, c_q, c_kv, k_pe, g_sb, g_mla = jnp.split(proj, offs, axis=-1)

    def heads(t, n):
        return t.reshape(B, S, n, -1).transpose(0, 2, 1, 3)

    o_sb = _stick_breaking_attention(heads(q_sb, SB_HEADS), heads(k_sb, SB_HEADS), heads(v_sb, SB_HEADS))
    o_sb = o_sb.transpose(0, 2, 1, 3).reshape(B, S, SB_WIDTH)

    q = (_rms_norm(c_q, q_norm_g) @ w_uq).reshape(B, S, MLA_HEADS, MLA_QK_DIM)
    q_nope, q_pe = q[..., :MLA_NOPE_DIM], q[..., MLA_NOPE_DIM:]
    q_pe = _apply_rope(q_pe, cos[:, :, None, :], sin[:, :, None, :])
    kv = (_rms_norm(c_kv, kv_norm_g) @ w_ukv).reshape(B, S, MLA_HEADS, MLA_NOPE_DIM + MLA_V_DIM)
    k_nope, v_mla = kv[..., :MLA_NOPE_DIM], kv[..., MLA_NOPE_DIM:]
    k_pe = _apply_rope(k_pe, cos, sin)
    q_full = jnp.concatenate([q_nope, q_pe], axis=-1)
    k_full = jnp.concatenate(
        [k_nope, jnp.broadcast_to(k_pe[:, :, None, :], (B, S, MLA_HEADS, MLA_ROPE_DIM))], axis=-1)
    o_mla = _causal_softmax_attention(q_full.transpose(0, 2, 1, 3), k_full.transpose(0, 2, 1, 3),
                                      v_mla.transpose(0, 2, 1, 3))
    o_mla = o_mla.transpose(0, 2, 1, 3).reshape(B, S, MLA_WIDTH)

    y = jax.nn.sigmoid(g_sb) * (o_sb @ w_sb_up) + jax.nn.sigmoid(g_mla) * (o_mla @ w_mla_up)
    return y @ w_o


def _swiglu(h, wg, wu, wd):
    return (jax.nn.silu(h @ wg) * (h @ wu)) @ wd


def _moe_swiglu(h, w_router, wg, wu, wd):
    logits = jnp.einsum('bsd,de->bse', h, w_router, preferred_element_type=jnp.float32)
    top_v, top_i = lax.top_k(logits, TOP_K)
    top_w = jax.nn.softmax(top_v, axis=-1)
    gates = jnp.sum(jax.nn.one_hot(top_i, N_EXPERTS, dtype=jnp.float32) * top_w[..., None],
                    axis=-2).astype(h.dtype)
    y = jnp.zeros_like(h)
    for e in range(N_EXPERTS):
        y = y + gates[..., e:e + 1] * _swiglu(h, wg[e], wu[e], wd[e])
    return y


def _normal(k, shape, fan_in, mult=1.0):
    return jax.random.normal(k, shape, jnp.float32) * (mult * fan_in ** -0.5)


def setup_inputs(seed: int = 0) -> dict:
    key = jax.random.key(seed)
    ks = jax.random.split(key, 24)
    D = D_MODEL
    gain = lambda k, shape: 1.0 + 0.02 * jax.random.normal(k, shape, jnp.float32)
    positions = (jnp.arange(SEQ, dtype=jnp.int32)[None, :]
                 + jax.random.randint(ks[2], (BATCH, 1), 0, 1024, dtype=jnp.int32))
    return {
        "x": jax.random.normal(ks[0], (BATCH, SEQ, D), jnp.float32),
        "c": jax.random.normal(ks[1], (BATCH, D), jnp.float32),
        "positions": positions,
        "w_ada": _normal(ks[3], (DEPTH, D, N_MOD * D), D, 0.5),
        "b_ada": 0.02 * jax.random.normal(ks[4], (DEPTH, N_MOD * D), jnp.float32),
        "norm_mix_g": gain(ks[5], (DEPTH, D)),
        "norm_ffn_g": gain(ks[6], (DEPTH, D)),
        "w_in": _normal(ks[7], (DEPTH, D, IN_COLS), D),
        "q_norm_g": gain(ks[8], (DEPTH, Q_LORA_RANK)),
        "kv_norm_g": gain(ks[9], (DEPTH, KV_LORA_RANK)),
        "w_uq": _normal(ks[10], (DEPTH, Q_LORA_RANK, MLA_HEADS * MLA_QK_DIM), Q_LORA_RANK),
        "w_ukv": _normal(ks[11], (DEPTH, KV_LORA_RANK, MLA_HEADS * (MLA_NOPE_DIM + MLA_V_DIM)), KV_LORA_RANK),
        "w_sb_up": _normal(ks[12], (DEPTH, SB_WIDTH, D), SB_WIDTH),
        "w_mla_up": _normal(ks[13], (DEPTH, MLA_WIDTH, D), MLA_WIDTH),
        "w_o": _normal(ks[14], (DEPTH, D, D), D),
        "w_ffn_gate": _normal(ks[15], (N_DENSE, D, D_FF), D),
        "w_ffn_up": _normal(ks[16], (N_DENSE, D, D_FF), D),
        "w_ffn_down": _normal(ks[17], (N_DENSE, D_FF, D), D_FF),
        "w_router": _normal(ks[18], (N_MOE, D, N_EXPERTS), D),
        "w_exp_gate": _normal(ks[19], (N_MOE, N_EXPERTS, D, D_FF), D),
        "w_exp_up": _normal(ks[20], (N_MOE, N_EXPERTS, D, D_FF), D),
        "w_exp_down": _normal(ks[21], (N_MOE, N_EXPERTS, D_FF, D), D_FF),
        "final_norm_g": gain(ks[22], (D,)),
    }


def reference(x, c, positions, w_ada, b_ada, norm_mix_g, norm_ffn_g, w_in, q_norm_g, kv_norm_g,
              w_uq, w_ukv, w_sb_up, w_mla_up, w_o, w_ffn_gate, w_ffn_up, w_ffn_down,
              w_router, w_exp_gate, w_exp_up, w_exp_down, final_norm_g):
    cos, sin = _rope_tables(positions, x.dtype)
    c_act = jax.nn.silu(c)
    for l in range(DEPTH):
        mod = c_act @ w_ada[l] + b_ada[l]
        sh1, sc1, g1, sh2, sc2, g2 = [m[:, None, :] for m in jnp.split(mod, N_MOD, axis=-1)]
        h = _rms_norm(x, norm_mix_g[l]) * (1.0 + sc1) + sh1
        x = x + g1 * _hybrid_mixer(h, cos, sin, w_in[l], q_norm_g[l], kv_norm_g[l], w_uq[l], w_ukv[l],
                                   w_sb_up[l], w_mla_up[l], w_o[l])
        h = _rms_norm(x, norm_ffn_g[l]) * (1.0 + sc2) + sh2
        j = l // 2
        if l % 2 == 0:
            f = _swiglu(h, w_ffn_gate[j], w_ffn_up[j], w_ffn_down[j])
        else:
            f = _moe_swiglu(h, w_router[j], w_exp_gate[j], w_exp_up[j], w_exp_down[j])
        x = x + g2 * f
    return _rms_norm(x, final_norm_g)
```

```python
import functools

import jax
import jax.numpy as jnp
from jax import lax
from jax.experimental import pallas as pl
from jax.experimental.pallas import tpu as pltpu

BF16 = jnp.bfloat16
F32 = jnp.float32

SB_HEADS = 8
HEAD_DIM = 128
SB_WIDTH = SB_HEADS * HEAD_DIM
MLA_HEADS = 8
MLA_NOPE = 128
MLA_ROPE = 64
MLA_V = 128
MLA_QK = MLA_NOPE + MLA_ROPE
Q_RANK = 512
KV_RANK = 512
ROPE_THETA = 10000.0
N_EXPERTS = 8
TOP_K = 2
N_MOD = 6
NORM_EPS = 1e-6

LANE = 128
QPAD = 2 * LANE
VMEM_LIMIT = 56 << 20


def _call(body, *, grid, in_specs, out_specs, out_shape, scratch=(), nsp=0, name):
    return pl.pallas_call(
        body,
        grid_spec=pltpu.PrefetchScalarGridSpec(
            num_scalar_prefetch=nsp, grid=grid, in_specs=in_specs,
            out_specs=out_specs, scratch_shapes=scratch),
        out_shape=out_shape,
        compiler_params=pltpu.CompilerParams(
            dimension_semantics=("arbitrary",) * len(grid),
            vmem_limit_bytes=VMEM_LIMIT),
        name=name)


def _tile(n, pref):
    t = min(n, pref)
    while n % t:
        t //= 2
    return t


def _norm_mod(x, gain, sc, sh):
    ms = jnp.mean(x * x, axis=-1, keepdims=True)
    y = x * lax.rsqrt(ms + NORM_EPS)
    return (y * gain) * (1.0 + sc) + sh


def _mm_body(*refs, n_w, n_extra, n_out, epilogue):
    a_ref = refs[0]
    w_refs = refs[1:1 + n_w]
    extra = refs[1 + n_w:1 + n_w + n_extra]
    outs = refs[1 + n_w + n_extra:1 + n_w + n_extra + n_out]
    wb_refs = refs[1 + n_w + n_extra + n_out:]

    @pl.when(pl.program_id(1) == 0)
    def _():
        for w_ref, wb_ref in zip(w_refs, wb_refs):
            wb_ref[...] = w_ref[...].astype(BF16)

    a = a_ref[...]
    accs = [jnp.dot(a, wb_ref[...], preferred_element_type=F32) for wb_ref in wb_refs]
    epilogue(accs, pl.program_id(0), extra, outs)


def _matmul(a, ws, *, layer, col_blk0, n_cols, tn, tm, epilogue, extras=(), extra_specs=(),
            out_shape, out_specs, name, a_blk=0):
    M = a.shape[0]
    K = ws[0].shape[1]
    grid = (n_cols // tn, M // tm)
    in_specs = [pl.BlockSpec((tm, K), lambda n, m: (m, a_blk))]
    for _ in ws:
        in_specs.append(pl.BlockSpec((None, K, tn), lambda n, m: (layer, 0, n + col_blk0)))
    in_specs += list(extra_specs)
    body = functools.partial(_mm_body, n_w=len(ws), n_extra=len(extras), n_out=len(out_shape),
                             epilogue=epilogue)
    return _call(body, grid=grid, in_specs=in_specs, out_specs=out_specs, out_shape=out_shape,
                 scratch=[pltpu.VMEM((K, tn), BF16) for _ in ws], name=name)(a, *ws, *extras)


def _mod_body(c_ref, w_ref, b_ref, o_ref):
    c = c_ref[...]
    c_act = (c * jax.nn.sigmoid(c)).astype(BF16)
    o_ref[...] = jnp.dot(c_act, w_ref[...].astype(BF16), preferred_element_type=F32) + b_ref[...]


def _modulation(c, w_ada, b_ada):
    L, D, N = w_ada.shape
    B = c.shape[0]
    rows = 8
    c_pad = jnp.zeros((rows, D), F32).at[:B].set(c)
    tn = _tile(N, 1024)
    out = _call(
        _mod_body, grid=(L, N // tn),
        in_specs=[pl.BlockSpec((rows, D), lambda l, n: (0, 0)),
                  pl.BlockSpec((None, D, tn), lambda l, n: (l, 0, n)),
                  pl.BlockSpec((None, 1, tn), lambda l, n: (l, 0, n))],
        out_specs=pl.BlockSpec((None, rows, tn), lambda l, n: (l, 0, n)),
        out_shape=jax.ShapeDtypeStruct((L, rows, N), F32),
        name="adaln_mod")(c_pad, w_ada, b_ada.reshape(L, 1, N))
    return out[:, :B, :]


def _prenorm_body(x_ref, g_ref, sc_ref, sh_ref, o_ref):
    o_ref[...] = _norm_mod(x_ref[...], g_ref[...], sc_ref[...], sh_ref[...]).astype(o_ref.dtype)


def _prenorm(x, gain, sc, sh, S, out_dtype):
    M, D = x.shape
    tm = _tile(S, 512)
    bpt = S // tm
    vec = pl.BlockSpec((None, 1, D), lambda m: (m // bpt, 0, 0))
    return _call(
        _prenorm_body, grid=(M // tm,),
        in_specs=[pl.BlockSpec((tm, D), lambda m: (m, 0)),
                  pl.BlockSpec((1, D), lambda m: (0, 0)), vec, vec],
        out_specs=pl.BlockSpec((tm, D), lambda m: (m, 0)),
        out_shape=jax.ShapeDtypeStruct((M, D), out_dtype),
        name="prenorm")(x, gain, sc, sh)


def _softplus(z):
    return jnp.maximum(z, 0.0) + jnp.log(1.0 + jnp.exp(-jnp.abs(z)))


def _sb_block(q, k, v, tri, carry, acc, mask):
    z = lax.dot_general(q, k, (((1,), (1,)), ((), ())), preferred_element_type=F32)
    sp = _softplus(z)
    if mask is not None:
        sp = jnp.where(mask, sp, 0.0)
    hi = sp.astype(BF16)
    lo = (sp - hi.astype(F32)).astype(BF16)
    csum = (jnp.dot(hi, tri, preferred_element_type=F32)
            + jnp.dot(lo, tri, preferred_element_type=F32))
    a = jnp.exp(z - csum - carry)
    if mask is not None:
        a = jnp.where(mask, a, 0.0)
    acc = acc + jnp.dot(a.astype(BF16), v, preferred_element_type=F32)
    return carry + csum[:, 0:1], acc


def _sb_body(q_ref, k_ref, v_ref, o_ref, *, t):
    i = pl.program_id(2)
    q = q_ref[...]
    r = lax.broadcasted_iota(jnp.int32, (t, t), 0)
    c = lax.broadcasted_iota(jnp.int32, (t, t), 1)
    tri = jnp.where(r >= c, 1.0, 0.0).astype(BF16)
    start = pl.multiple_of(i * t, t)
    carry, acc = _sb_block(q, k_ref[pl.ds(start, t), :], v_ref[pl.ds(start, t), :], tri,
                           jnp.zeros((t, 1), F32), jnp.zeros((t, HEAD_DIM), F32), c < r)

    def step(jj, state):
        s0 = pl.multiple_of((i - 1 - jj) * t, t)
        return _sb_block(q, k_ref[pl.ds(s0, t), :], v_ref[pl.ds(s0, t), :], tri,
                         state[0], state[1], None)

    carry, acc = lax.fori_loop(0, i, step, (carry, acc))
    o_ref[...] = acc.astype(o_ref.dtype)


def _sb_attention(qkv, B, S):
    t = _tile(S, 256)
    nq = S // t
    H = SB_HEADS
    return _call(
        functools.partial(_sb_body, t=t), grid=(B, H, nq),
        in_specs=[pl.BlockSpec((t, HEAD_DIM), lambda b, h, i: (b * nq + i, h)),
                  pl.BlockSpec((S, HEAD_DIM), lambda b, h, i: (b, H + h)),
                  pl.BlockSpec((S, HEAD_DIM), lambda b, h, i: (b, 2 * H + h))],
        out_specs=pl.BlockSpec((t, HEAD_DIM), lambda b, h, i: (b * nq + i, h)),
        out_shape=jax.ShapeDtypeStruct((B * S, SB_WIDTH), BF16),
        name="sb_attention")(qkv, qkv, qkv)


def _mla_block(q, k, v, m, l, acc, mask):
    s = lax.dot_general(q, k, (((1,), (1,)), ((), ())), preferred_element_type=F32)
    if mask is not None:
        s = jnp.where(mask, s, -jnp.inf)
    m_new = jnp.maximum(m, jnp.max(s, axis=-1, keepdims=True))
    alpha = jnp.exp(m - m_new)
    p = jnp.exp(s - m_new)
    l = alpha * l + jnp.sum(p, axis=-1, keepdims=True)
    acc = alpha * acc + jnp.dot(p.astype(BF16), v, preferred_element_type=F32)
    return m_new, l, acc


def _mla_body(q_ref, kn_ref, kpe_ref, v_ref, o_ref, kf_ref, *, t):
    i = pl.program_id(2)

    @pl.when(i == 0)
    def _():
        kf_ref[:, 0:LANE] = kn_ref[...]
        kf_ref[:, LANE:QPAD] = kpe_ref[...]

    q = q_ref[...]
    r = lax.broadcasted_iota(jnp.int32, (t, t), 0)
    c = lax.broadcasted_iota(jnp.int32, (t, t), 1)
    start = pl.multiple_of(i * t, t)
    state = _mla_block(q, kf_ref[pl.ds(start, t), :], v_ref[pl.ds(start, t), :],
                       jnp.full((t, 1), -jnp.inf, F32), jnp.zeros((t, 1), F32),
                       jnp.zeros((t, MLA_V), F32), c <= r)

    def step(j, st):
        s0 = pl.multiple_of(j * t, t)
        return _mla_block(q, kf_ref[pl.ds(s0, t), :], v_ref[pl.ds(s0, t), :], *st, None)

    m, l, acc = lax.fori_loop(0, i, step, state)
    o_ref[...] = (acc / l).astype(o_ref.dtype)


def _mla_attention(q_full, kv, kpe, B, S):
    t = _tile(S, 256)
    nq = S // t
    H = MLA_HEADS
    return _call(
        functools.partial(_mla_body, t=t), grid=(B, H, nq),
        in_specs=[pl.BlockSpec((t, QPAD), lambda b, h, i: (b * nq + i, h)),
                  pl.BlockSpec((S, MLA_NOPE), lambda b, h, i: (b, 2 * h)),
                  pl.BlockSpec((S, LANE), lambda b, h, i: (b, 0)),
                  pl.BlockSpec((S, MLA_V), lambda b, h, i: (b, 2 * h + 1))],
        out_specs=pl.BlockSpec((t, MLA_V), lambda b, h, i: (b * nq + i, h)),
        out_shape=jax.ShapeDtypeStruct((B * S, H * MLA_V), BF16),
        scratch=[pltpu.VMEM((S, QPAD), BF16)],
        name="mla_attention")(q_full, kv, kpe, kv)


def _merge_body(osb_ref, omla_ref, gate_sb_ref, gate_mla_ref, wsb_ref, wmla_ref, wo_ref, x_ref,
                g1_ref, gain_ref, sc_ref, sh_ref, xo_ref, ho_ref):
    y = (gate_sb_ref[...].astype(F32)
         * jnp.dot(osb_ref[...], wsb_ref[...], preferred_element_type=F32)
         + gate_mla_ref[...].astype(F32)
         * jnp.dot(omla_ref[...], wmla_ref[...], preferred_element_type=F32))
    out = jnp.dot(y.astype(BF16), wo_ref[...], preferred_element_type=F32)
    x_new = x_ref[...] + g1_ref[...] * out
    xo_ref[...] = x_new
    ho_ref[...] = _norm_mod(x_new, gain_ref[...], sc_ref[...], sh_ref[...]).astype(ho_ref.dtype)


def _merge(o_sb, o_mla, gates, w_sb, w_mla, w_o, layer, x, g1, gain, sc, sh, S, h_dtype):
    M, D = x.shape
    tm = _tile(S, 256)
    bpt = S // tm
    nd = D // D
    vec = pl.BlockSpec((None, 1, D), lambda m: (m // bpt, 0, 0))
    res = lambda k: pl.BlockSpec((None, k, D), lambda m: (layer, 0, 0),
                                 pipeline_mode=pl.Buffered(1))
    return _call(
        _merge_body, grid=(M // tm,),
        in_specs=[pl.BlockSpec((tm, SB_WIDTH), lambda m: (m, 0)),
                  pl.BlockSpec((tm, MLA_HEADS * MLA_V), lambda m: (m, 0)),
                  pl.BlockSpec((tm, D), lambda m: (m, 0)),
                  pl.BlockSpec((tm, D), lambda m: (m, nd)),
                  res(SB_WIDTH), res(MLA_HEADS * MLA_V), res(D),
                  pl.BlockSpec((tm, D), lambda m: (m, 0)),
                  vec, pl.BlockSpec((1, D), lambda m: (0, 0)), vec, vec],
        out_specs=[pl.BlockSpec((tm, D), lambda m: (m, 0)),
                   pl.BlockSpec((tm, D), lambda m: (m, 0))],
        out_shape=[jax.ShapeDtypeStruct((M, D), F32), jax.ShapeDtypeStruct((M, D), h_dtype)],
        name="merge_out")(o_sb, o_mla, gates, gates, w_sb, w_mla, w_o, x, g1, gain, sc, sh)


def _down_body(h_ref, w_ref, x_ref, g2_ref, gain_ref, sc_ref, sh_ref, xo_ref, ho_ref, acc_ref):
    k = pl.program_id(1)

    @pl.when(k == 0)
    def _():
        acc_ref[...] = jnp.zeros_like(acc_ref)

    acc_ref[...] += jnp.dot(h_ref[...], w_ref[...].astype(BF16), preferred_element_type=F32)

    @pl.when(k == pl.num_programs(1) - 1)
    def _():
        x_new = x_ref[...] + g2_ref[...] * acc_ref[...]
        xo_ref[...] = x_new
        ho_ref[...] = _norm_mod(x_new, gain_ref[...], sc_ref[...], sh_ref[...]).astype(ho_ref.dtype)


def _ffn_down(hmid, w_down, layer, x, g2, gain, sc, sh, S, h_dtype):
    M, D = x.shape
    F = hmid.shape[1]
    tm = _tile(S, 512)
    tk = _tile(F, 512)
    bpt = S // tm
    vec = pl.BlockSpec((None, 1, D), lambda m, k: (m // bpt, 0, 0))
    return _call(
        _down_body, grid=(M // tm, F // tk),
        in_specs=[pl.BlockSpec((tm, tk), lambda m, k: (m, k)),
                  pl.BlockSpec((None, tk, D), lambda m, k: (layer, k, 0)),
                  pl.BlockSpec((tm, D), lambda m, k: (m, 0)),
                  vec, pl.BlockSpec((1, D), lambda m, k: (0, 0)), vec, vec],
        out_specs=[pl.BlockSpec((tm, D), lambda m, k: (m, 0)),
                   pl.BlockSpec((tm, D), lambda m, k: (m, 0))],
        out_shape=[jax.ShapeDtypeStruct((M, D), F32), jax.ShapeDtypeStruct((M, D), h_dtype)],
        scratch=[pltpu.VMEM((tm, D), F32)],
        name="ffn_down")(hmid, w_down, x, g2, gain, sc, sh)


def _split_bf16(v):
    hi = v.astype(BF16)
    return hi, (v - hi.astype(F32)).astype(BF16)


def _router_body(h_ref, w_ref, idx_ref, wt_ref):
    hh, hl = _split_bf16(h_ref[...])
    wh, wl = _split_bf16(w_ref[...])
    dot = functools.partial(jnp.dot, preferred_element_type=F32)
    logits = dot(hh, wh) + (dot(hl, wh) + dot(hh, wl))
    lane = lax.broadcasted_iota(jnp.int32, logits.shape, 1).astype(F32)
    logits = jnp.where(lane < N_EXPERTS, logits, -jnp.inf)
    m1 = jnp.max(logits, axis=-1, keepdims=True)
    i1 = jnp.min(jnp.where(logits == m1, lane, float(LANE)), axis=-1, keepdims=True)
    rest = jnp.where(lane == i1, -jnp.inf, logits)
    m2 = jnp.max(rest, axis=-1, keepdims=True)
    i2 = jnp.min(jnp.where(rest == m2, lane, float(LANE)), axis=-1, keepdims=True)
    e = jnp.exp(m2 - m1)
    w1 = 1.0 / (1.0 + e)
    w2 = e * w1
    idx_ref[...] = jnp.where(lane == 0.0, i1, jnp.where(lane == 1.0, i2, 0.0)).astype(jnp.int32)
    wt_ref[...] = jnp.where(lane == 0.0, w1, jnp.where(lane == 1.0, w2, 0.0))


def _router(h, w_router_pad, j):
    M, D = h.shape
    tm = _tile(M, 512)
    return _call(
        _router_body, grid=(M // tm,),
        in_specs=[pl.BlockSpec((tm, D), lambda m: (m, 0)),
                  pl.BlockSpec((None, D, LANE), lambda m: (j, 0, 0))],
        out_specs=[pl.BlockSpec((tm, LANE), lambda m: (m, 0)),
                   pl.BlockSpec((tm, LANE), lambda m: (m, 0))],
        out_shape=[jax.ShapeDtypeStruct((M, LANE), jnp.int32),
                   jax.ShapeDtypeStruct((M, LANE), F32)],
        name="router")(h, w_router_pad)


def _gather_body(src_ref, h_hbm, o_ref, buf_ref, sem, *, tm):
    t = pl.program_id(0)
    base = t * tm

    def row_copy(r):
        return pltpu.make_async_copy(h_hbm.at[pl.ds(src_ref[base + r], 1)],
                                     buf_ref.at[pl.ds(r, 1)], sem)

    def issue(r, _):
        row_copy(r).start()
        return 0

    lax.fori_loop(0, tm, issue, 0)

    def drain(r, _):
        row_copy(r).wait()
        return 0

    lax.fori_loop(0, tm, drain, 0)
    o_ref[...] = buf_ref[...].astype(o_ref.dtype)


def _gather_rows(src_tok, h, n_tiles, tm):
    M, D = h.shape
    return _call(
        functools.partial(_gather_body, tm=tm), grid=(n_tiles,), nsp=1,
        in_specs=[pl.BlockSpec(memory_space=pl.ANY)],
        out_specs=pl.BlockSpec((tm, D), lambda t, src: (t, 0)),
        out_shape=jax.ShapeDtypeStruct((n_tiles * tm, D), BF16),
        scratch=[pltpu.VMEM((tm, D), F32), pltpu.SemaphoreType.DMA(())],
        name="moe_gather")(src_tok, h)


def _grouped_body(te_ref, a_ref, *refs, n_w, epilogue):
    w_refs = refs[:n_w]
    rest = refs[n_w:]
    n_rest = len(rest) - n_w
    wb_refs = rest[n_rest:]
    t = pl.program_id(1)
    changed = jnp.logical_or(t == 0, te_ref[t] != te_ref[jnp.maximum(t - 1, 0)])

    @pl.when(changed)
    def _():
        for w_ref, wb_ref in zip(w_refs, wb_refs):
            wb_ref[...] = w_ref[...].astype(BF16)

    a = a_ref[...]
    accs = [jnp.dot(a, wb_ref[...], preferred_element_type=F32) for wb_ref in wb_refs]
    epilogue(accs, rest[:n_rest])


def _up_epilogue(accs, refs):
    g, u = accs
    refs[0][...] = (g * jax.nn.sigmoid(g) * u).astype(refs[0].dtype)


def _grouped_up(tile_e, xs, w_gate, w_up, j, tm, tf):
    NP, D = xs.shape
    F = w_gate.shape[-1]
    wspec = pl.BlockSpec((None, None, D, tf), lambda f, t, te: (j, te[t], 0, f))
    return _call(
        functools.partial(_grouped_body, n_w=2, epilogue=_up_epilogue),
        grid=(F // tf, NP // tm), nsp=1,
        in_specs=[pl.BlockSpec((tm, D), lambda f, t, te: (t, 0)), wspec, wspec],
        out_specs=pl.BlockSpec((tm, tf), lambda f, t, te: (t, f)),
        out_shape=jax.ShapeDtypeStruct((NP, F), BF16),
        scratch=[pltpu.VMEM((D, tf), BF16), pltpu.VMEM((D, tf), BF16)],
        name="moe_up")(tile_e, xs, w_gate, w_up)


def _down_epilogue(accs, refs):
    roww_ref, o_ref = refs
    o_ref[...] = accs[0] * roww_ref[...]


def _grouped_down(tile_e, hs, w_down, row_w, j, tm, tn):
    NP, F = hs.shape
    D = w_down.shape[-1]
    return _call(
        functools.partial(_grouped_body, n_w=1, epilogue=_down_epilogue),
        grid=(D // tn, NP // tm), nsp=1,
        in_specs=[pl.BlockSpec((tm, F), lambda n, t, te: (t, 0)),
                  pl.BlockSpec((None, None, F, tn), lambda n, t, te: (j, te[t], 0, n)),
                  pl.BlockSpec((tm, 1), lambda n, t, te: (t, 0))],
        out_specs=pl.BlockSpec((tm, tn), lambda n, t, te: (t, n)),
        out_shape=jax.ShapeDtypeStruct((NP, D), F32),
        scratch=[pltpu.VMEM((F, tn), BF16)],
        name="moe_down")(tile_e, hs, w_down, row_w)


def _combine_body(pos_ref, y_hbm, x_ref, g2_ref, gain_ref, sc_ref, sh_ref, xo_ref, ho_ref,
                  buf_ref, sem, *, tm):
    base = pl.program_id(0) * tm * TOP_K

    def row_copy(r, k):
        return pltpu.make_async_copy(y_hbm.at[pl.ds(pos_ref[base + TOP_K * r + k], 1)],
                                     buf_ref.at[k, pl.ds(r, 1)], sem)

    def issue(r, _):
        for k in range(TOP_K):
            row_copy(r, k).start()
        return 0

    lax.fori_loop(0, tm, issue, 0)

    def drain(r, _):
        for k in range(TOP_K):
            row_copy(r, k).wait()
        return 0

    lax.fori_loop(0, tm, drain, 0)
    f = buf_ref[0] + buf_ref[1]
    x_new = x_ref[...] + g2_ref[...] * f
    xo_ref[...] = x_new
    ho_ref[...] = _norm_mod(x_new, gain_ref[...], sc_ref[...], sh_ref[...]).astype(ho_ref.dtype)


def _combine(pos, y_sorted, x, g2, gain, sc, sh, S, h_dtype):
    M, D = x.shape
    tm = _tile(S, 256)
    bpt = S // tm
    vec = pl.BlockSpec((None, 1, D), lambda m, p: (m // bpt, 0, 0))
    return _call(
        functools.partial(_combine_body, tm=tm), grid=(M // tm,), nsp=1,
        in_specs=[pl.BlockSpec(memory_space=pl.ANY),
                  pl.BlockSpec((tm, D), lambda m, p: (m, 0)),
                  vec, pl.BlockSpec((1, D), lambda m, p: (0, 0)), vec, vec],
        out_specs=[pl.BlockSpec((tm, D), lambda m, p: (m, 0)),
                   pl.BlockSpec((tm, D), lambda m, p: (m, 0))],
        out_shape=[jax.ShapeDtypeStruct((M, D), F32), jax.ShapeDtypeStruct((M, D), h_dtype)],
        scratch=[pltpu.VMEM((TOP_K, tm, D), F32), pltpu.SemaphoreType.DMA(())],
        name="moe_combine")(pos, y_sorted, x, g2, gain, sc, sh)


def _dispatch_plan(top_i, top_w, tm):
    M = top_i.shape[0]
    E = N_EXPERTS
    n_ent = M * TOP_K
    e_flat = top_i.reshape(-1)
    onehot = (e_flat[:, None] == jnp.arange(E, dtype=jnp.int32)[None, :]).astype(jnp.int32)
    csum = jnp.cumsum(onehot, axis=0)
    rank = jnp.sum(onehot * csum, axis=1) - 1
    counts = csum[-1]
    pcounts = ((counts + tm - 1) // tm) * tm
    pend = jnp.cumsum(pcounts)
    pstart = pend - pcounts
    dest = (pstart[e_flat] + rank).astype(jnp.int32)
    n_tiles = n_ent // tm + E
    n_rows = n_tiles * tm
    src_tok = jnp.zeros((n_rows,), jnp.int32).at[dest].set(
        jnp.arange(n_ent, dtype=jnp.int32) // TOP_K)
    row_w = jnp.zeros((n_rows,), F32).at[dest].set(top_w.reshape(-1))
    tile_start = jnp.arange(n_tiles, dtype=jnp.int32) * tm
    tile_e = jnp.sum((tile_start[:, None] >= pend[None, :]).astype(jnp.int32), axis=1)
    tile_e = jnp.minimum(tile_e, E - 1).astype(jnp.int32)
    return src_tok, row_w.reshape(n_rows, 1), dest, tile_e, n_tiles


def _rot_cols(w):
    half = MLA_ROPE // 2
    return jnp.concatenate([-w[..., half:], w[..., :half]], axis=-1)


def _rope_tables(positions):
    inv_freq = 1.0 / (ROPE_THETA ** (jnp.arange(0, MLA_ROPE, 2, dtype=F32) / MLA_ROPE))
    ang = positions.astype(F32)[..., None] * inv_freq
    cos, sin = jnp.cos(ang), jnp.sin(ang)
    M = positions.size
    cos2 = jnp.concatenate([cos, cos], axis=-1).reshape(M, MLA_ROPE)
    sin2 = jnp.concatenate([sin, sin], axis=-1).reshape(M, MLA_ROPE)
    zeros = jnp.zeros((M, LANE - MLA_ROPE), F32)
    cos_k = jnp.concatenate([cos2, zeros], axis=-1)
    sin_k = jnp.concatenate([sin2, zeros], axis=-1)
    scale = MLA_QK ** -0.5
    cos_q = jnp.concatenate([jnp.full((M, MLA_NOPE), scale, F32), cos_k * scale], axis=-1)
    sin_q = jnp.concatenate([jnp.zeros((M, MLA_NOPE), F32), sin_k * scale], axis=-1)
    return cos_k, sin_k, cos_q, sin_q


def kernel(x, c, positions, w_ada, b_ada, norm_mix_g, norm_ffn_g, w_in, q_norm_g, kv_norm_g, w_uq, w_ukv, w_sb_up, w_mla_up, w_o, w_ffn_gate, w_ffn_up, w_ffn_down, w_router, w_exp_gate, w_exp_up, w_exp_down, final_norm_g):
    B, S, D = x.shape
    L = w_ada.shape[0]
    M = B * S
    x = x.reshape(M, D)

    mod = _modulation(c, w_ada, b_ada)
    mod = mod.reshape(L, B, N_MOD, 1, D)
    sh1, sc1, g1, sh2, sc2, g2 = [mod[:, :, k] for k in range(N_MOD)]
    cos_k, sin_k, cos_q, sin_q = _rope_tables(positions)

    pe0 = 3 * SB_WIDTH + Q_RANK + KV_RANK
    w_pe = w_in[:, :, pe0:pe0 + MLA_ROPE]
    zpad = jnp.zeros(w_pe.shape[:2] + (LANE - MLA_ROPE,), F32)
    w_pe2 = jnp.concatenate([w_pe, zpad, _rot_cols(w_pe), zpad], axis=-1).astype(BF16)
    w_gates = w_in[:, :, pe0 + MLA_ROPE:].astype(BF16)
    c_gain = jnp.concatenate([q_norm_g, kv_norm_g], axis=-1)[:, None, :]

    wq = w_uq.reshape(L, Q_RANK, MLA_HEADS, MLA_QK)
    wq_n, wq_r = wq[..., :MLA_NOPE], wq[..., MLA_NOPE:]
    zq = jnp.zeros(wq_r.shape, F32)
    wq_main = jnp.concatenate([wq_n, wq_r, zq], axis=-1)
    wq_rot = jnp.concatenate([jnp.zeros(wq_n.shape, F32), _rot_cols(wq_r), zq], axis=-1)
    wq_both = jnp.concatenate([wq_main, wq_rot], axis=-1).reshape(
        L, Q_RANK, MLA_HEADS * 2 * QPAD).astype(BF16)

    w_sb_b, w_mla_b, w_o_b = (w.astype(BF16) for w in (w_sb_up, w_mla_up, w_o))
    w_router_pad = jnp.concatenate(
        [w_router, jnp.zeros(w_router.shape[:2] + (LANE - N_EXPERTS,), F32)], axis=-1)

    tm = _tile(S, 1024)
    bpt = S // tm
    row = lambda width: pl.BlockSpec((tm, width), lambda n, m: (m, 0))
    sb_scale = HEAD_DIM ** -0.5

    def h_dtype_for(layer_is_moe):
        return F32 if layer_is_moe else BF16

    h = _prenorm(x, norm_mix_g[0][None, :], sc1[0], sh1[0], S, BF16)

    for l in range(L):
        tn = 512
        nq_blk = SB_WIDTH // tn

        def qkv_epi(accs, n, extra, outs):
            s = jnp.where(n < nq_blk, sb_scale, 1.0)
            outs[0][...] = (accs[0] * s).astype(BF16)

        (qkv,) = _matmul(h, [w_in], layer=l, col_blk0=0, n_cols=3 * SB_WIDTH, tn=tn, tm=tm,
                         epilogue=qkv_epi,
                         out_shape=[jax.ShapeDtypeStruct((M, 3 * SB_WIDTH), BF16)],
                         out_specs=[pl.BlockSpec((tm, tn), lambda n, m: (m, n))], name="in_qkv")

        def lat_epi(accs, n, extra, outs):
            a = accs[0]
            ms = jnp.mean(a * a, axis=-1, keepdims=True)
            outs[0][...] = (a * lax.rsqrt(ms + NORM_EPS) * extra[0][...]).astype(BF16)

        (lat,) = _matmul(h, [w_in], layer=l, col_blk0=3 * SB_WIDTH // Q_RANK,
                         n_cols=Q_RANK + KV_RANK, tn=Q_RANK, tm=tm, epilogue=lat_epi,
                         extras=[c_gain],
                         extra_specs=[pl.BlockSpec((None, 1, Q_RANK), lambda n, m: (l, 0, n))],
                         out_shape=[jax.ShapeDtypeStruct((M, Q_RANK + KV_RANK), BF16)],
                         out_specs=[pl.BlockSpec((tm, Q_RANK), lambda n, m: (m, n))],
                         name="in_latent")

        def pe_epi(accs, n, extra, outs):
            a = accs[0]
            outs[0][...] = (a[:, :LANE] * extra[0][...] + a[:, LANE:] * extra[1][...]).astype(BF16)

        (kpe,) = _matmul(h, [w_pe2], layer=l, col_blk0=0, n_cols=2 * LANE, tn=2 * LANE, tm=tm,
                         epilogue=pe_epi, extras=[cos_k, sin_k], extra_specs=[row(LANE), row(LANE)],
                         out_shape=[jax.ShapeDtypeStruct((M, LANE), BF16)],
                         out_specs=[pl.BlockSpec((tm, LANE), lambda n, m: (m, 0))], name="in_kpe")

        def gate_epi(accs, n, extra, outs):
            outs[0][...] = jax.nn.sigmoid(accs[0]).astype(BF16)

        (gates,) = _matmul(h, [w_gates], layer=l, col_blk0=0, n_cols=2 * D, tn=tn, tm=tm,
                           epilogue=gate_epi,
                           out_shape=[jax.ShapeDtypeStruct((M, 2 * D), BF16)],
                           out_specs=[pl.BlockSpec((tm, tn), lambda n, m: (m, n))],
                           name="in_gates")

        def q_epi(accs, n, extra, outs):
            a = accs[0]
            outs[0][...] = (a[:, :QPAD] * extra[0][...] + a[:, QPAD:] * extra[1][...]).astype(BF16)

        (q_full,) = _matmul(lat, [wq_both], layer=l, col_blk0=0, n_cols=MLA_HEADS * 2 * QPAD,
                            tn=2 * QPAD, tm=tm, epilogue=q_epi, extras=[cos_q, sin_q],
                            extra_specs=[row(QPAD), row(QPAD)],
                            out_shape=[jax.ShapeDtypeStruct((M, MLA_HEADS * QPAD), BF16)],
                            out_specs=[pl.BlockSpec((tm, QPAD), lambda n, m: (m, n))],
                            name="q_up")

        def plain_epi(accs, n, extra, outs):
            outs[0][...] = accs[0].astype(BF16)

        kv_cols = MLA_HEADS * (MLA_NOPE + MLA_V)
        (kv,) = _matmul(lat, [w_ukv], layer=l, col_blk0=0, n_cols=kv_cols, tn=tn, tm=tm,
                        epilogue=plain_epi, a_blk=1,
                        out_shape=[jax.ShapeDtypeStruct((M, kv_cols), BF16)],
                        out_specs=[pl.BlockSpec((tm, tn), lambda n, m: (m, n))], name="kv_up")

        o_sb = _sb_attention(qkv, B, S)
        o_mla = _mla_attention(q_full, kv, kpe, B, S)

        moe = (l % 2 == 1)
        x, h = _merge(o_sb, o_mla, gates, w_sb_b, w_mla_b, w_o_b, l, x, g1[l],
                      norm_ffn_g[l][None, :], sc2[l], sh2[l], S, h_dtype_for(moe))

        last = (l == L - 1)
        if last:
            n_gain = final_norm_g[None, :]
            n_sc = jnp.zeros_like(sc1[0])
            n_sh = jnp.zeros_like(sh1[0])
            n_dtype = F32
        else:
            n_gain, n_sc, n_sh, n_dtype = norm_mix_g[l + 1][None, :], sc1[l + 1], sh1[l + 1], BF16
        j = l // 2
        if not moe:
            F = w_ffn_gate.shape[-1]
            tf = _tile(F, 512)

            def up_epi(accs, n, extra, outs):
                _up_epilogue(accs, outs)

            (hmid,) = _matmul(h, [w_ffn_gate, w_ffn_up], layer=j, col_blk0=0, n_cols=F, tn=tf,
                              tm=tm, epilogue=up_epi,
                              out_shape=[jax.ShapeDtypeStruct((M, F), BF16)],
                              out_specs=[pl.BlockSpec((tm, tf), lambda n, m: (m, n))],
                              name="ffn_up")
            x, h = _ffn_down(hmid, w_ffn_down, j, x, g2[l], n_gain, n_sc, n_sh, S, n_dtype)
        else:
            F = w_exp_gate.shape[-1]
            tme = _tile(M, 256)
            idx, wts = _router(h, w_router_pad, j)
            src_tok, row_w, dest, tile_e, n_tiles = _dispatch_plan(
                idx[:, :TOP_K], wts[:, :TOP_K], tme)
            xs = _gather_rows(src_tok, h, n_tiles, tme)
            hs = _grouped_up(tile_e, xs, w_exp_gate, w_exp_up, j, tme, _tile(F, 512))
            ys = _grouped_down(tile_e, hs, w_exp_down, row_w, j, tme, _tile(D, 512))
            x, h = _combine(dest, ys, x, g2[l], n_gain, n_sc, n_sh, S, n_dtype)

    return h.reshape(B, S, D)
```

```python
import functools

import jax
import jax.numpy as jnp
from jax import lax
from jax.experimental import pallas as pl
from jax.experimental.pallas import tpu as pltpu

BF16 = jnp.bfloat16
F32 = jnp.float32

SB_HEADS = 8
HEAD_DIM = 128
SB_WIDTH = SB_HEADS * HEAD_DIM
MLA_HEADS = 8
MLA_NOPE = 128
MLA_ROPE = 64
MLA_V = 128
MLA_QK = MLA_NOPE + MLA_ROPE
Q_RANK = 512
KV_RANK = 512
ROPE_THETA = 10000.0
N_EXPERTS = 8
TOP_K = 2
N_MOD = 6
NORM_EPS = 1e-6

LANE = 128
QPAD = 2 * LANE
VMEM_LIMIT = 56 << 20


def _call(body, *, grid, in_specs, out_specs, out_shape, scratch=(), nsp=0, name):
    return pl.pallas_call(
        body,
        grid_spec=pltpu.PrefetchScalarGridSpec(
            num_scalar_prefetch=nsp, grid=grid, in_specs=in_specs,
            out_specs=out_specs, scratch_shapes=scratch),
        out_shape=out_shape,
        compiler_params=pltpu.CompilerParams(
            dimension_semantics=("arbitrary",) * len(grid),
            vmem_limit_bytes=VMEM_LIMIT),
        name=name)


def _tile(n, pref):
    t = min(n, pref)
    while n % t:
        t //= 2
    return t


def _norm_mod(x, gain, sc, sh):
    ms = jnp.mean(x * x, axis=-1, keepdims=True)
    y = x * lax.rsqrt(ms + NORM_EPS)
    return (y * gain) * (1.0 + sc) + sh


def _mm_body(*refs, n_w, n_extra, n_out, epilogue):
    a_ref = refs[0]
    w_refs = refs[1:1 + n_w]
    extra = refs[1 + n_w:1 + n_w + n_extra]
    outs = refs[1 + n_w + n_extra:1 + n_w + n_extra + n_out]
    wb_refs = refs[1 + n_w + n_extra + n_out:]

    @pl.when(pl.program_id(1) == 0)
    def _():
        for w_ref, wb_ref in zip(w_refs, wb_refs):
            wb_ref[...] = w_ref[...].astype(BF16)

    a = a_ref[...]
    accs = [jnp.dot(a, wb_ref[...], preferred_element_type=F32) for wb_ref in wb_refs]
    epilogue(accs, pl.program_id(0), extra, outs)


def _matmul(a, ws, *, layer, col_blk0, n_cols, tn, tm, epilogue, extras=(), extra_specs=(),
            out_shape, out_specs, name, a_blk=0):
    M = a.shape[0]
    K = ws[0].shape[1]
    grid = (n_cols // tn, M // tm)
    in_specs = [pl.BlockSpec((tm, K), lambda n, m: (m, a_blk))]
    for _ in ws:
        in_specs.append(pl.BlockSpec((None, K, tn), lambda n, m: (layer, 0, n + col_blk0)))
    in_specs += list(extra_specs)
    body = functools.partial(_mm_body, n_w=len(ws), n_extra=len(extras), n_out=len(out_shape),
                             epilogue=epilogue)
    return _call(body, grid=grid, in_specs=in_specs, out_specs=out_specs, out_shape=out_shape,
                 scratch=[pltpu.VMEM((K, tn), BF16) for _ in ws], name=name)(a, *ws, *extras)


def _mod_body(c_ref, w_ref, b_ref, o_ref):
    c = c_ref[...]
    c_act = (c * jax.nn.sigmoid(c)).astype(BF16)
    o_ref[...] = jnp.dot(c_act, w_ref[...].astype(BF16), preferred_element_type=F32) + b_ref[...]


def _modulation(c, w_ada, b_ada):
    L, D, N = w_ada.shape
    B = c.shape[0]
    rows = 8
    c_pad = jnp.zeros((rows, D), F32).at[:B].set(c)
    tn = _tile(N, 1024)
    out = _call(
        _mod_body, grid=(L, N // tn),
        in_specs=[pl.BlockSpec((rows, D), lambda l, n: (0, 0)),
                  pl.BlockSpec((None, D, tn), lambda l, n: (l, 0, n)),
                  pl.BlockSpec((None, 1, tn), lambda l, n: (l, 0, n))],
        out_specs=pl.BlockSpec((None, rows, tn), lambda l, n: (l, 0, n)),
        out_shape=jax.ShapeDtypeStruct((L, rows, N), F32),
        name="adaln_mod")(c_pad, w_ada, b_ada.reshape(L, 1, N))
    return out[:, :B, :]


def _prenorm_body(x_ref, g_ref, sc_ref, sh_ref, o_ref):
    o_ref[...] = _norm_mod(x_ref[...], g_ref[...], sc_ref[...], sh_ref[...]).astype(o_ref.dtype)


def _prenorm(x, gain, sc, sh, S, out_dtype):
    M, D = x.shape
    tm = _tile(S, 512)
    bpt = S // tm
    vec = pl.BlockSpec((None, 1, D), lambda m: (m // bpt, 0, 0))
    return _call(
        _prenorm_body, grid=(M // tm,),
        in_specs=[pl.BlockSpec((tm, D), lambda m: (m, 0)),
                  pl.BlockSpec((1, D), lambda m: (0, 0)), vec, vec],
        out_specs=pl.BlockSpec((tm, D), lambda m: (m, 0)),
        out_shape=jax.ShapeDtypeStruct((M, D), out_dtype),
        name="prenorm")(x, gain, sc, sh)


SUB = 256
ATT_HEADS = 2
LOG2E = 1.4426950408889634
SP_CLAMP = 40.0


def _qk(q, k):
    return lax.dot_general(q, k, (((1,), (1,)), ((), ())), preferred_element_type=F32)


def _sb_chain(q, k, v, tri, carry, acc, mask):
    z = _qk(q, k)
    sp = jnp.maximum(jnp.log2(1.0 + jnp.exp2(jnp.minimum(z, SP_CLAMP))), z)
    if mask is not None:
        sp = jnp.where(mask, sp, 0.0)
    hi = sp.astype(BF16)
    lo = (sp - hi.astype(F32)).astype(BF16)
    parts = []
    for c in reversed(range(k.shape[0] // SUB)):
        sl = slice(c * SUB, (c + 1) * SUB)
        loc = (jnp.dot(hi[:, sl], tri, preferred_element_type=F32)
               + jnp.dot(lo[:, sl], tri, preferred_element_type=F32))
        parts.insert(0, loc + carry)
        carry = carry + loc[:, 0:1]
    csum = parts[0] if len(parts) == 1 else jnp.concatenate(parts, axis=1)
    a = jnp.exp2(z - csum)
    if mask is not None:
        a = jnp.where(mask, a, 0.0)
    return carry, acc + jnp.dot(a.astype(BF16), v, preferred_element_type=F32)


def _sb_body(q_ref, k_ref, v_ref, o_ref, *, t):
    i = pl.program_id(2)
    halves = t // SUB
    r = lax.broadcasted_iota(jnp.int32, (SUB, SUB), 0)
    c = lax.broadcasted_iota(jnp.int32, (SUB, SUB), 1)
    tri = jnp.where(r >= c, 1.0, 0.0).astype(BF16)
    start = pl.multiple_of(i * t, t)
    chains = [(g, a) for g in range(ATT_HEADS) for a in range(halves)]

    def operands(g, a, row0, nrows):
        cols = slice(g * HEAD_DIM, (g + 1) * HEAD_DIM)
        return (q_ref[a * SUB:(a + 1) * SUB, cols], k_ref[pl.ds(row0, nrows), cols],
                v_ref[pl.ds(row0, nrows), cols])

    state = []
    for g, a in chains:
        n = (a + 1) * SUB
        rr = lax.broadcasted_iota(jnp.int32, (SUB, n), 0)
        cc = lax.broadcasted_iota(jnp.int32, (SUB, n), 1)
        state.append(_sb_chain(*operands(g, a, start, n), tri, jnp.zeros((SUB, 1), F32),
                               jnp.zeros((SUB, HEAD_DIM), F32), cc < rr + a * SUB))

    def step(jj, st):
        row0 = pl.multiple_of((i - 1 - jj) * t, t)
        return tuple(_sb_chain(*operands(g, a, row0, t), tri, *st[n], None)
                     for n, (g, a) in enumerate(chains))

    state = lax.fori_loop(0, i, step, tuple(state))
    for n, (g, a) in enumerate(chains):
        o_ref[a * SUB:(a + 1) * SUB, g * HEAD_DIM:(g + 1) * HEAD_DIM] = state[n][1].astype(o_ref.dtype)


def _sb_attention(qkv, B, S):
    t = _tile(S, 2 * SUB)
    nq = S // t
    G = ATT_HEADS
    HG = SB_HEADS // G
    W = G * HEAD_DIM
    return _call(
        functools.partial(_sb_body, t=t), grid=(B, HG, nq),
        in_specs=[pl.BlockSpec((t, W), lambda b, h, i: (b * nq + i, h)),
                  pl.BlockSpec((S, W), lambda b, h, i: (b, HG + h)),
                  pl.BlockSpec((S, W), lambda b, h, i: (b, 2 * HG + h))],
        out_specs=pl.BlockSpec((t, W), lambda b, h, i: (b * nq + i, h)),
        out_shape=jax.ShapeDtypeStruct((B * S, SB_WIDTH), BF16),
        name="sb_attention")(qkv, qkv, qkv)


def _mla_chain(q, k, v, m, l, acc, mask):
    s = _qk(q, k)
    if mask is not None:
        s = jnp.where(mask, s, -jnp.inf)
    m_new = jnp.maximum(m, jnp.max(s, axis=-1, keepdims=True))
    alpha = jnp.exp2(m - m_new)
    p = jnp.exp2(s - m_new)
    l = alpha * l + jnp.sum(p, axis=-1, keepdims=True)
    acc = alpha * acc + jnp.dot(p.astype(BF16), v, preferred_element_type=F32)
    return m_new, l, acc


def _mla_body(q_ref, kv_ref, kpe_ref, o_ref, kf_ref, *, t):
    i = pl.program_id(2)
    halves = t // SUB
    hw = MLA_NOPE + MLA_V

    @pl.when(i == 0)
    def _():
        for g in range(ATT_HEADS):
            kf_ref[g, :, 0:LANE] = kv_ref[:, g * hw:g * hw + MLA_NOPE]
            kf_ref[g, :, LANE:QPAD] = kpe_ref[...]

    start = pl.multiple_of(i * t, t)
    chains = [(g, a) for g in range(ATT_HEADS) for a in range(halves)]

    def operands(g, a, row0, nrows):
        return (q_ref[a * SUB:(a + 1) * SUB, g * QPAD:(g + 1) * QPAD],
                kf_ref[g, pl.ds(row0, nrows), :],
                kv_ref[pl.ds(row0, nrows), g * hw + MLA_NOPE:(g + 1) * hw])

    state = []
    for g, a in chains:
        n = (a + 1) * SUB
        rr = lax.broadcasted_iota(jnp.int32, (SUB, n), 0)
        cc = lax.broadcasted_iota(jnp.int32, (SUB, n), 1)
        state.append(_mla_chain(*operands(g, a, start, n), jnp.full((SUB, 1), -jnp.inf, F32),
                                jnp.zeros((SUB, 1), F32), jnp.zeros((SUB, MLA_V), F32),
                                cc <= rr + a * SUB))

    def step(j, st):
        row0 = pl.multiple_of(j * t, t)
        return tuple(_mla_chain(*operands(g, a, row0, t), *st[n], None)
                     for n, (g, a) in enumerate(chains))

    state = lax.fori_loop(0, i, step, tuple(state))
    for n, (g, a) in enumerate(chains):
        m, l, acc = state[n]
        o_ref[a * SUB:(a + 1) * SUB, g * MLA_V:(g + 1) * MLA_V] = (acc / l).astype(o_ref.dtype)


def _mla_attention(q_full, kv, kpe, B, S):
    t = _tile(S, 2 * SUB)
    nq = S // t
    G = ATT_HEADS
    HG = MLA_HEADS // G
    return _call(
        functools.partial(_mla_body, t=t), grid=(B, HG, nq),
        in_specs=[pl.BlockSpec((t, G * QPAD), lambda b, h, i: (b * nq + i, h)),
                  pl.BlockSpec((S, G * (MLA_NOPE + MLA_V)), lambda b, h, i: (b, h)),
                  pl.BlockSpec((S, LANE), lambda b, h, i: (b, 0))],
        out_specs=pl.BlockSpec((t, G * MLA_V), lambda b, h, i: (b * nq + i, h)),
        out_shape=jax.ShapeDtypeStruct((B * S, MLA_HEADS * MLA_V), BF16),
        scratch=[pltpu.VMEM((G, S, QPAD), BF16)],
        name="mla_attention")(q_full, kv, kpe)


def _merge_body(osb_ref, omla_ref, gate_sb_ref, gate_mla_ref, wsb_ref, wmla_ref, wo_ref, x_ref,
                g1_ref, gain_ref, sc_ref, sh_ref, xo_ref, ho_ref):
    y = (gate_sb_ref[...].astype(F32)
         * jnp.dot(osb_ref[...], wsb_ref[...], preferred_element_type=F32)
         + gate_mla_ref[...].astype(F32)
         * jnp.dot(omla_ref[...], wmla_ref[...], preferred_element_type=F32))
    out = jnp.dot(y.astype(BF16), wo_ref[...], preferred_element_type=F32)
    x_new = x_ref[...] + g1_ref[...] * out
    xo_ref[...] = x_new
    ho_ref[...] = _norm_mod(x_new, gain_ref[...], sc_ref[...], sh_ref[...]).astype(ho_ref.dtype)


def _merge(o_sb, o_mla, gates, w_sb, w_mla, w_o, layer, x, g1, gain, sc, sh, S, h_dtype):
    M, D = x.shape
    tm = _tile(S, 256)
    bpt = S // tm
    nd = D // D
    vec = pl.BlockSpec((None, 1, D), lambda m: (m // bpt, 0, 0))
    res = lambda k: pl.BlockSpec((None, k, D), lambda m: (layer, 0, 0),
                                 pipeline_mode=pl.Buffered(1))
    return _call(
        _merge_body, grid=(M // tm,),
        in_specs=[pl.BlockSpec((tm, SB_WIDTH), lambda m: (m, 0)),
                  pl.BlockSpec((tm, MLA_HEADS * MLA_V), lambda m: (m, 0)),
                  pl.BlockSpec((tm, D), lambda m: (m, 0)),
                  pl.BlockSpec((tm, D), lambda m: (m, nd)),
                  res(SB_WIDTH), res(MLA_HEADS * MLA_V), res(D),
                  pl.BlockSpec((tm, D), lambda m: (m, 0)),
                  vec, pl.BlockSpec((1, D), lambda m: (0, 0)), vec, vec],
        out_specs=[pl.BlockSpec((tm, D), lambda m: (m, 0)),
                   pl.BlockSpec((tm, D), lambda m: (m, 0))],
        out_shape=[jax.ShapeDtypeStruct((M, D), F32), jax.ShapeDtypeStruct((M, D), h_dtype)],
        name="merge_out")(o_sb, o_mla, gates, gates, w_sb, w_mla, w_o, x, g1, gain, sc, sh)


def _down_body(h_ref, w_ref, x_ref, g2_ref, gain_ref, sc_ref, sh_ref, xo_ref, ho_ref, acc_ref):
    k = pl.program_id(1)

    @pl.when(k == 0)
    def _():
        acc_ref[...] = jnp.zeros_like(acc_ref)

    acc_ref[...] += jnp.dot(h_ref[...], w_ref[...].astype(BF16), preferred_element_type=F32)

    @pl.when(k == pl.num_programs(1) - 1)
    def _():
        x_new = x_ref[...] + g2_ref[...] * acc_ref[...]
        xo_ref[...] = x_new
        ho_ref[...] = _norm_mod(x_new, gain_ref[...], sc_ref[...], sh_ref[...]).astype(ho_ref.dtype)


def _ffn_down(hmid, w_down, layer, x, g2, gain, sc, sh, S, h_dtype):
    M, D = x.shape
    F = hmid.shape[1]
    tm = _tile(S, 512)
    tk = _tile(F, 512)
    bpt = S // tm
    vec = pl.BlockSpec((None, 1, D), lambda m, k: (m // bpt, 0, 0))
    return _call(
        _down_body, grid=(M // tm, F // tk),
        in_specs=[pl.BlockSpec((tm, tk), lambda m, k: (m, k)),
                  pl.BlockSpec((None, tk, D), lambda m, k: (layer, k, 0)),
                  pl.BlockSpec((tm, D), lambda m, k: (m, 0)),
                  vec, pl.BlockSpec((1, D), lambda m, k: (0, 0)), vec, vec],
        out_specs=[pl.BlockSpec((tm, D), lambda m, k: (m, 0)),
                   pl.BlockSpec((tm, D), lambda m, k: (m, 0))],
        out_shape=[jax.ShapeDtypeStruct((M, D), F32), jax.ShapeDtypeStruct((M, D), h_dtype)],
        scratch=[pltpu.VMEM((tm, D), F32)],
        name="ffn_down")(hmid, w_down, x, g2, gain, sc, sh)


def _split_bf16(v):
    hi = v.astype(BF16)
    return hi, (v - hi.astype(F32)).astype(BF16)


def _router_body(h_ref, w_ref, idx_ref, wt_ref):
    hh, hl = _split_bf16(h_ref[...])
    wh, wl = _split_bf16(w_ref[...])
    dot = functools.partial(jnp.dot, preferred_element_type=F32)
    logits = dot(hh, wh) + (dot(hl, wh) + dot(hh, wl))
    lane = lax.broadcasted_iota(jnp.int32, logits.shape, 1).astype(F32)
    logits = jnp.where(lane < N_EXPERTS, logits, -jnp.inf)
    m1 = jnp.max(logits, axis=-1, keepdims=True)
    i1 = jnp.min(jnp.where(logits == m1, lane, float(LANE)), axis=-1, keepdims=True)
    rest = jnp.where(lane == i1, -jnp.inf, logits)
    m2 = jnp.max(rest, axis=-1, keepdims=True)
    i2 = jnp.min(jnp.where(rest == m2, lane, float(LANE)), axis=-1, keepdims=True)
    e = jnp.exp(m2 - m1)
    w1 = 1.0 / (1.0 + e)
    w2 = e * w1
    idx_ref[...] = jnp.where(lane == 0.0, i1, jnp.where(lane == 1.0, i2, 0.0)).astype(jnp.int32)
    wt_ref[...] = jnp.where(lane == 0.0, w1, jnp.where(lane == 1.0, w2, 0.0))


def _router(h, w_router_pad, j):
    M, D = h.shape
    tm = _tile(M, 512)
    return _call(
        _router_body, grid=(M // tm,),
        in_specs=[pl.BlockSpec((tm, D), lambda m: (m, 0)),
                  pl.BlockSpec((None, D, LANE), lambda m: (j, 0, 0))],
        out_specs=[pl.BlockSpec((tm, LANE), lambda m: (m, 0)),
                   pl.BlockSpec((tm, LANE), lambda m: (m, 0))],
        out_shape=[jax.ShapeDtypeStruct((M, LANE), jnp.int32),
                   jax.ShapeDtypeStruct((M, LANE), F32)],
        name="router")(h, w_router_pad)


ROW_UNROLL = 8


def _gather_body(src_ref, h_hbm, o_ref, buf_ref, sem, *, tm):
    t = pl.program_id(0)
    base = t * tm

    def issue(blk, _):
        for u in range(ROW_UNROLL):
            r = blk * ROW_UNROLL + u
            pltpu.make_async_copy(h_hbm.at[pl.ds(src_ref[base + r], 1)],
                                  buf_ref.at[pl.ds(r, 1)], sem).start(priority=u % 2)
        return 0

    lax.fori_loop(0, tm // ROW_UNROLL, issue, 0)
    pltpu.make_async_copy(h_hbm.at[pl.ds(0, tm)], buf_ref, sem).wait()
    o_ref[...] = buf_ref[...].astype(o_ref.dtype)


def _gather_rows(src_tok, h, n_tiles, tm):
    M, D = h.shape
    return _call(
        functools.partial(_gather_body, tm=tm), grid=(n_tiles,), nsp=1,
        in_specs=[pl.BlockSpec(memory_space=pl.ANY)],
        out_specs=pl.BlockSpec((tm, D), lambda t, src: (t, 0)),
        out_shape=jax.ShapeDtypeStruct((n_tiles * tm, D), BF16),
        scratch=[pltpu.VMEM((tm, D), F32), pltpu.SemaphoreType.DMA(())],
        name="moe_gather")(src_tok, h)


def _grouped_body(te_ref, a_ref, *refs, n_w, epilogue):
    w_refs = refs[:n_w]
    rest = refs[n_w:]
    n_rest = len(rest) - n_w
    wb_refs = rest[n_rest:]
    t = pl.program_id(1)
    n_active = te_ref[pl.num_programs(1)]
    changed = jnp.logical_or(t == 0, te_ref[t] != te_ref[jnp.maximum(t - 1, 0)])

    @pl.when(changed)
    def _():
        for w_ref, wb_ref in zip(w_refs, wb_refs):
            wb_ref[...] = w_ref[...].astype(BF16)

    @pl.when(t < n_active)
    def _():
        a = a_ref[...]
        accs = [jnp.dot(a, wb_ref[...], preferred_element_type=F32) for wb_ref in wb_refs]
        epilogue(accs, rest[:n_rest])

    @pl.when(t >= n_active)
    def _():
        for o_ref in rest[:n_rest]:
            o_ref[...] = jnp.zeros_like(o_ref)


def _up_epilogue(accs, refs):
    g, u = accs
    refs[0][...] = (g * jax.nn.sigmoid(g) * u).astype(refs[0].dtype)


def _grouped_up(tile_e, xs, w_gate, w_up, j, tm, tf):
    NP, D = xs.shape
    F = w_gate.shape[-1]
    wspec = pl.BlockSpec((None, None, D, tf), lambda f, t, te: (j, te[t], 0, f))
    return _call(
        functools.partial(_grouped_body, n_w=2, epilogue=_up_epilogue),
        grid=(F // tf, NP // tm), nsp=1,
        in_specs=[pl.BlockSpec((tm, D), lambda f, t, te: (t, 0)), wspec, wspec],
        out_specs=pl.BlockSpec((tm, tf), lambda f, t, te: (t, f)),
        out_shape=jax.ShapeDtypeStruct((NP, F), BF16),
        scratch=[pltpu.VMEM((D, tf), BF16), pltpu.VMEM((D, tf), BF16)],
        name="moe_up")(tile_e, xs, w_gate, w_up)


def _down_epilogue(accs, refs):
    refs[0][...] = accs[0]


def _grouped_down(tile_e, hs, w_down, j, tm, tn):
    NP, F = hs.shape
    D = w_down.shape[-1]
    return _call(
        functools.partial(_grouped_body, n_w=1, epilogue=_down_epilogue),
        grid=(D // tn, NP // tm), nsp=1,
        in_specs=[pl.BlockSpec((tm, F), lambda n, t, te: (t, 0)),
                  pl.BlockSpec((None, None, F, tn), lambda n, t, te: (j, te[t], 0, n))],
        out_specs=pl.BlockSpec((tm, tn), lambda n, t, te: (t, n)),
        out_shape=jax.ShapeDtypeStruct((NP, D), F32),
        scratch=[pltpu.VMEM((F, tn), BF16)],
        name="moe_down")(tile_e, hs, w_down)


def _combine_body(pos_ref, y_hbm, wt_ref, x_ref, g2_ref, gain_ref, sc_ref, sh_ref, xo_ref, ho_ref,
                  buf_ref, sem, *, tm):
    base = pl.program_id(0) * tm * TOP_K

    def issue(blk, _):
        for u in range(ROW_UNROLL):
            r = blk * ROW_UNROLL + u
            for k in range(TOP_K):
                pltpu.make_async_copy(y_hbm.at[pl.ds(pos_ref[base + TOP_K * r + k], 1)],
                                      buf_ref.at[k, pl.ds(r, 1)], sem).start(priority=k % 2)
        return 0

    lax.fori_loop(0, tm // ROW_UNROLL, issue, 0)
    for k in range(TOP_K):
        pltpu.make_async_copy(y_hbm.at[pl.ds(0, tm)], buf_ref.at[k], sem).wait()
    wt = wt_ref[...]
    f = buf_ref[0] * wt[:, 0:1]
    for k in range(1, TOP_K):
        f = f + buf_ref[k] * wt[:, k:k + 1]
    x_new = x_ref[...] + g2_ref[...] * f
    xo_ref[...] = x_new
    ho_ref[...] = _norm_mod(x_new, gain_ref[...], sc_ref[...], sh_ref[...]).astype(ho_ref.dtype)


def _combine(pos, y_sorted, wts, x, g2, gain, sc, sh, S, h_dtype):
    M, D = x.shape
    tm = _tile(S, 256)
    bpt = S // tm
    vec = pl.BlockSpec((None, 1, D), lambda m, p: (m // bpt, 0, 0))
    return _call(
        functools.partial(_combine_body, tm=tm), grid=(M // tm,), nsp=1,
        in_specs=[pl.BlockSpec(memory_space=pl.ANY),
                  pl.BlockSpec((tm, LANE), lambda m, p: (m, 0)),
                  pl.BlockSpec((tm, D), lambda m, p: (m, 0)),
                  vec, pl.BlockSpec((1, D), lambda m, p: (0, 0)), vec, vec],
        out_specs=[pl.BlockSpec((tm, D), lambda m, p: (m, 0)),
                   pl.BlockSpec((tm, D), lambda m, p: (m, 0))],
        out_shape=[jax.ShapeDtypeStruct((M, D), F32), jax.ShapeDtypeStruct((M, D), h_dtype)],
        scratch=[pltpu.VMEM((TOP_K, tm, D), F32), pltpu.SemaphoreType.DMA(())],
        name="moe_combine")(pos, y_sorted, wts, x, g2, gain, sc, sh)


def _dispatch_plan(top_i, tm):
    M = top_i.shape[0]
    E = N_EXPERTS
    n_ent = M * TOP_K
    e_flat = top_i.reshape(-1)
    onehot = (e_flat[:, None] == jnp.arange(E, dtype=jnp.int32)[None, :]).astype(jnp.int32)
    csum = jnp.cumsum(onehot, axis=0)
    rank = jnp.sum(onehot * csum, axis=1) - 1
    counts = csum[-1]
    pcounts = ((counts + tm - 1) // tm) * tm
    pend = jnp.cumsum(pcounts)
    pstart = pend - pcounts
    dest = (pstart[e_flat] + rank).astype(jnp.int32)
    n_tiles = n_ent // tm + E
    n_rows = n_tiles * tm
    src_tok = jnp.zeros((n_rows,), jnp.int32).at[dest].set(
        jnp.arange(n_ent, dtype=jnp.int32) // TOP_K)
    tile_start = jnp.arange(n_tiles, dtype=jnp.int32) * tm
    tile_e = jnp.sum((tile_start[:, None] >= pend[None, :]).astype(jnp.int32), axis=1)
    tile_e = jnp.minimum(tile_e, E - 1).astype(jnp.int32)
    n_active = (pend[-1] // tm).astype(jnp.int32)
    return src_tok, dest, jnp.concatenate([tile_e, n_active[None]]), n_tiles


def _rot_cols(w):
    half = MLA_ROPE // 2
    return jnp.concatenate([-w[..., half:], w[..., :half]], axis=-1)


def _rope_tables(positions):
    inv_freq = 1.0 / (ROPE_THETA ** (jnp.arange(0, MLA_ROPE, 2, dtype=F32) / MLA_ROPE))
    ang = positions.astype(F32)[..., None] * inv_freq
    cos, sin = jnp.cos(ang), jnp.sin(ang)
    M = positions.size
    cos2 = jnp.concatenate([cos, cos], axis=-1).reshape(M, MLA_ROPE)
    sin2 = jnp.concatenate([sin, sin], axis=-1).reshape(M, MLA_ROPE)
    zeros = jnp.zeros((M, LANE - MLA_ROPE), F32)
    cos_k = jnp.concatenate([cos2, zeros], axis=-1)
    sin_k = jnp.concatenate([sin2, zeros], axis=-1)
    scale = MLA_QK ** -0.5 * LOG2E
    cos_q =jnp.concatenate([jnp.full((M, MLA_NOPE), scale, F32), cos_k * scale], axis=-1)
    sin_q = jnp.concatenate([jnp.zeros((M, MLA_NOPE), F32), sin_k * scale], axis=-1)
    return cos_k, sin_k, cos_q, sin_q


def kernel(x, c, positions, w_ada, b_ada, norm_mix_g, norm_ffn_g, w_in, q_norm_g, kv_norm_g, w_uq, w_ukv, w_sb_up, w_mla_up, w_o, w_ffn_gate, w_ffn_up, w_ffn_down, w_router, w_exp_gate, w_exp_up, w_exp_down, final_norm_g):
    B, S, D = x.shape
    L = w_ada.shape[0]
    M = B * S
    x = x.reshape(M, D)

    mod = _modulation(c, w_ada, b_ada)
    mod = mod.reshape(L, B, N_MOD, 1, D)
    sh1, sc1, g1, sh2, sc2, g2 = [mod[:, :, k] for k in range(N_MOD)]
    cos_k, sin_k, cos_q, sin_q = _rope_tables(positions)

    pe0 = 3 * SB_WIDTH + Q_RANK + KV_RANK
    w_pe = w_in[:, :, pe0:pe0 + MLA_ROPE]
    zpad = jnp.zeros(w_pe.shape[:2] + (LANE - MLA_ROPE,), F32)
    w_pe2 = jnp.concatenate([w_pe, zpad, _rot_cols(w_pe), zpad], axis=-1).astype(BF16)
    w_gates = w_in[:, :, pe0 + MLA_ROPE:].astype(BF16)
    c_gain = jnp.concatenate([q_norm_g, kv_norm_g], axis=-1)[:, None, :]

    wq = w_uq.reshape(L, Q_RANK, MLA_HEADS, MLA_QK)
    wq_n, wq_r = wq[..., :MLA_NOPE], wq[..., MLA_NOPE:]
    zq = jnp.zeros(wq_r.shape, F32)
    wq_main = jnp.concatenate([wq_n, wq_r, zq], axis=-1)
    wq_rot = jnp.concatenate([jnp.zeros(wq_n.shape, F32), _rot_cols(wq_r), zq], axis=-1)
    wq_both = jnp.concatenate([wq_main, wq_rot], axis=-1).reshape(
        L, Q_RANK, MLA_HEADS * 2 * QPAD).astype(BF16)

    w_sb_b, w_mla_b, w_o_b = (w.astype(BF16) for w in (w_sb_up, w_mla_up, w_o))
    w_router_pad = jnp.concatenate(
        [w_router, jnp.zeros(w_router.shape[:2] + (LANE - N_EXPERTS,), F32)], axis=-1)

    tm = _tile(S, 1024)
    bpt = S // tm
    row = lambda width: pl.BlockSpec((tm, width), lambda n, m: (m, 0))
    sb_scale = HEAD_DIM ** -0.5 * LOG2E

    def h_dtype_for(layer_is_moe):
        return F32 if layer_is_moe else BF16

    h = _prenorm(x, norm_mix_g[0][None, :], sc1[0], sh1[0], S, BF16)

    for l in range(L):
        tn = 512
        nq_blk = SB_WIDTH // tn

        def qkv_epi(accs, n, extra, outs):
            s = jnp.where(n < nq_blk, sb_scale, 1.0)
            outs[0][...] = (accs[0] * s).astype(BF16)

        (qkv,) = _matmul(h, [w_in], layer=l, col_blk0=0, n_cols=3 * SB_WIDTH, tn=tn, tm=tm,
                         epilogue=qkv_epi,
                         out_shape=[jax.ShapeDtypeStruct((M, 3 * SB_WIDTH), BF16)],
                         out_specs=[pl.BlockSpec((tm, tn), lambda n, m: (m, n))], name="in_qkv")

        def lat_epi(accs, n, extra, outs):
            a = accs[0]
            ms = jnp.mean(a * a, axis=-1, keepdims=True)
            outs[0][...] = (a * lax.rsqrt(ms + NORM_EPS) * extra[0][...]).astype(BF16)

        (lat,) = _matmul(h, [w_in], layer=l, col_blk0=3 * SB_WIDTH // Q_RANK,
                         n_cols=Q_RANK + KV_RANK, tn=Q_RANK, tm=tm, epilogue=lat_epi,
                         extras=[c_gain],
                         extra_specs=[pl.BlockSpec((None, 1, Q_RANK), lambda n, m: (l, 0, n))],
                         out_shape=[jax.ShapeDtypeStruct((M, Q_RANK + KV_RANK), BF16)],
                         out_specs=[pl.BlockSpec((tm, Q_RANK), lambda n, m: (m, n))],
                         name="in_latent")

        def pe_epi(accs, n, extra, outs):
            a = accs[0]
            outs[0][...] = (a[:, :LANE] * extra[0][...] + a[:, LANE:] * extra[1][...]).astype(BF16)

        (kpe,) = _matmul(h, [w_pe2], layer=l, col_blk0=0, n_cols=2 * LANE, tn=2 * LANE, tm=tm,
                         epilogue=pe_epi, extras=[cos_k, sin_k], extra_specs=[row(LANE), row(LANE)],
                         out_shape=[jax.ShapeDtypeStruct((M, LANE), BF16)],
                         out_specs=[pl.BlockSpec((tm, LANE), lambda n, m: (m, 0))], name="in_kpe")

        def gate_epi(accs, n, extra, outs):
            outs[0][...] = jax.nn.sigmoid(accs[0]).astype(BF16)

        (gates,) = _matmul(h, [w_gates], layer=l, col_blk0=0, n_cols=2 * D, tn=tn, tm=tm,
                           epilogue=gate_epi,
                           out_shape=[jax.ShapeDtypeStruct((M, 2 * D), BF16)],
                           out_specs=[pl.BlockSpec((tm, tn), lambda n, m: (m, n))],
                           name="in_gates")

        def q_epi(accs, n, extra, outs):
            a = accs[0]
            outs[0][...] = (a[:, :QPAD] * extra[0][...] + a[:, QPAD:] * extra[1][...]).astype(BF16)

        (q_full,) = _matmul(lat, [wq_both], layer=l, col_blk0=0, n_cols=MLA_HEADS * 2 * QPAD,
                            tn=2 * QPAD, tm=tm, epilogue=q_epi, extras=[cos_q, sin_q],
                            extra_specs=[row(QPAD), row(QPAD)],
                            out_shape=[jax.ShapeDtypeStruct((M, MLA_HEADS * QPAD), BF16)],
                            out_specs=[pl.BlockSpec((tm, QPAD), lambda n, m: (m, n))],
                            name="q_up")

        def plain_epi(accs, n, extra, outs):
            outs[0][...] = accs[0].astype(BF16)

        kv_cols = MLA_HEADS * (MLA_NOPE + MLA_V)
        (kv,) = _matmul(lat, [w_ukv], layer=l, col_blk0=0, n_cols=kv_cols, tn=tn, tm=tm,
                        epilogue=plain_epi, a_blk=1,
                        out_shape=[jax.ShapeDtypeStruct((M, kv_cols), BF16)],
                        out_specs=[pl.BlockSpec((tm, tn), lambda n, m: (m, n))], name="kv_up")

        o_sb = _sb_attention(qkv, B, S)
        o_mla = _mla_attention(q_full, kv, kpe, B, S)

        moe = (l % 2 == 1)
        x, h = _merge(o_sb, o_mla, gates, w_sb_b, w_mla_b, w_o_b, l, x, g1[l],
                      norm_ffn_g[l][None, :], sc2[l], sh2[l], S, h_dtype_for(moe))

        last = (l == L - 1)
        if last:
            n_gain = final_norm_g[None, :]
            n_sc = jnp.zeros_like(sc1[0])
            n_sh = jnp.zeros_like(sh1[0])
            n_dtype = F32
        else:
            n_gain, n_sc, n_sh, n_dtype = norm_mix_g[l + 1][None, :], sc1[l + 1], sh1[l + 1], BF16
        j = l // 2
        if not moe:
            F = w_ffn_gate.shape[-1]
            tf = _tile(F, 512)

            def up_epi(accs, n, extra, outs):
                _up_epilogue(accs, outs)

            (hmid,) = _matmul(h, [w_ffn_gate, w_ffn_up], layer=j, col_blk0=0, n_cols=F, tn=tf,
                              tm=tm, epilogue=up_epi,
                              out_shape=[jax.ShapeDtypeStruct((M, F), BF16)],
                              out_specs=[pl.BlockSpec((tm, tf), lambda n, m: (m, n))],
                              name="ffn_up")
            x, h = _ffn_down(hmid, w_ffn_down, j, x, g2[l], n_gain, n_sc, n_sh, S, n_dtype)
        else:
            F = w_exp_gate.shape[-1]
            tme = _tile(M, 256)
            idx, wts = _router(h, w_router_pad, j)
            src_tok, dest, tile_e, n_tiles = _dispatch_plan(idx[:, :TOP_K], tme)
            xs = _gather_rows(src_tok, h, n_tiles, tme)
            hs = _grouped_up(tile_e, xs, w_exp_gate, w_exp_up, j, tme, _tile(F, 512))
            ys = _grouped_down(tile_e, hs, w_exp_down, j, tme, _tile(D, 512))
            x, h = _combine(dest, ys, wts, x, g2[l], n_gain, n_sc, n_sh, S, n_dtype)

    return h.reshape(B, S, D)
```

```python
import functools

import jax
import jax.numpy as jnp
from jax import lax
from jax.experimental import pallas as pl
from jax.experimental.pallas import tpu as pltpu

BF16 = jnp.bfloat16
F32 = jnp.float32

SB_HEADS = 8
HEAD_DIM = 128
SB_WIDTH = SB_HEADS * HEAD_DIM
MLA_HEADS = 8
MLA_NOPE = 128
MLA_ROPE = 64
MLA_V = 128
MLA_QK = MLA_NOPE + MLA_ROPE
Q_RANK = 512
KV_RANK = 512
ROPE_THETA = 10000.0
N_EXPERTS = 8
TOP_K = 2
N_MOD = 6
NORM_EPS = 1e-6

LANE = 128
QPAD = 2 * LANE
VMEM_LIMIT = 56 << 20


def _call(body, *, grid, in_specs, out_specs, out_shape, scratch=(), nsp=0, name, flags=None):
    return pl.pallas_call(
        body,
        grid_spec=pltpu.PrefetchScalarGridSpec(
            num_scalar_prefetch=nsp, grid=grid, in_specs=in_specs,
            out_specs=out_specs, scratch_shapes=scratch),
        out_shape=out_shape,
        compiler_params=pltpu.CompilerParams(
            dimension_semantics=("arbitrary",) * len(grid),
            vmem_limit_bytes=VMEM_LIMIT, flags=flags),
        name=name)


def _tile(n, pref):
    t = min(n, pref)
    while n % t:
        t //= 2
    return t


def _norm_mod(x, gain, sc, sh):
    ms = jnp.mean(x * x, axis=-1, keepdims=True)
    y = x * lax.rsqrt(ms + NORM_EPS)
    return (y * gain) * (1.0 + sc) + sh


def _mm_body(*refs, n_w, n_extra, n_out, epilogue):
    a_ref = refs[0]
    w_refs = refs[1:1 + n_w]
    extra = refs[1 + n_w:1 + n_w + n_extra]
    outs = refs[1 + n_w + n_extra:1 + n_w + n_extra + n_out]
    wb_refs = refs[1 + n_w + n_extra + n_out:]

    @pl.when(pl.program_id(1) == 0)
    def _():
        for w_ref, wb_ref in zip(w_refs, wb_refs):
            wb_ref[...] = w_ref[...].astype(BF16)

    a = a_ref[...]
    accs = [jnp.dot(a, wb_ref[...], preferred_element_type=F32) for wb_ref in wb_refs]
    epilogue(accs, pl.program_id(0), extra, outs)


def _matmul(a, ws, *, layer, col_blk0, n_cols, tn, tm, epilogue, extras=(), extra_specs=(),
            out_shape, out_specs, name, a_blk=0):
    M = a.shape[0]
    K = ws[0].shape[1]
    grid = (n_cols // tn, M // tm)
    in_specs = [pl.BlockSpec((tm, K), lambda n, m: (m, a_blk))]
    for _ in ws:
        in_specs.append(pl.BlockSpec((None, K, tn), lambda n, m: (layer, 0, n + col_blk0)))
    in_specs += list(extra_specs)
    body = functools.partial(_mm_body, n_w=len(ws), n_extra=len(extras), n_out=len(out_shape),
                             epilogue=epilogue)
    return _call(body, grid=grid, in_specs=in_specs, out_specs=out_specs, out_shape=out_shape,
                 scratch=[pltpu.VMEM((K, tn), BF16) for _ in ws], name=name)(a, *ws, *extras)


def _mod_body(c_ref, w_ref, b_ref, o_ref):
    c = c_ref[...]
    c_act = (c * jax.nn.sigmoid(c)).astype(BF16)
    o_ref[...] = jnp.dot(c_act, w_ref[...].astype(BF16), preferred_element_type=F32) + b_ref[...]


def _modulation(c, w_ada, b_ada):
    L, D, N = w_ada.shape
    B = c.shape[0]
    rows = 8
    c_pad = jnp.zeros((rows, D), F32).at[:B].set(c)
    tn = _tile(N, 1024)
    out = _call(
        _mod_body, grid=(L, N // tn),
        in_specs=[pl.BlockSpec((rows, D), lambda l, n: (0, 0)),
                  pl.BlockSpec((None, D, tn), lambda l, n: (l, 0, n)),
                  pl.BlockSpec((None, 1, tn), lambda l, n: (l, 0, n))],
        out_specs=pl.BlockSpec((None, rows, tn), lambda l, n: (l, 0, n)),
        out_shape=jax.ShapeDtypeStruct((L, rows, N), F32),
        name="adaln_mod")(c_pad, w_ada, b_ada.reshape(L, 1, N))
    return out[:, :B, :]


def _prenorm_body(x_ref, g_ref, sc_ref, sh_ref, o_ref):
    o_ref[...] = _norm_mod(x_ref[...], g_ref[...], sc_ref[...], sh_ref[...]).astype(o_ref.dtype)


def _prenorm(x, gain, sc, sh, S, out_dtype):
    M, D = x.shape
    tm = _tile(S, 512)
    bpt = S // tm
    vec = pl.BlockSpec((None, 1, D), lambda m: (m // bpt, 0, 0))
    return _call(
        _prenorm_body, grid=(M // tm,),
        in_specs=[pl.BlockSpec((tm, D), lambda m: (m, 0)),
                  pl.BlockSpec((1, D), lambda m: (0, 0)), vec, vec],
        out_specs=pl.BlockSpec((tm, D), lambda m: (m, 0)),
        out_shape=jax.ShapeDtypeStruct((M, D), out_dtype),
        name="prenorm")(x, gain, sc, sh)


SUB = 256
ATT_HEADS = 2
LOG2E = 1.4426950408889634


def _qk(q, k):
    return lax.dot_general(q, k, (((1,), (1,)), ((), ())), preferred_element_type=F32)


def _sb_chains(ops, tri, states, masks):
    zs = [_qk(q, k) for q, k, v in ops]
    sps = []
    for z, mask in zip(zs, masks):
        sp = jnp.maximum(z, 0.0) + jnp.log2(1.0 + jnp.exp2(jnp.minimum(z, -z)))
        if mask is not None:
            sp = jnp.where(mask, sp, 0.0)
        hi = sp.astype(BF16)
        sps.append((hi, (sp - hi.astype(F32)).astype(BF16)))
    csums, carries = [], []
    for (hi, lo), (carry, acc) in zip(sps, states):
        parts = []
        for c in reversed(range(hi.shape[1] // SUB)):
            sl = slice(c * SUB, (c + 1) * SUB)
            loc = (jnp.dot(hi[:, sl], tri, preferred_element_type=F32)
                   + jnp.dot(lo[:, sl], tri, preferred_element_type=F32))
            parts.insert(0, loc + carry)
            carry = carry + loc[:, 0:1]
        csums.append(parts[0] if len(parts) == 1 else jnp.concatenate(parts, axis=1))
        carries.append(carry)
    out = []
    for z, csum, mask, carry, (q, k, v), (_, acc) in zip(zs, csums, masks, carries, ops, states):
        a = jnp.exp2(z - csum)
        if mask is not None:
            a = jnp.where(mask, a, 0.0)
        out.append((carry, acc + jnp.dot(a.astype(BF16), v, preferred_element_type=F32)))
    return tuple(out)


def _sb_body(q_ref, k_ref, v_ref, o_ref, *, t):
    i = pl.program_id(2)
    halves = t // SUB
    r = lax.broadcasted_iota(jnp.int32, (SUB, SUB), 0)
    c = lax.broadcasted_iota(jnp.int32, (SUB, SUB), 1)
    tri = jnp.where(r >= c, 1.0, 0.0).astype(BF16)
    start = pl.multiple_of(i * t, t)
    chains = [(g, a) for g in range(ATT_HEADS) for a in range(halves)]

    def operands(g, a, row0, nrows):
        cols = slice(g * HEAD_DIM, (g + 1) * HEAD_DIM)
        return (q_ref[a * SUB:(a + 1) * SUB, cols], k_ref[pl.ds(row0, nrows), cols],
                v_ref[pl.ds(row0, nrows), cols])

    ops, masks, st0 = [], [], []
    for g, a in chains:
        n = (a + 1) * SUB
        rr = lax.broadcasted_iota(jnp.int32, (SUB, n), 0)
        cc = lax.broadcasted_iota(jnp.int32, (SUB, n), 1)
        ops.append(operands(g, a, start, n))
        masks.append(cc < rr + a * SUB)
        st0.append((jnp.zeros((SUB, 1), F32), jnp.zeros((SUB, HEAD_DIM), F32)))
    state = _sb_chains(ops, tri, st0, masks)

    def step(jj, st):
        row0 = pl.multiple_of((i - 1 - jj) * t, t)
        return _sb_chains([operands(g, a, row0, t) for g, a in chains], tri, st,
                          [None] * len(chains))

    state = lax.fori_loop(0, i, step, state)
    for n, (g, a) in enumerate(chains):
        o_ref[a * SUB:(a + 1) * SUB, g * HEAD_DIM:(g + 1) * HEAD_DIM] = state[n][1].astype(o_ref.dtype)


def _sb_attention(qkv, B, S):
    t = _tile(S, 2 * SUB)
    nq = S // t
    G = ATT_HEADS
    HG = SB_HEADS // G
    W = G * HEAD_DIM
    return _call(
        functools.partial(_sb_body, t=t), grid=(B, HG, nq),
        in_specs=[pl.BlockSpec((t, W), lambda b, h, i: (b * nq + i, h)),
                  pl.BlockSpec((S, W), lambda b, h, i: (b, HG + h)),
                  pl.BlockSpec((S, W), lambda b, h, i: (b, 2 * HG + h))],
        out_specs=pl.BlockSpec((t, W), lambda b, h, i: (b * nq + i, h)),
        out_shape=jax.ShapeDtypeStruct((B * S, SB_WIDTH), BF16),
        name="sb_attention")(qkv, qkv, qkv)


def _mla_chains(ops, states, masks):
    ss = []
    for (q, k, v), mask in zip(ops, masks):
        s = _qk(q, k)
        if mask is not None:
            s = jnp.where(mask, s, -jnp.inf)
        ss.append(s)
    ms = [jnp.maximum(m, jnp.max(s, axis=-1, keepdims=True)) for s, (m, l, acc) in zip(ss, states)]
    ps = [jnp.exp2(s - m_new) for s, m_new in zip(ss, ms)]
    out = []
    for p, m_new, (q, k, v), (m, l, acc) in zip(ps, ms, ops, states):
        alpha = jnp.exp2(m - m_new)
        l = alpha * l + jnp.sum(p, axis=-1, keepdims=True)
        acc = alpha * acc + jnp.dot(p.astype(BF16), v, preferred_element_type=F32)
        out.append((m_new, l, acc))
    return tuple(out)


def _mla_body(q_ref, kv_ref, kpe_ref, o_ref, kf_ref, *, t):
    i = pl.program_id(2)
    halves = t // SUB
    hw = MLA_NOPE + MLA_V

    @pl.when(i == 0)
    def _():
        for g in range(ATT_HEADS):
            kf_ref[g, :, 0:LANE] = kv_ref[:, g * hw:g * hw + MLA_NOPE]
            kf_ref[g, :, LANE:QPAD] = kpe_ref[...]

    start = pl.multiple_of(i * t, t)
    chains = [(g, a) for g in range(ATT_HEADS) for a in range(halves)]

    def operands(g, a, row0, nrows):
        return (q_ref[a * SUB:(a + 1) * SUB, g * QPAD:(g + 1) * QPAD],
                kf_ref[g, pl.ds(row0, nrows), :],
                kv_ref[pl.ds(row0, nrows), g * hw + MLA_NOPE:(g + 1) * hw])

    ops, masks, st0 = [], [], []
    for g, a in chains:
        n = (a + 1) * SUB
        rr = lax.broadcasted_iota(jnp.int32, (SUB, n), 0)
        cc = lax.broadcasted_iota(jnp.int32, (SUB, n), 1)
        ops.append(operands(g, a, start, n))
        masks.append(cc <= rr + a * SUB)
        st0.append((jnp.full((SUB, 1), -jnp.inf, F32), jnp.zeros((SUB, 1), F32),
                    jnp.zeros((SUB, MLA_V), F32)))
    state = _mla_chains(ops, st0, masks)

    def step(j, st):
        row0 = pl.multiple_of(j * t, t)
        return _mla_chains([operands(g, a, row0, t) for g, a in chains], st, [None] * len(chains))

    state = lax.fori_loop(0, i, step, state)
    for n, (g, a) in enumerate(chains):
        m, l, acc = state[n]
        o_ref[a * SUB:(a + 1) * SUB, g * MLA_V:(g + 1) * MLA_V] = (acc / l).astype(o_ref.dtype)


def _mla_attention(q_full, kv, kpe, B, S):
    t = _tile(S, 2 * SUB)
    nq = S // t
    G = ATT_HEADS
    HG = MLA_HEADS // G
    return _call(
        functools.partial(_mla_body, t=t), grid=(B, HG, nq),
        in_specs=[pl.BlockSpec((t, G * QPAD), lambda b, h, i: (b * nq + i, h)),
                  pl.BlockSpec((S, G * (MLA_NOPE + MLA_V)), lambda b, h, i: (b, h)),
                  pl.BlockSpec((S, LANE), lambda b, h, i: (b, 0))],
        out_specs=pl.BlockSpec((t, G * MLA_V), lambda b, h, i: (b * nq + i, h)),
        out_shape=jax.ShapeDtypeStruct((B * S, MLA_HEADS * MLA_V), BF16),
        scratch=[pltpu.VMEM((G, S, QPAD), BF16)],
        name="mla_attention")(q_full, kv, kpe)


def _merge_body(osb_ref, omla_ref, gate_sb_ref, gate_mla_ref, wsb_ref, wmla_ref, wo_ref, x_ref,
                g1_ref, gain_ref, sc_ref, sh_ref, xo_ref, ho_ref):
    y = (gate_sb_ref[...].astype(F32)
         * jnp.dot(osb_ref[...], wsb_ref[...], preferred_element_type=F32)
         + gate_mla_ref[...].astype(F32)
         * jnp.dot(omla_ref[...], wmla_ref[...], preferred_element_type=F32))
    out = jnp.dot(y.astype(BF16), wo_ref[...], preferred_element_type=F32)
    x_new = x_ref[...] + g1_ref[...] * out
    xo_ref[...] = x_new
    ho_ref[...] = _norm_mod(x_new, gain_ref[...], sc_ref[...], sh_ref[...]).astype(ho_ref.dtype)


def _merge(o_sb, o_mla, gates, w_sb, w_mla, w_o, layer, x, g1, gain, sc, sh, S, h_dtype):
    M, D = x.shape
    tm = _tile(S, 256)
    bpt = S // tm
    nd = D // D
    vec = pl.BlockSpec((None, 1, D), lambda m: (m // bpt, 0, 0))
    res = lambda k: pl.BlockSpec((None, k, D), lambda m: (layer, 0, 0),
                                 pipeline_mode=pl.Buffered(1))
    return _call(
        _merge_body, grid=(M // tm,),
        in_specs=[pl.BlockSpec((tm, SB_WIDTH), lambda m: (m, 0)),
                  pl.BlockSpec((tm, MLA_HEADS * MLA_V), lambda m: (m, 0)),
                  pl.BlockSpec((tm, D), lambda m: (m, 0)),
                  pl.BlockSpec((tm, D), lambda m: (m, nd)),
                  res(SB_WIDTH), res(MLA_HEADS * MLA_V), res(D),
                  pl.BlockSpec((tm, D), lambda m: (m, 0)),
                  vec, pl.BlockSpec((1, D), lambda m: (0, 0)), vec, vec],
        out_specs=[pl.BlockSpec((tm, D), lambda m: (m, 0)),
                   pl.BlockSpec((tm, D), lambda m: (m, 0))],
        out_shape=[jax.ShapeDtypeStruct((M, D), F32), jax.ShapeDtypeStruct((M, D), h_dtype)],
        name="merge_out")(o_sb, o_mla, gates, gates, w_sb, w_mla, w_o, x, g1, gain, sc, sh)


def _resnorm_body(f_ref, x_ref, g_ref, gain_ref, sc_ref, sh_ref, xo_ref, ho_ref):
    x_new = x_ref[...] + g_ref[...] * f_ref[...]
    xo_ref[...] = x_new
    ho_ref[...] = _norm_mod(x_new, gain_ref[...], sc_ref[...], sh_ref[...]).astype(ho_ref.dtype)


def _resnorm(f, x, g, gain, sc, sh, S, h_dtype):
    M, D = x.shape
    tm = _tile(S, 512)
    bpt = S // tm
    vec = pl.BlockSpec((None, 1, D), lambda m: (m // bpt, 0, 0))
    rows = pl.BlockSpec((tm, D), lambda m: (m, 0))
    return _call(
        _resnorm_body, grid=(M // tm,),
        in_specs=[rows, rows, vec, pl.BlockSpec((1, D), lambda m: (0, 0)), vec, vec],
        out_specs=[rows, rows],
        out_shape=[jax.ShapeDtypeStruct((M, D), F32), jax.ShapeDtypeStruct((M, D), h_dtype)],
        name="resnorm")(f, x, g, gain, sc, sh)


def _split_bf16(v):
    hi = v.astype(BF16)
    return hi, (v - hi.astype(F32)).astype(BF16)


def _router_body(h_ref, w_ref, idx_ref, wt_ref):
    hh, hl = _split_bf16(h_ref[...])
    wh, wl = _split_bf16(w_ref[...])
    dot = functools.partial(jnp.dot, preferred_element_type=F32)
    logits = dot(hh, wh) + (dot(hl, wh) + dot(hh, wl))
    lane = lax.broadcasted_iota(jnp.int32, logits.shape, 1).astype(F32)
    logits = jnp.where(lane < N_EXPERTS, logits, -jnp.inf)
    m1 = jnp.max(logits, axis=-1, keepdims=True)
    i1 = jnp.min(jnp.where(logits == m1, lane, float(LANE)), axis=-1, keepdims=True)
    rest = jnp.where(lane == i1, -jnp.inf, logits)
    m2 = jnp.max(rest, axis=-1, keepdims=True)
    i2 = jnp.min(jnp.where(rest == m2, lane, float(LANE)), axis=-1, keepdims=True)
    e = jnp.exp(m2 - m1)
    w1 = 1.0 / (1.0 + e)
    w2 = e * w1
    idx_ref[...] = jnp.where(lane == 0.0, i1, jnp.where(lane == 1.0, i2, 0.0)).astype(jnp.int32)
    wt_ref[...] = jnp.where(lane == 0.0, w1, jnp.where(lane == 1.0, w2, 0.0))


def _router(h, w_router_pad, j):
    M, D = h.shape
    tm = _tile(M, 512)
    return _call(
        _router_body, grid=(M // tm,),
        in_specs=[pl.BlockSpec((tm, D), lambda m: (m, 0)),
                  pl.BlockSpec((None, D, LANE), lambda m: (j, 0, 0))],
        out_specs=[pl.BlockSpec((tm, LANE), lambda m: (m, 0)),
                   pl.BlockSpec((tm, LANE), lambda m: (m, 0))],
        out_shape=[jax.ShapeDtypeStruct((M, LANE), jnp.int32),
                   jax.ShapeDtypeStruct((M, LANE), F32)],
        name="router")(h, w_router_pad)


MOE_SUB = 256
ROW_UNROLL = 8


def _gather_body(src_ref, h_hbm, o_ref, buf_ref, sem, *, tm):
    t = pl.program_id(0)
    base = t * tm

    def issue(blk, _):
        for u in range(ROW_UNROLL):
            r = blk * ROW_UNROLL + u
            pltpu.make_async_copy(h_hbm.at[pl.ds(src_ref[base + r], 1)],
                                  buf_ref.at[pl.ds(r, 1)], sem).start(priority=u % 2)
        return 0

    lax.fori_loop(0, tm // ROW_UNROLL, issue, 0)
    pltpu.make_async_copy(h_hbm.at[pl.ds(0, tm)], buf_ref, sem).wait()
    o_ref[...] = buf_ref[...].astype(o_ref.dtype)


def _gather_rows(src_tok, h, n_tiles, tm):
    M, D = h.shape
    return _call(
        functools.partial(_gather_body, tm=tm), grid=(n_tiles,), nsp=1,
        in_specs=[pl.BlockSpec(memory_space=pl.ANY)],
        out_specs=pl.BlockSpec((tm, D), lambda t, src: (t, 0)),
        out_shape=jax.ShapeDtypeStruct((n_tiles * tm, D), BF16),
        scratch=[pltpu.VMEM((tm, D), F32), pltpu.SemaphoreType.DMA(())],
        name="moe_gather")(src_tok, h)


def _grouped_body(te_ref, a_ref, *refs, n_w, epilogue):
    w_refs = refs[:n_w]
    rest = refs[n_w:]
    n_rest = len(rest) - n_w
    wb_refs = rest[n_rest:]
    t = pl.program_id(1)
    n_valid = te_ref[pl.num_programs(1) + t]
    changed = jnp.logical_or(t == 0, te_ref[t] != te_ref[jnp.maximum(t - 1, 0)])

    @pl.when(changed)
    def _():
        for w_ref, wb_ref in zip(w_refs, wb_refs):
            wb_ref[...] = w_ref[...].astype(BF16)

    for s in range(a_ref.shape[0] // MOE_SUB):
        rows = slice(s * MOE_SUB, (s + 1) * MOE_SUB)

        @pl.when(s * MOE_SUB < n_valid)
        def _():
            a = a_ref[rows, :]
            accs = [jnp.dot(a, wb_ref[...], preferred_element_type=F32) for wb_ref in wb_refs]
            epilogue(accs, rest[:n_rest], rows)

        @pl.when(s * MOE_SUB >= n_valid)
        def _():
            for o_ref in rest[:n_rest]:
                o_ref[rows, :] = jnp.zeros((MOE_SUB, o_ref.shape[1]), o_ref.dtype)


def _up_epilogue(accs, refs, rows=slice(None)):
    g, u = accs
    refs[0][rows, :] = (g * jax.nn.sigmoid(g) * u).astype(refs[0].dtype)


def _grouped_up(tile_e, xs, w_gate, w_up, j, tm, tf):
    NP, D = xs.shape
    F = w_gate.shape[-1]
    wspec = pl.BlockSpec((None, None, D, tf), lambda f, t, te: (j, te[t], 0, f))
    return _call(
        functools.partial(_grouped_body, n_w=2, epilogue=_up_epilogue),
        grid=(F // tf, NP // tm), nsp=1,
        in_specs=[pl.BlockSpec((tm, D), lambda f, t, te: (t, 0)), wspec, wspec],
        out_specs=pl.BlockSpec((tm, tf), lambda f, t, te: (t, f)),
        out_shape=jax.ShapeDtypeStruct((NP, F), BF16),
        scratch=[pltpu.VMEM((D, tf), BF16), pltpu.VMEM((D, tf), BF16)],
        name="moe_up")(tile_e, xs, w_gate, w_up)


def _down_epilogue(accs, refs, rows):
    refs[0][rows, :] = accs[0]


def _grouped_down(tile_e, hs, w_down, j, tm, tn):
    NP, F = hs.shape
    D = w_down.shape[-1]
    return _call(
        functools.partial(_grouped_body, n_w=1, epilogue=_down_epilogue),
        grid=(D // tn, NP // tm), nsp=1,
        in_specs=[pl.BlockSpec((tm, F), lambda n, t, te: (t, 0)),
                  pl.BlockSpec((None, None, F, tn), lambda n, t, te: (j, te[t], 0, n))],
        out_specs=pl.BlockSpec((tm, tn), lambda n, t, te: (t, n)),
        out_shape=jax.ShapeDtypeStruct((NP, D), F32),
        scratch=[pltpu.VMEM((F, tn), BF16)],
        name="moe_down")(tile_e, hs, w_down)


def _combine_body(pos_ref, y_hbm, wt_ref, x_ref, g2_ref, gain_ref, sc_ref, sh_ref, xo_ref, ho_ref,
                  buf_ref, sem, *, tm):
    base = pl.program_id(0) * tm * TOP_K

    def issue(blk, _):
        for u in range(ROW_UNROLL):
            r = blk * ROW_UNROLL + u
            for k in range(TOP_K):
                pltpu.make_async_copy(y_hbm.at[pl.ds(pos_ref[base + TOP_K * r + k], 1)],
                                      buf_ref.at[k, pl.ds(r, 1)], sem).start(priority=k % 2)
        return 0

    lax.fori_loop(0, tm // ROW_UNROLL, issue, 0)
    for k in range(TOP_K):
        pltpu.make_async_copy(y_hbm.at[pl.ds(0, tm)], buf_ref.at[k], sem).wait()
    wt = wt_ref[...]
    f = buf_ref[0] * wt[:, 0:1]
    for k in range(1, TOP_K):
        f = f + buf_ref[k] * wt[:, k:k + 1]
    x_new = x_ref[...] + g2_ref[...] * f
    xo_ref[...] = x_new
    ho_ref[...] = _norm_mod(x_new, gain_ref[...], sc_ref[...], sh_ref[...]).astype(ho_ref.dtype)


def _combine(pos, y_sorted, wts, x, g2, gain, sc, sh, S, h_dtype):
    M, D = x.shape
    tm = _tile(S, 256)
    bpt = S // tm
    vec = pl.BlockSpec((None, 1, D), lambda m, p: (m // bpt, 0, 0))
    return _call(
        functools.partial(_combine_body, tm=tm), grid=(M // tm,), nsp=1,
        in_specs=[pl.BlockSpec(memory_space=pl.ANY),
                  pl.BlockSpec((tm, LANE), lambda m, p: (m, 0)),
                  pl.BlockSpec((tm, D), lambda m, p: (m, 0)),
                  vec, pl.BlockSpec((1, D), lambda m, p: (0, 0)), vec, vec],
        out_specs=[pl.BlockSpec((tm, D), lambda m, p: (m, 0)),
                   pl.BlockSpec((tm, D), lambda m, p: (m, 0))],
        out_shape=[jax.ShapeDtypeStruct((M, D), F32), jax.ShapeDtypeStruct((M, D), h_dtype)],
        scratch=[pltpu.VMEM((TOP_K, tm, D), F32), pltpu.SemaphoreType.DMA(())],
        name="moe_combine")(pos, y_sorted, wts, x, g2, gain, sc, sh)


def _dispatch_plan(top_i, tm):
    M = top_i.shape[0]
    E = N_EXPERTS
    n_ent = M * TOP_K
    e_flat = top_i.reshape(-1)
    onehot = (e_flat[:, None] == jnp.arange(E, dtype=jnp.int32)[None, :]).astype(jnp.int32)
    csum = jnp.cumsum(onehot, axis=0)
    rank = jnp.sum(onehot * csum, axis=1) - 1
    counts = csum[-1]
    pcounts = ((counts + tm - 1) // tm) * tm
    pend = jnp.cumsum(pcounts)
    pstart = pend - pcounts
    dest = (pstart[e_flat] + rank).astype(jnp.int32)
    n_tiles = n_ent // tm + E
    n_rows = n_tiles * tm
    src_tok = jnp.zeros((n_rows,), jnp.int32).at[dest].set(
        jnp.arange(n_ent, dtype=jnp.int32) // TOP_K)
    tile_start = jnp.arange(n_tiles, dtype=jnp.int32) * tm
    tile_e = jnp.sum((tile_start[:, None] >= pend[None, :]).astype(jnp.int32), axis=1)
    tile_e = jnp.minimum(tile_e, E - 1).astype(jnp.int32)
    n_valid = jnp.clip(pstart[tile_e] + counts[tile_e] - tile_start, 0, tm).astype(jnp.int32)
    return src_tok, dest, jnp.concatenate([tile_e, n_valid]), n_tiles


def _rot_cols(w):
    half = MLA_ROPE // 2
    return jnp.concatenate([-w[..., half:], w[..., :half]], axis=-1)


def _rope_tables(positions):
    inv_freq = 1.0 / (ROPE_THETA ** (jnp.arange(0, MLA_ROPE, 2, dtype=F32) / MLA_ROPE))
    ang = positions.astype(F32)[..., None] * inv_freq
    cos, sin = jnp.cos(ang), jnp.sin(ang)
    M = positions.size
    cos2 = jnp.concatenate([cos, cos], axis=-1).reshape(M, MLA_ROPE)
    sin2 = jnp.concatenate([sin, sin], axis=-1).reshape(M, MLA_ROPE)
    zeros = jnp.zeros((M, LANE - MLA_ROPE), F32)
    cos_k = jnp.concatenate([cos2, zeros], axis=-1)
    sin_k = jnp.concatenate([sin2, zeros], axis=-1)
    scale = MLA_QK ** -0.5 * LOG2E
    cos_q =jnp.concatenate([jnp.full((M, MLA_NOPE), scale, F32), cos_k * scale], axis=-1)
    sin_q = jnp.concatenate([jnp.zeros((M, MLA_NOPE), F32), sin_k * scale], axis=-1)
    return cos_k, sin_k, cos_q, sin_q


def kernel(x, c, positions, w_ada, b_ada, norm_mix_g, norm_ffn_g, w_in, q_norm_g, kv_norm_g, w_uq, w_ukv, w_sb_up, w_mla_up, w_o, w_ffn_gate, w_ffn_up, w_ffn_down, w_router, w_exp_gate, w_exp_up, w_exp_down, final_norm_g):
    B, S, D = x.shape
    L = w_ada.shape[0]
    M = B * S
    x = x.reshape(M, D)

    mod = _modulation(c, w_ada, b_ada)
    mod = mod.reshape(L, B, N_MOD, 1, D)
    sh1, sc1, g1, sh2, sc2, g2 = [mod[:, :, k] for k in range(N_MOD)]
    cos_k, sin_k, cos_q, sin_q = _rope_tables(positions)

    pe0 = 3 * SB_WIDTH + Q_RANK + KV_RANK
    w_pe = w_in[:, :, pe0:pe0 + MLA_ROPE]
    zpad = jnp.zeros(w_pe.shape[:2] + (LANE - MLA_ROPE,), F32)
    w_pe2 = jnp.concatenate([w_pe, zpad, _rot_cols(w_pe), zpad], axis=-1).astype(BF16)
    w_gates = w_in[:, :, pe0 + MLA_ROPE:].astype(BF16)
    c_gain = jnp.concatenate([q_norm_g, kv_norm_g], axis=-1)[:, None, :]

    wq = w_uq.reshape(L, Q_RANK, MLA_HEADS, MLA_QK)
    wq_n, wq_r = wq[..., :MLA_NOPE], wq[..., MLA_NOPE:]
    zq = jnp.zeros(wq_r.shape, F32)
    wq_main = jnp.concatenate([wq_n, wq_r, zq], axis=-1)
    wq_rot = jnp.concatenate([jnp.zeros(wq_n.shape, F32), _rot_cols(wq_r), zq], axis=-1)
    wq_both = jnp.concatenate([wq_main, wq_rot], axis=-1).reshape(
        L, Q_RANK, MLA_HEADS * 2 * QPAD).astype(BF16)

    w_sb_b, w_mla_b, w_o_b = (w.astype(BF16) for w in (w_sb_up, w_mla_up, w_o))
    w_router_pad = jnp.concatenate(
        [w_router, jnp.zeros(w_router.shape[:2] + (LANE - N_EXPERTS,), F32)], axis=-1)

    tm = _tile(S, 1024)
    bpt = S // tm
    row = lambda width: pl.BlockSpec((tm, width), lambda n, m: (m, 0))
    sb_scale = HEAD_DIM ** -0.5 * LOG2E

    def h_dtype_for(layer_is_moe):
        return F32 if layer_is_moe else BF16

    h = _prenorm(x, norm_mix_g[0][None, :], sc1[0], sh1[0], S, BF16)

    for l in range(L):
        tn = 512
        nq_blk = SB_WIDTH // tn

        def qkv_epi(accs, n, extra, outs):
            s = jnp.where(n < nq_blk, sb_scale, 1.0)
            outs[0][...] = (accs[0] * s).astype(BF16)

        (qkv,) = _matmul(h, [w_in], layer=l, col_blk0=0, n_cols=3 * SB_WIDTH, tn=tn, tm=tm,
                         epilogue=qkv_epi,
                         out_shape=[jax.ShapeDtypeStruct((M, 3 * SB_WIDTH), BF16)],
                         out_specs=[pl.BlockSpec((tm, tn), lambda n, m: (m, n))], name="in_qkv")

        def lat_epi(accs, n, extra, outs):
            a = accs[0]
            ms = jnp.mean(a * a, axis=-1, keepdims=True)
            outs[0][...] = (a * lax.rsqrt(ms + NORM_EPS) * extra[0][...]).astype(BF16)

        (lat,) = _matmul(h, [w_in], layer=l, col_blk0=3 * SB_WIDTH // Q_RANK,
                         n_cols=Q_RANK + KV_RANK, tn=Q_RANK, tm=tm, epilogue=lat_epi,
                         extras=[c_gain],
                         extra_specs=[pl.BlockSpec((None, 1, Q_RANK), lambda n, m: (l, 0, n))],
                         out_shape=[jax.ShapeDtypeStruct((M, Q_RANK + KV_RANK), BF16)],
                         out_specs=[pl.BlockSpec((tm, Q_RANK), lambda n, m: (m, n))],
                         name="in_latent")

        def pe_epi(accs, n, extra, outs):
            a = accs[0]
            outs[0][...] = (a[:, :LANE] * extra[0][...] + a[:, LANE:] * extra[1][...]).astype(BF16)

        (kpe,) = _matmul(h, [w_pe2], layer=l, col_blk0=0, n_cols=2 * LANE, tn=2 * LANE, tm=tm,
                         epilogue=pe_epi, extras=[cos_k, sin_k], extra_specs=[row(LANE), row(LANE)],
                         out_shape=[jax.ShapeDtypeStruct((M, LANE), BF16)],
                         out_specs=[pl.BlockSpec((tm, LANE), lambda n, m: (m, 0))], name="in_kpe")

        def gate_epi(accs, n, extra, outs):
            outs[0][...] = jax.nn.sigmoid(accs[0]).astype(BF16)

        (gates,) = _matmul(h, [w_gates], layer=l, col_blk0=0, n_cols=2 * D, tn=tn, tm=tm,
                           epilogue=gate_epi,
                           out_shape=[jax.ShapeDtypeStruct((M, 2 * D), BF16)],
                           out_specs=[pl.BlockSpec((tm, tn), lambda n, m: (m, n))],
                           name="in_gates")

        def q_epi(accs, n, extra, outs):
            a = accs[0]
            outs[0][...] = (a[:, :QPAD] * extra[0][...] + a[:, QPAD:] * extra[1][...]).astype(BF16)

        (q_full,) = _matmul(lat, [wq_both], layer=l, col_blk0=0, n_cols=MLA_HEADS * 2 * QPAD,
                            tn=2 * QPAD, tm=tm, epilogue=q_epi, extras=[cos_q, sin_q],
                            extra_specs=[row(QPAD), row(QPAD)],
                            out_shape=[jax.ShapeDtypeStruct((M, MLA_HEADS * QPAD), BF16)],
                            out_specs=[pl.BlockSpec((tm, QPAD), lambda n, m: (m, n))],
                            name="q_up")

        def plain_epi(accs, n, extra, outs):
            outs[0][...] = accs[0].astype(BF16)

        kv_cols = MLA_HEADS * (MLA_NOPE + MLA_V)
        (kv,) = _matmul(lat, [w_ukv], layer=l, col_blk0=0, n_cols=kv_cols, tn=tn, tm=tm,
                        epilogue=plain_epi, a_blk=1,
                        out_shape=[jax.ShapeDtypeStruct((M, kv_cols), BF16)],
                        out_specs=[pl.BlockSpec((tm, tn), lambda n, m: (m, n))], name="kv_up")

        o_sb = _sb_attention(qkv, B, S)
        o_mla = _mla_attention(q_full, kv, kpe, B, S)

        moe = (l % 2 == 1)
        x, h = _merge(o_sb, o_mla, gates, w_sb_b, w_mla_b, w_o_b, l, x, g1[l],
                      norm_ffn_g[l][None, :], sc2[l], sh2[l], S, h_dtype_for(moe))

        last = (l == L - 1)
        if last:
            n_gain = final_norm_g[None, :]
            n_sc = jnp.zeros_like(sc1[0])
            n_sh = jnp.zeros_like(sh1[0])
            n_dtype = F32
        else:
            n_gain, n_sc, n_sh, n_dtype = norm_mix_g[l + 1][None, :], sc1[l + 1], sh1[l + 1], BF16
        j = l // 2
        if not moe:
            F = w_ffn_gate.shape[-1]
            tf = _tile(F, 512)

            def up_epi(accs, n, extra, outs):
                _up_epilogue(accs, outs)

            (hmid,) = _matmul(h, [w_ffn_gate, w_ffn_up], layer=j, col_blk0=0, n_cols=F, tn=tf,
                              tm=tm, epilogue=up_epi,
                              out_shape=[jax.ShapeDtypeStruct((M, F), BF16)],
                              out_specs=[pl.BlockSpec((tm, tf), lambda n, m: (m, n))],
                              name="ffn_up")
            def down_epi(accs, n, extra, outs):
                outs[0][...] = accs[0]

            tnd = _tile(D, 512)
            (f,) = _matmul(hmid, [w_ffn_down], layer=j, col_blk0=0, n_cols=D, tn=tnd,
                           tm=_tile(S, 512), epilogue=down_epi,
                           out_shape=[jax.ShapeDtypeStruct((M, D), F32)],
                           out_specs=[pl.BlockSpec((_tile(S, 512), tnd), lambda n, m: (m, n))],
                           name="ffn_down")
            x, h = _resnorm(f, x, g2[l], n_gain, n_sc, n_sh, S, n_dtype)
        else:
            F = w_exp_gate.shape[-1]
            tme = max(_tile(M, 2 * MOE_SUB), MOE_SUB)
            idx, wts = _router(h, w_router_pad, j)
            src_tok, dest, tile_e, n_tiles = _dispatch_plan(idx[:, :TOP_K], tme)
            xs = _gather_rows(src_tok, h, n_tiles, tme)
            hs = _grouped_up(tile_e, xs, w_exp_gate, w_exp_up, j, tme, _tile(F, 512))
            ys = _grouped_down(tile_e, hs, w_exp_down, j, tme, _tile(D, 512))
            x, h = _combine(dest, ys, wts, x, g2[l], n_gain, n_sc, n_sh, S, n_dtype)

    return h.reshape(B, S, D)
```

```python
import functools

import jax
import jax.numpy as jnp
from jax import lax
from jax.experimental import pallas as pl
from jax.experimental.pallas import tpu as pltpu

BF16 = jnp.bfloat16
F32 = jnp.float32

SB_HEADS = 8
HEAD_DIM = 128
SB_WIDTH = SB_HEADS * HEAD_DIM
MLA_HEADS = 8
MLA_NOPE = 128
MLA_ROPE = 64
MLA_V = 128
MLA_QK = MLA_NOPE + MLA_ROPE
Q_RANK = 512
KV_RANK = 512
ROPE_THETA = 10000.0
N_EXPERTS = 8
TOP_K = 2
N_MOD = 6
NORM_EPS = 1e-6

LOG2E = 1.4426950408889634
MLA_SCALE = MLA_QK ** -0.5 * LOG2E

LANE = 128
QPAD = 2 * LANE
VMEM_LIMIT = 56 << 20


def _call(body, *, grid, in_specs, out_specs, out_shape, scratch=(), nsp=0, name, flags=None):
    return pl.pallas_call(
        body,
        grid_spec=pltpu.PrefetchScalarGridSpec(
            num_scalar_prefetch=nsp, grid=grid, in_specs=in_specs,
            out_specs=out_specs, scratch_shapes=scratch),
        out_shape=out_shape,
        compiler_params=pltpu.CompilerParams(
            dimension_semantics=("arbitrary",) * len(grid),
            vmem_limit_bytes=VMEM_LIMIT, flags=flags),
        name=name)


def _tile(n, pref):
    t = min(n, pref)
    while n % t:
        t //= 2
    return t


def _norm_mod(x, gain, sc, sh):
    ms = jnp.mean(x * x, axis=-1, keepdims=True)
    y = x * lax.rsqrt(ms + NORM_EPS)
    return (y * gain) * (1.0 + sc) + sh


def _mm_body(*refs, n_w, n_extra, n_out, epilogue):
    a_ref = refs[0]
    w_refs = refs[1:1 + n_w]
    extra = refs[1 + n_w:1 + n_w + n_extra]
    outs = refs[1 + n_w + n_extra:1 + n_w + n_extra + n_out]
    wb_refs = refs[1 + n_w + n_extra + n_out:]

    @pl.when(pl.program_id(1) == 0)
    def _():
        for w_ref, wb_ref in zip(w_refs, wb_refs):
            wb_ref[...] = w_ref[...].astype(BF16)

    a = a_ref[...]
    accs = [jnp.dot(a, wb_ref[...], preferred_element_type=F32) for wb_ref in wb_refs]
    epilogue(accs, pl.program_id(0), extra, outs)


def _matmul(a, ws, *, layer, col_blk0, n_cols, tn, tm, epilogue, extras=(), extra_specs=(),
            out_shape, out_specs, name, a_blk=0):
    M = a.shape[0]
    K = ws[0].shape[1]
    grid = (n_cols // tn, M // tm)
    in_specs = [pl.BlockSpec((tm, K), lambda n, m: (m, a_blk))]
    for _ in ws:
        in_specs.append(pl.BlockSpec((None, K, tn), lambda n, m: (layer, 0, n + col_blk0)))
    in_specs += list(extra_specs)
    body = functools.partial(_mm_body, n_w=len(ws), n_extra=len(extras), n_out=len(out_shape),
                             epilogue=epilogue)
    return _call(body, grid=grid, in_specs=in_specs, out_specs=out_specs, out_shape=out_shape,
                 scratch=[pltpu.VMEM((K, tn), BF16) for _ in ws], name=name)(a, *ws, *extras)


def _mod_body(c_ref, w_ref, b_ref, o_ref):
    c = c_ref[...]
    c_act = (c * jax.nn.sigmoid(c)).astype(BF16)
    o_ref[...] = jnp.dot(c_act, w_ref[...].astype(BF16), preferred_element_type=F32) + b_ref[...]


def _modulation(c, w_ada, b_ada):
    L, D, N = w_ada.shape
    B = c.shape[0]
    rows = 8
    c_pad = jnp.zeros((rows, D), F32).at[:B].set(c)
    tn = _tile(N, 1024)
    out = _call(
        _mod_body, grid=(L, N // tn),
        in_specs=[pl.BlockSpec((rows, D), lambda l, n: (0, 0)),
                  pl.BlockSpec((None, D, tn), lambda l, n: (l, 0, n)),
                  pl.BlockSpec((None, 1, tn), lambda l, n: (l, 0, n))],
        out_specs=pl.BlockSpec((None, rows, tn), lambda l, n: (l, 0, n)),
        out_shape=jax.ShapeDtypeStruct((L, rows, N), F32),
        name="adaln_mod")(c_pad, w_ada, b_ada.reshape(L, 1, N))
    return out[:, :B, :]


def _latent_up_body(lat_ref, wqm_ref, wqr_ref, wkv_ref, cos_ref, sin_ref, q_ref, kv_ref, wkvb_ref):
    @pl.when(pl.program_id(0) == 0)
    def _():
        wkvb_ref[...] = wkv_ref[...].astype(BF16)

    cq = lat_ref[:, :Q_RANK]
    main = jnp.dot(cq, wqm_ref[...], preferred_element_type=F32)
    rot = jnp.dot(cq, wqr_ref[...], preferred_element_type=F32)
    cos, sin = cos_ref[...], sin_ref[...]
    for h in range(MLA_HEADS):
        c0 = h * QPAD
        q_ref[:, c0:c0 + LANE] = (main[:, c0:c0 + LANE] * MLA_SCALE).astype(q_ref.dtype)
        q_ref[:, c0 + LANE:c0 + QPAD] = (main[:, c0 + LANE:c0 + QPAD] * cos
                                         + rot[:, h * LANE:(h + 1) * LANE] * sin).astype(q_ref.dtype)
    kv_ref[...] = jnp.dot(lat_ref[:, Q_RANK:], wkvb_ref[...],
                          preferred_element_type=F32).astype(kv_ref.dtype)


def _latent_up(lat, wq_main, wq_rot, w_ukv, layer, cos_q, sin_q, tm):
    M = lat.shape[0]
    nq, nr, nkv = wq_main.shape[-1], wq_rot.shape[-1], w_ukv.shape[-1]
    whole = lambda k, n: pl.BlockSpec((None, k, n), lambda m: (layer, 0, 0))
    rows = lambda n: pl.BlockSpec((tm, n), lambda m: (m, 0))
    return _call(
        _latent_up_body, grid=(M // tm,),
        in_specs=[rows(Q_RANK + KV_RANK), whole(Q_RANK, nq), whole(Q_RANK, nr),
                  whole(KV_RANK, nkv), rows(LANE), rows(LANE)],
        out_specs=[rows(nq), rows(nkv)],
        out_shape=[jax.ShapeDtypeStruct((M, nq), BF16), jax.ShapeDtypeStruct((M, nkv), BF16)],
        scratch=[pltpu.VMEM((KV_RANK, nkv), BF16)],
        name="latent_up")(lat, wq_main, wq_rot, w_ukv, cos_q, sin_q)


def _prenorm_body(x_ref, g_ref, sc_ref, sh_ref, o_ref):
    o_ref[...] = _norm_mod(x_ref[...], g_ref[...], sc_ref[...], sh_ref[...]).astype(o_ref.dtype)


def _prenorm(x, gain, sc, sh, S, out_dtype):
    M, D = x.shape
    tm = _tile(S, 512)
    bpt = S // tm
    vec = pl.BlockSpec((None, 1, D), lambda m: (m // bpt, 0, 0))
    return _call(
        _prenorm_body, grid=(M // tm,),
        in_specs=[pl.BlockSpec((tm, D), lambda m: (m, 0)),
                  pl.BlockSpec((1, D), lambda m: (0, 0)), vec, vec],
        out_specs=pl.BlockSpec((tm, D), lambda m: (m, 0)),
        out_shape=jax.ShapeDtypeStruct((M, D), out_dtype),
        name="prenorm")(x, gain, sc, sh)


SUB = 256
ATT_HEADS = 2


def _qk(q, k):
    return lax.dot_general(q, k, (((1,), (1,)), ((), ())), preferred_element_type=F32)


def _sb_chains(ops, tri, states, masks):
    zs = [_qk(q, k) for q, k, v in ops]
    sps = []
    for z, mask in zip(zs, masks):
        sp = jnp.maximum(z, 0.0) + jnp.log2(1.0 + jnp.exp2(jnp.minimum(z, -z)))
        if mask is not None:
            sp = jnp.where(mask, sp, 0.0)
        sps.append(sp.astype(BF16))
    csums, carries = [], []
    for spb, (carry, acc) in zip(sps, states):
        parts = []
        for c in reversed(range(spb.shape[1] // SUB)):
            loc = jnp.dot(spb[:, c * SUB:(c + 1) * SUB], tri, preferred_element_type=F32)
            parts.insert(0, loc + carry)
            carry = carry + loc[:, 0:1]
        csums.append(parts[0] if len(parts) == 1 else jnp.concatenate(parts, axis=1))
        carries.append(carry)
    out = []
    for z, csum, mask, carry, (q, k, v), (_, acc) in zip(zs, csums, masks, carries, ops, states):
        a = jnp.exp2(z - csum)
        if mask is not None:
            a = jnp.where(mask, a, 0.0)
        out.append((carry, acc + jnp.dot(a.astype(BF16), v, preferred_element_type=F32)))
    return tuple(out)


def _sb_body(q_ref, k_ref, v_ref, o_ref, *, t):
    i = pl.program_id(2)
    halves = t // SUB
    r = lax.broadcasted_iota(jnp.int32, (SUB, SUB), 0)
    c = lax.broadcasted_iota(jnp.int32, (SUB, SUB), 1)
    tri = jnp.where(r >= c, 1.0, 0.0).astype(BF16)
    start = pl.multiple_of(i * t, t)
    chains = [(g, a) for g in range(ATT_HEADS) for a in range(halves)]

    def operands(g, a, row0, nrows):
        cols = slice(g * HEAD_DIM, (g + 1) * HEAD_DIM)
        return (q_ref[a * SUB:(a + 1) * SUB, cols], k_ref[pl.ds(row0, nrows), cols],
                v_ref[pl.ds(row0, nrows), cols])

    ops, masks, st0 = [], [], []
    for g, a in chains:
        n = (a + 1) * SUB
        rr = lax.broadcasted_iota(jnp.int32, (SUB, n), 0)
        cc = lax.broadcasted_iota(jnp.int32, (SUB, n), 1)
        ops.append(operands(g, a, start, n))
        masks.append(cc < rr + a * SUB)
        st0.append((jnp.zeros((SUB, 1), F32), jnp.zeros((SUB, HEAD_DIM), F32)))
    state = _sb_chains(ops, tri, st0, masks)

    def step(jj, st):
        row0 = pl.multiple_of((i - 1 - jj) * t, t)
        return _sb_chains([operands(g, a, row0, t) for g, a in chains], tri, st,
                          [None] * len(chains))

    state = lax.fori_loop(0, i, step, state)
    for n, (g, a) in enumerate(chains):
        o_ref[a * SUB:(a + 1) * SUB, g * HEAD_DIM:(g + 1) * HEAD_DIM] = state[n][1].astype(o_ref.dtype)


def _sb_attention(qkv, B, S):
    t = _tile(S, 2 * SUB)
    nq = S // t
    G = ATT_HEADS
    HG = SB_HEADS // G
    W = G * HEAD_DIM
    return _call(
        functools.partial(_sb_body, t=t), grid=(B, HG, nq),
        in_specs=[pl.BlockSpec((t, W), lambda b, h, i: (b * nq + i, h)),
                  pl.BlockSpec((S, W), lambda b, h, i: (b, HG + h)),
                  pl.BlockSpec((S, W), lambda b, h, i: (b, 2 * HG + h))],
        out_specs=pl.BlockSpec((t, W), lambda b, h, i: (b * nq + i, h)),
        out_shape=jax.ShapeDtypeStruct((B * S, SB_WIDTH), BF16),
        name="sb_attention")(qkv, qkv, qkv)


def _mla_chains(ops, states, masks):
    ss = []
    for (q, k, v), mask in zip(ops, masks):
        s = _qk(q, k)
        if mask is not None:
            s = jnp.where(mask, s, -jnp.inf)
        ss.append(s)
    ms = [jnp.maximum(m, jnp.max(s, axis=-1, keepdims=True)) for s, (m, l, acc) in zip(ss, states)]
    ps = [jnp.exp2(s - m_new) for s, m_new in zip(ss, ms)]
    out = []
    for p, m_new, (q, k, v), (m, l, acc) in zip(ps, ms, ops, states):
        alpha = jnp.exp2(m - m_new)
        l = alpha * l + jnp.sum(p, axis=-1, keepdims=True)
        acc = alpha * acc + jnp.dot(p.astype(BF16), v, preferred_element_type=F32)
        out.append((m_new, l, acc))
    return tuple(out)


def _mla_body(q_ref, kv_ref, kpe_ref, o_ref, kf_ref, *, t):
    i = pl.program_id(2)
    halves = t // SUB
    hw = MLA_NOPE + MLA_V

    @pl.when(i == 0)
    def _():
        for g in range(ATT_HEADS):
            kf_ref[g, :, 0:LANE] = kv_ref[:, g * hw:g * hw + MLA_NOPE]
            kf_ref[g, :, LANE:QPAD] = kpe_ref[...]

    start = pl.multiple_of(i * t, t)
    chains = [(g, a) for g in range(ATT_HEADS) for a in range(halves)]

    def operands(g, a, row0, nrows):
        return (q_ref[a * SUB:(a + 1) * SUB, g * QPAD:(g + 1) * QPAD],
                kf_ref[g, pl.ds(row0, nrows), :],
                kv_ref[pl.ds(row0, nrows), g * hw + MLA_NOPE:(g + 1) * hw])

    ops, masks, st0 = [], [], []
    for g, a in chains:
        n = (a + 1) * SUB
        rr = lax.broadcasted_iota(jnp.int32, (SUB, n), 0)
        cc = lax.broadcasted_iota(jnp.int32, (SUB, n), 1)
        ops.append(operands(g, a, start, n))
        masks.append(cc <= rr + a * SUB)
        st0.append((jnp.full((SUB, 1), -jnp.inf, F32), jnp.zeros((SUB, 1), F32),
                    jnp.zeros((SUB, MLA_V), F32)))
    state = _mla_chains(ops, st0, masks)

    def step(j, st):
        row0 = pl.multiple_of(j * t, t)
        return _mla_chains([operands(g, a, row0, t) for g, a in chains], st, [None] * len(chains))

    state = lax.fori_loop(0, i, step, state)
    for n, (g, a) in enumerate(chains):
        m, l, acc = state[n]
        o_ref[a * SUB:(a + 1) * SUB, g * MLA_V:(g + 1) * MLA_V] = (acc / l).astype(o_ref.dtype)


def _mla_attention(q_full, kv, kpe, B, S):
    t = _tile(S, 2 * SUB)
    nq = S // t
    G = ATT_HEADS
    HG = MLA_HEADS // G
    return _call(
        functools.partial(_mla_body, t=t), grid=(B, HG, nq),
        in_specs=[pl.BlockSpec((t, G * QPAD), lambda b, h, i: (b * nq + i, h)),
                  pl.BlockSpec((S, G * (MLA_NOPE + MLA_V)), lambda b, h, i: (b, h)),
                  pl.BlockSpec((S, LANE), lambda b, h, i: (b, 0))],
        out_specs=pl.BlockSpec((t, G * MLA_V), lambda b, h, i: (b * nq + i, h)),
        out_shape=jax.ShapeDtypeStruct((B * S, MLA_HEADS * MLA_V), BF16),
        scratch=[pltpu.VMEM((G, S, QPAD), BF16)],
        name="mla_attention")(q_full, kv, kpe)


def _merge_body(osb_ref, omla_ref, gate_sb_ref, gate_mla_ref, wsb_ref, wmla_ref, wo_ref, x_ref,
                g1_ref, gain_ref, sc_ref, sh_ref, xo_ref, ho_ref):
    y = (gate_sb_ref[...].astype(F32)
         * jnp.dot(osb_ref[...], wsb_ref[...], preferred_element_type=F32)
         + gate_mla_ref[...].astype(F32)
         * jnp.dot(omla_ref[...], wmla_ref[...], preferred_element_type=F32))
    out = jnp.dot(y.astype(BF16), wo_ref[...], preferred_element_type=F32)
    x_new = x_ref[...] + g1_ref[...] * out
    xo_ref[...] = x_new
    ho_ref[...] = _norm_mod(x_new, gain_ref[...], sc_ref[...], sh_ref[...]).astype(ho_ref.dtype)


def _merge(o_sb, o_mla, gates, w_sb, w_mla, w_o, layer, x, g1, gain, sc, sh, S, h_dtype):
    M, D = x.shape
    tm = _tile(S, 256)
    bpt = S // tm
    nd = D // D
    vec = pl.BlockSpec((None, 1, D), lambda m: (m // bpt, 0, 0))
    res = lambda k: pl.BlockSpec((None, k, D), lambda m: (layer, 0, 0),
                                 pipeline_mode=pl.Buffered(1))
    return _call(
        _merge_body, grid=(M // tm,),
        in_specs=[pl.BlockSpec((tm, SB_WIDTH), lambda m: (m, 0)),
                  pl.BlockSpec((tm, MLA_HEADS * MLA_V), lambda m: (m, 0)),
                  pl.BlockSpec((tm, D), lambda m: (m, 0)),
                  pl.BlockSpec((tm, D), lambda m: (m, nd)),
                  res(SB_WIDTH), res(MLA_HEADS * MLA_V), res(D),
                  pl.BlockSpec((tm, D), lambda m: (m, 0)),
                  vec, pl.BlockSpec((1, D), lambda m: (0, 0)), vec, vec],
        out_specs=[pl.BlockSpec((tm, D), lambda m: (m, 0)),
                   pl.BlockSpec((tm, D), lambda m: (m, 0))],
        out_shape=[jax.ShapeDtypeStruct((M, D), F32), jax.ShapeDtypeStruct((M, D), h_dtype)],
        name="merge_out")(o_sb, o_mla, gates, gates, w_sb, w_mla, w_o, x, g1, gain, sc, sh)


def _resnorm_body(f_ref, x_ref, g_ref, gain_ref, sc_ref, sh_ref, xo_ref, ho_ref):
    x_new = x_ref[...] + g_ref[...] * f_ref[...]
    xo_ref[...] = x_new
    ho_ref[...] = _norm_mod(x_new, gain_ref[...], sc_ref[...], sh_ref[...]).astype(ho_ref.dtype)


def _resnorm(f, x, g, gain, sc, sh, S, h_dtype):
    M, D = x.shape
    tm = _tile(S, 512)
    bpt = S // tm
    vec = pl.BlockSpec((None, 1, D), lambda m: (m // bpt, 0, 0))
    rows = pl.BlockSpec((tm, D), lambda m: (m, 0))
    return _call(
        _resnorm_body, grid=(M // tm,),
        in_specs=[rows, rows, vec, pl.BlockSpec((1, D), lambda m: (0, 0)), vec, vec],
        out_specs=[rows, rows],
        out_shape=[jax.ShapeDtypeStruct((M, D), F32), jax.ShapeDtypeStruct((M, D), h_dtype)],
        name="resnorm")(f, x, g, gain, sc, sh)


def _split_bf16(v):
    hi = v.astype(BF16)
    return hi, (v - hi.astype(F32)).astype(BF16)


def _router_body(h_ref, w_ref, idx_ref, wt_ref):
    hh, hl = _split_bf16(h_ref[...])
    wh, wl = _split_bf16(w_ref[...])
    dot = functools.partial(jnp.dot, preferred_element_type=F32)
    logits = dot(hh, wh) + (dot(hl, wh) + dot(hh, wl))
    lane = lax.broadcasted_iota(jnp.int32, logits.shape, 1).astype(F32)
    logits = jnp.where(lane < N_EXPERTS, logits, -jnp.inf)
    m1 = jnp.max(logits, axis=-1, keepdims=True)
    i1 = jnp.min(jnp.where(logits == m1, lane, float(LANE)), axis=-1, keepdims=True)
    rest = jnp.where(lane == i1, -jnp.inf, logits)
    m2 = jnp.max(rest, axis=-1, keepdims=True)
    i2 = jnp.min(jnp.where(rest == m2, lane, float(LANE)), axis=-1, keepdims=True)
    e = jnp.exp(m2 - m1)
    w1 = 1.0 / (1.0 + e)
    w2 = e * w1
    idx_ref[...] = jnp.where(lane == 0.0, i1, jnp.where(lane == 1.0, i2, 0.0)).astype(jnp.int32)
    wt_ref[...] = jnp.where(lane == 0.0, w1, jnp.where(lane == 1.0, w2, 0.0))


def _router(h, w_router_pad, j):
    M, D = h.shape
    tm = _tile(M, 512)
    return _call(
        _router_body, grid=(M // tm,),
        in_specs=[pl.BlockSpec((tm, D), lambda m: (m, 0)),
                  pl.BlockSpec((None, D, LANE), lambda m: (j, 0, 0))],
        out_specs=[pl.BlockSpec((tm, LANE), lambda m: (m, 0)),
                   pl.BlockSpec((tm, LANE), lambda m: (m, 0))],
        out_shape=[jax.ShapeDtypeStruct((M, LANE), jnp.int32),
                   jax.ShapeDtypeStruct((M, LANE), F32)],
        name="router")(h, w_router_pad)


MOE_SUB = 256
ROW_UNROLL = 8


def _gather_body(src_ref, h_hbm, o_ref, buf_ref, sem, *, tm):
    t = pl.program_id(0)

    def issue(tile):
        slot = tile % 2

        def blk(b, _):
            for u in range(ROW_UNROLL):
                r = b * ROW_UNROLL + u
                pltpu.make_async_copy(h_hbm.at[pl.ds(src_ref[tile * tm + r], 1)],
                                      buf_ref.at[slot, pl.ds(r, 1)],
                                      sem.at[slot]).start(priority=u % 2)
            return 0

        lax.fori_loop(0, tm // ROW_UNROLL, blk, 0)

    @pl.when(t == 0)
    def _():
        issue(t)

    @pl.when(t + 1 < pl.num_programs(0))
    def _():
        issue(t + 1)

    slot = t % 2
    pltpu.make_async_copy(h_hbm.at[pl.ds(0, tm)], buf_ref.at[slot], sem.at[slot]).wait()
    o_ref[...] = buf_ref[slot].astype(o_ref.dtype)


def _gather_rows(src_tok, h, n_tiles, tm):
    M, D = h.shape
    return _call(
        functools.partial(_gather_body, tm=tm), grid=(n_tiles,), nsp=1,
        in_specs=[pl.BlockSpec(memory_space=pl.ANY)],
        out_specs=pl.BlockSpec((tm, D), lambda t, src: (t, 0)),
        out_shape=jax.ShapeDtypeStruct((n_tiles * tm, D), BF16),
        scratch=[pltpu.VMEM((2, tm, D), F32), pltpu.SemaphoreType.DMA((2,))],
        name="moe_gather")(src_tok, h)


def _grouped_body(te_ref, a_ref, *refs, n_w, epilogue):
    w_refs = refs[:n_w]
    rest = refs[n_w:]
    n_rest = len(rest) - n_w
    wb_refs = rest[n_rest:]
    t = pl.program_id(1)
    n_valid = te_ref[pl.num_programs(1) + t]
    changed = jnp.logical_or(t == 0, te_ref[t] != te_ref[jnp.maximum(t - 1, 0)])

    @pl.when(changed)
    def _():
        for w_ref, wb_ref in zip(w_refs, wb_refs):
            wb_ref[...] = w_ref[...].astype(BF16)

    for s in range(a_ref.shape[0] // MOE_SUB):
        rows = slice(s * MOE_SUB, (s + 1) * MOE_SUB)

        @pl.when(s * MOE_SUB < n_valid)
        def _():
            a = a_ref[rows, :]
            accs = [jnp.dot(a, wb_ref[...], preferred_element_type=F32) for wb_ref in wb_refs]
            epilogue(accs, rest[:n_rest], rows)

        @pl.when(s * MOE_SUB >= n_valid)
        def _():
            for o_ref in rest[:n_rest]:
                o_ref[rows, :] = jnp.zeros((MOE_SUB, o_ref.shape[1]), o_ref.dtype)


def _up_epilogue(accs, refs, rows=slice(None)):
    g, u = accs
    refs[0][rows, :] = (g * jax.nn.sigmoid(g) * u).astype(refs[0].dtype)


def _grouped_up(tile_e, xs, w_gate, w_up, j, tm, tf):
    NP, D = xs.shape
    F = w_gate.shape[-1]
    wspec = pl.BlockSpec((None, None, D, tf), lambda f, t, te: (j, te[t], 0, f))
    return _call(
        functools.partial(_grouped_body, n_w=2, epilogue=_up_epilogue),
        grid=(F // tf, NP // tm), nsp=1,
        in_specs=[pl.BlockSpec((tm, D), lambda f, t, te: (t, 0)), wspec, wspec],
        out_specs=pl.BlockSpec((tm, tf), lambda f, t, te: (t, f)),
        out_shape=jax.ShapeDtypeStruct((NP, F), BF16),
        scratch=[pltpu.VMEM((D, tf), BF16), pltpu.VMEM((D, tf), BF16)],
        name="moe_up")(tile_e, xs, w_gate, w_up)


def _down_epilogue(accs, refs, rows):
    refs[0][rows, :] = accs[0]


def _grouped_down(tile_e, hs, w_down, j, tm, tn):
    NP, F = hs.shape
    D = w_down.shape[-1]
    return _call(
        functools.partial(_grouped_body, n_w=1, epilogue=_down_epilogue),
        grid=(D // tn, NP // tm), nsp=1,
        in_specs=[pl.BlockSpec((tm, F), lambda n, t, te: (t, 0)),
                  pl.BlockSpec((None, None, F, tn), lambda n, t, te: (j, te[t], 0, n))],
        out_specs=pl.BlockSpec((tm, tn), lambda n, t, te: (t, n)),
        out_shape=jax.ShapeDtypeStruct((NP, D), F32),
        scratch=[pltpu.VMEM((F, tn), BF16)],
        name="moe_down")(tile_e, hs, w_down)


def _combine_body(pos_ref, y_hbm, wt_ref, x_ref, g2_ref, gain_ref, sc_ref, sh_ref, xo_ref, ho_ref,
                  buf_ref, sem, *, tm):
    t = pl.program_id(0)

    def issue(tile):
        slot = tile % 2

        def blk(b, _):
            for u in range(ROW_UNROLL):
                r = b * ROW_UNROLL + u
                for k in range(TOP_K):
                    pltpu.make_async_copy(
                        y_hbm.at[pl.ds(pos_ref[(tile * tm + r) * TOP_K + k], 1)],
                        buf_ref.at[slot, k, pl.ds(r, 1)], sem.at[slot]).start(priority=k % 2)
            return 0

        lax.fori_loop(0, tm // ROW_UNROLL, blk, 0)

    @pl.when(t == 0)
    def _():
        issue(t)

    @pl.when(t + 1 < pl.num_programs(0))
    def _():
        issue(t + 1)

    slot = t % 2
    for k in range(TOP_K):
        pltpu.make_async_copy(y_hbm.at[pl.ds(0, tm)], buf_ref.at[slot, k], sem.at[slot]).wait()
    wt = wt_ref[...]
    f = buf_ref[slot, 0] * wt[:, 0:1]
    for k in range(1, TOP_K):
        f = f + buf_ref[slot, k] * wt[:, k:k + 1]
    x_new = x_ref[...] + g2_ref[...] * f
    xo_ref[...] = x_new
    ho_ref[...] = _norm_mod(x_new, gain_ref[...], sc_ref[...], sh_ref[...]).astype(ho_ref.dtype)


def _combine(pos, y_sorted, wts, x, g2, gain, sc, sh, S, h_dtype):
    M, D = x.shape
    tm = _tile(S, 256)
    bpt = S // tm
    vec = pl.BlockSpec((None, 1, D), lambda m, p: (m // bpt, 0, 0))
    return _call(
        functools.partial(_combine_body, tm=tm), grid=(M // tm,), nsp=1,
        in_specs=[pl.BlockSpec(memory_space=pl.ANY),
                  pl.BlockSpec((tm, LANE), lambda m, p: (m, 0)),
                  pl.BlockSpec((tm, D), lambda m, p: (m, 0)),
                  vec, pl.BlockSpec((1, D), lambda m, p: (0, 0)), vec, vec],
        out_specs=[pl.BlockSpec((tm, D), lambda m, p: (m, 0)),
                   pl.BlockSpec((tm, D), lambda m, p: (m, 0))],
        out_shape=[jax.ShapeDtypeStruct((M, D), F32), jax.ShapeDtypeStruct((M, D), h_dtype)],
        scratch=[pltpu.VMEM((2, TOP_K, tm, D), F32), pltpu.SemaphoreType.DMA((2,))],
        name="moe_combine")(pos, y_sorted, wts, x, g2, gain, sc, sh)


def _dispatch_plan(top_i, tm):
    M = top_i.shape[0]
    E = N_EXPERTS
    n_ent = M * TOP_K
    e_flat = top_i.reshape(-1)
    onehot = (e_flat[:, None] == jnp.arange(E, dtype=jnp.int32)[None, :]).astype(jnp.int32)
    csum = jnp.cumsum(onehot, axis=0)
    rank = jnp.sum(onehot * csum, axis=1) - 1
    counts = csum[-1]
    pcounts = ((counts + tm - 1) // tm) * tm
    pend = jnp.cumsum(pcounts)
    pstart = pend - pcounts
    dest = (pstart[e_flat] + rank).astype(jnp.int32)
    n_tiles = n_ent // tm + E
    n_rows = n_tiles * tm
    src_tok = jnp.zeros((n_rows,), jnp.int32).at[dest].set(
        jnp.arange(n_ent, dtype=jnp.int32) // TOP_K)
    tile_start = jnp.arange(n_tiles, dtype=jnp.int32) * tm
    tile_e = jnp.sum((tile_start[:, None] >= pend[None, :]).astype(jnp.int32), axis=1)
    tile_e = jnp.minimum(tile_e, E - 1).astype(jnp.int32)
    n_valid = jnp.clip(pstart[tile_e] + counts[tile_e] - tile_start, 0, tm).astype(jnp.int32)
    return src_tok, dest, jnp.concatenate([tile_e, n_valid]), n_tiles


def _rot_cols(w):
    half = MLA_ROPE // 2
    return jnp.concatenate([-w[..., half:], w[..., :half]], axis=-1)


def _rope_tables(positions):
    inv_freq = 1.0 / (ROPE_THETA ** (jnp.arange(0, MLA_ROPE, 2, dtype=F32) / MLA_ROPE))
    ang = positions.astype(F32)[..., None] * inv_freq
    cos, sin = jnp.cos(ang), jnp.sin(ang)
    M = positions.size
    cos2 = jnp.concatenate([cos, cos], axis=-1).reshape(M, MLA_ROPE)
    sin2 = jnp.concatenate([sin, sin], axis=-1).reshape(M, MLA_ROPE)
    zeros = jnp.zeros((M, LANE - MLA_ROPE), F32)
    cos_k = jnp.concatenate([cos2, zeros], axis=-1)
    sin_k = jnp.concatenate([sin2, zeros], axis=-1)
    return cos_k, sin_k, cos_k * MLA_SCALE, sin_k * MLA_SCALE


def kernel(x, c, positions, w_ada, b_ada, norm_mix_g, norm_ffn_g, w_in, q_norm_g, kv_norm_g, w_uq, w_ukv, w_sb_up, w_mla_up, w_o, w_ffn_gate, w_ffn_up, w_ffn_down, w_router, w_exp_gate, w_exp_up, w_exp_down, final_norm_g):
    B, S, D = x.shape
    L = w_ada.shape[0]
    M = B * S
    x = x.reshape(M, D)

    mod = _modulation(c, w_ada, b_ada)
    mod = mod.reshape(L, B, N_MOD, 1, D)
    sh1, sc1, g1, sh2, sc2, g2 = [mod[:, :, k] for k in range(N_MOD)]
    cos_k, sin_k, cos_q, sin_q = _rope_tables(positions)

    pe0 = 3 * SB_WIDTH + Q_RANK + KV_RANK
    w_pe = w_in[:, :, pe0:pe0 + MLA_ROPE]
    zpad = jnp.zeros(w_pe.shape[:2] + (LANE - MLA_ROPE,), F32)
    w_pe2 = jnp.concatenate([w_pe, zpad, _rot_cols(w_pe), zpad], axis=-1).astype(BF16)
    w_gates = w_in[:, :, pe0 + MLA_ROPE:].astype(BF16)
    c_gain = jnp.concatenate([q_norm_g, kv_norm_g], axis=-1)[:, None, :]

    wq = w_uq.reshape(L, Q_RANK, MLA_HEADS, MLA_QK)
    wq_n, wq_r = wq[..., :MLA_NOPE], wq[..., MLA_NOPE:]
    zq = jnp.zeros(wq_r.shape, F32)
    wq_main = jnp.concatenate([wq_n, wq_r, zq], axis=-1).reshape(
        L, Q_RANK, MLA_HEADS * QPAD).astype(BF16)
    wq_rot = jnp.concatenate([_rot_cols(wq_r), zq], axis=-1).reshape(
        L, Q_RANK, MLA_HEADS * LANE).astype(BF16)

    w_sb_b, w_mla_b, w_o_b = (w.astype(BF16) for w in (w_sb_up, w_mla_up, w_o))
    w_router_pad = jnp.concatenate(
        [w_router, jnp.zeros(w_router.shape[:2] + (LANE - N_EXPERTS,), F32)], axis=-1)

    tm = _tile(S, 1024)
    bpt = S // tm
    row = lambda width: pl.BlockSpec((tm, width), lambda n, m: (m, 0))
    sb_scale = HEAD_DIM ** -0.5 * LOG2E

    def h_dtype_for(layer_is_moe):
        return F32 if layer_is_moe else BF16

    h = _prenorm(x, norm_mix_g[0][None, :], sc1[0], sh1[0], S, BF16)

    for l in range(L):
        tn = _tile(min(SB_WIDTH, 2 * D), 1024)
        nq_blk = SB_WIDTH // tn

        def qkv_epi(accs, n, extra, outs):
            s = jnp.where(n < nq_blk, sb_scale, 1.0)
            outs[0][...] = (accs[0] * s).astype(BF16)

        (qkv,) = _matmul(h, [w_in], layer=l, col_blk0=0, n_cols=3 * SB_WIDTH, tn=tn, tm=tm,
                         epilogue=qkv_epi,
                         out_shape=[jax.ShapeDtypeStruct((M, 3 * SB_WIDTH), BF16)],
                         out_specs=[pl.BlockSpec((tm, tn), lambda n, m: (m, n))], name="in_qkv")

        def lat_epi(accs, n, extra, outs):
            a = accs[0]
            ms = jnp.mean(a * a, axis=-1, keepdims=True)
            outs[0][...] = (a * lax.rsqrt(ms + NORM_EPS) * extra[0][...]).astype(BF16)

        (lat,) = _matmul(h, [w_in], layer=l, col_blk0=3 * SB_WIDTH // Q_RANK,
                         n_cols=Q_RANK + KV_RANK, tn=Q_RANK, tm=tm, epilogue=lat_epi,
                         extras=[c_gain],
                         extra_specs=[pl.BlockSpec((None, 1, Q_RANK), lambda n, m: (l, 0, n))],
                         out_shape=[jax.ShapeDtypeStruct((M, Q_RANK + KV_RANK), BF16)],
                         out_specs=[pl.BlockSpec((tm, Q_RANK), lambda n, m: (m, n))],
                         name="in_latent")

        def pe_epi(accs, n, extra, outs):
            a = accs[0]
            outs[0][...] = (a[:, :LANE] * extra[0][...] + a[:, LANE:] * extra[1][...]).astype(BF16)

        (kpe,) = _matmul(h, [w_pe2], layer=l, col_blk0=0, n_cols=2 * LANE, tn=2 * LANE, tm=tm,
                         epilogue=pe_epi, extras=[cos_k, sin_k], extra_specs=[row(LANE), row(LANE)],
                         out_shape=[jax.ShapeDtypeStruct((M, LANE), BF16)],
                         out_specs=[pl.BlockSpec((tm, LANE), lambda n, m: (m, 0))], name="in_kpe")

        def gate_epi(accs, n, extra, outs):
            outs[0][...] = jax.nn.sigmoid(accs[0]).astype(BF16)

        (gates,) = _matmul(h, [w_gates], layer=l, col_blk0=0, n_cols=2 * D, tn=tn, tm=tm,
                           epilogue=gate_epi,
                           out_shape=[jax.ShapeDtypeStruct((M, 2 * D), BF16)],
                           out_specs=[pl.BlockSpec((tm, tn), lambda n, m: (m, n))],
                           name="in_gates")

        q_full, kv = _latent_up(lat, wq_main, wq_rot, w_ukv, l, cos_q, sin_q, tm)

        o_sb = _sb_attention(qkv, B, S)
        o_mla = _mla_attention(q_full, kv, kpe, B, S)

        moe = (l % 2 == 1)
        x, h = _merge(o_sb, o_mla, gates, w_sb_b, w_mla_b, w_o_b, l, x, g1[l],
                      norm_ffn_g[l][None, :], sc2[l], sh2[l], S, h_dtype_for(moe))

        last = (l == L - 1)
        if last:
            n_gain = final_norm_g[None, :]
            n_sc = jnp.zeros_like(sc1[0])
            n_sh = jnp.zeros_like(sh1[0])
            n_dtype = F32
        else:
            n_gain, n_sc, n_sh, n_dtype = norm_mix_g[l + 1][None, :], sc1[l + 1], sh1[l + 1], BF16
        j = l // 2
        if not moe:
            F = w_ffn_gate.shape[-1]
            tf = _tile(F, 512)

            def up_epi(accs, n, extra, outs):
                _up_epilogue(accs, outs)

            (hmid,) = _matmul(h, [w_ffn_gate, w_ffn_up], layer=j, col_blk0=0, n_cols=F, tn=tf,
                              tm=tm, epilogue=up_epi,
                              out_shape=[jax.ShapeDtypeStruct((M, F), BF16)],
                              out_specs=[pl.BlockSpec((tm, tf), lambda n, m: (m, n))],
                              name="ffn_up")
            def down_epi(accs, n, extra, outs):
                outs[0][...] = accs[0]

            tnd = _tile(D, 512)
            (f,) = _matmul(hmid, [w_ffn_down], layer=j, col_blk0=0, n_cols=D, tn=tnd,
                           tm=_tile(S, 512), epilogue=down_epi,
                           out_shape=[jax.ShapeDtypeStruct((M, D), F32)],
                           out_specs=[pl.BlockSpec((_tile(S, 512), tnd), lambda n, m: (m, n))],
                           name="ffn_down")
            x, h = _resnorm(f, x, g2[l], n_gain, n_sc, n_sh, S, n_dtype)
        else:
            F = w_exp_gate.shape[-1]
            tme = max(_tile(M, 2 * MOE_SUB), MOE_SUB)
            idx, wts = _router(h, w_router_pad, j)
            src_tok, dest, tile_e, n_tiles = _dispatch_plan(idx[:, :TOP_K], tme)
            xs = _gather_rows(src_tok, h, n_tiles, tme)
            hs = _grouped_up(tile_e, xs, w_exp_gate, w_exp_up, j, tme, _tile(F, 512))
            ys = _grouped_down(tile_e, hs, w_exp_down, j, tme, _tile(D, 512))
            x, h = _combine(dest, ys, wts, x, g2[l], n_gain, n_sc, n_sh, S, n_dtype)

    return h.reshape(B, S, D)
```

```python
import functools

import jax
import jax.numpy as jnp
from jax import lax
from jax.experimental import pallas as pl
from jax.experimental.pallas import tpu as pltpu

BF16 = jnp.bfloat16
F32 = jnp.float32

SB_HEADS = 8
HEAD_DIM = 128
SB_WIDTH = SB_HEADS * HEAD_DIM
MLA_HEADS = 8
MLA_NOPE = 128
MLA_ROPE = 64
MLA_V = 128
MLA_QK = MLA_NOPE + MLA_ROPE
Q_RANK = 512
KV_RANK = 512
ROPE_THETA = 10000.0
N_EXPERTS = 8
TOP_K = 2
N_MOD = 6
NORM_EPS = 1e-6

LOG2E = 1.4426950408889634
MLA_SCALE = MLA_QK ** -0.5 * LOG2E

LANE = 128
QPAD = 2 * LANE
VMEM_LIMIT = 56 << 20


def _call(body, *, grid, in_specs, out_specs, out_shape, scratch=(), nsp=0, name, flags=None):
    return pl.pallas_call(
        body,
        grid_spec=pltpu.PrefetchScalarGridSpec(
            num_scalar_prefetch=nsp, grid=grid, in_specs=in_specs,
            out_specs=out_specs, scratch_shapes=scratch),
        out_shape=out_shape,
        compiler_params=pltpu.CompilerParams(
            dimension_semantics=("arbitrary",) * len(grid),
            vmem_limit_bytes=VMEM_LIMIT, flags=flags),
        name=name)


def _tile(n, pref):
    t = min(n, pref)
    while n % t:
        t //= 2
    return t


def _norm_mod(x, gain, sc, sh):
    ms = jnp.mean(x * x, axis=-1, keepdims=True)
    y = x * lax.rsqrt(ms + NORM_EPS)
    return (y * gain) * (1.0 + sc) + sh


def _mm_body(*refs, n_w, n_extra, n_out, epilogue):
    a_ref = refs[0]
    w_refs = refs[1:1 + n_w]
    extra = refs[1 + n_w:1 + n_w + n_extra]
    outs = refs[1 + n_w + n_extra:1 + n_w + n_extra + n_out]
    wb_refs = refs[1 + n_w + n_extra + n_out:]

    @pl.when(pl.program_id(1) == 0)
    def _():
        for w_ref, wb_ref in zip(w_refs, wb_refs):
            wb_ref[...] = w_ref[...].astype(BF16)

    a = a_ref[...]
    accs = [jnp.dot(a, wb_ref[...], preferred_element_type=F32) for wb_ref in wb_refs]
    epilogue(accs, pl.program_id(0), extra, outs)


def _matmul(a, ws, *, layer, col_blk0, n_cols, tn, tm, epilogue, extras=(), extra_specs=(),
            out_shape, out_specs, name, a_blk=0):
    M = a.shape[0]
    K = ws[0].shape[1]
    grid = (n_cols // tn, M // tm)
    in_specs = [pl.BlockSpec((tm, K), lambda n, m: (m, a_blk))]
    for _ in ws:
        in_specs.append(pl.BlockSpec((None, K, tn), lambda n, m: (layer, 0, n + col_blk0)))
    in_specs += list(extra_specs)
    body = functools.partial(_mm_body, n_w=len(ws), n_extra=len(extras), n_out=len(out_shape),
                             epilogue=epilogue)
    return _call(body, grid=grid, in_specs=in_specs, out_specs=out_specs, out_shape=out_shape,
                 scratch=[pltpu.VMEM((K, tn), BF16) for _ in ws], name=name)(a, *ws, *extras)


def _mod_body(c_ref, w_ref, b_ref, o_ref):
    c = c_ref[...]
    c_act = (c * jax.nn.sigmoid(c)).astype(BF16)
    o_ref[...] = jnp.dot(c_act, w_ref[...].astype(BF16), preferred_element_type=F32) + b_ref[...]


def _modulation(c, w_ada, b_ada):
    L, D, N = w_ada.shape
    B = c.shape[0]
    rows = 8
    c_pad = jnp.zeros((rows, D), F32).at[:B].set(c)
    tn = _tile(N, 1024)
    out = _call(
        _mod_body, grid=(L, N // tn),
        in_specs=[pl.BlockSpec((rows, D), lambda l, n: (0, 0)),
                  pl.BlockSpec((None, D, tn), lambda l, n: (l, 0, n)),
                  pl.BlockSpec((None, 1, tn), lambda l, n: (l, 0, n))],
        out_specs=pl.BlockSpec((None, rows, tn), lambda l, n: (l, 0, n)),
        out_shape=jax.ShapeDtypeStruct((L, rows, N), F32),
        name="adaln_mod")(c_pad, w_ada, b_ada.reshape(L, 1, N))
    return out[:, :B, :]


def _latent_up_body(lat_ref, wqm_ref, wqr_ref, wkv_ref, cos_ref, sin_ref, q_ref, kv_ref, wkvb_ref):
    @pl.when(pl.program_id(0) == 0)
    def _():
        wkvb_ref[...] = wkv_ref[...].astype(BF16)

    cq = lat_ref[:, :Q_RANK]
    main = jnp.dot(cq, wqm_ref[...], preferred_element_type=F32)
    rot = jnp.dot(cq, wqr_ref[...], preferred_element_type=F32)
    cos, sin = cos_ref[...], sin_ref[...]
    for h in range(MLA_HEADS):
        c0 = h * QPAD
        q_ref[:, c0:c0 + LANE] = (main[:, c0:c0 + LANE] * MLA_SCALE).astype(q_ref.dtype)
        q_ref[:, c0 + LANE:c0 + QPAD] = (main[:, c0 + LANE:c0 + QPAD] * cos
                                         + rot[:, h * LANE:(h + 1) * LANE] * sin).astype(q_ref.dtype)
    kv_ref[...] = jnp.dot(lat_ref[:, Q_RANK:], wkvb_ref[...],
                          preferred_element_type=F32).astype(kv_ref.dtype)


def _latent_up(lat, wq_main, wq_rot, w_ukv, layer, cos_q, sin_q, tm):
    M = lat.shape[0]
    nq, nr, nkv = wq_main.shape[-1], wq_rot.shape[-1], w_ukv.shape[-1]
    whole = lambda k, n: pl.BlockSpec((None, k, n), lambda m: (layer, 0, 0))
    rows = lambda n: pl.BlockSpec((tm, n), lambda m: (m, 0))
    return _call(
        _latent_up_body, grid=(M // tm,),
        in_specs=[rows(Q_RANK + KV_RANK), whole(Q_RANK, nq), whole(Q_RANK, nr),
                  whole(KV_RANK, nkv), rows(LANE), rows(LANE)],
        out_specs=[rows(nq), rows(nkv)],
        out_shape=[jax.ShapeDtypeStruct((M, nq), BF16), jax.ShapeDtypeStruct((M, nkv), BF16)],
        scratch=[pltpu.VMEM((KV_RANK, nkv), BF16)],
        name="latent_up")(lat, wq_main, wq_rot, w_ukv, cos_q, sin_q)


def _prenorm_body(x_ref, g_ref, sc_ref, sh_ref, o_ref):
    o_ref[...] = _norm_mod(x_ref[...], g_ref[...], sc_ref[...], sh_ref[...]).astype(o_ref.dtype)


def _prenorm(x, gain, sc, sh, S, out_dtype):
    M, D = x.shape
    tm = _tile(S, 512)
    bpt = S // tm
    vec = pl.BlockSpec((None, 1, D), lambda m: (m // bpt, 0, 0))
    return _call(
        _prenorm_body, grid=(M // tm,),
        in_specs=[pl.BlockSpec((tm, D), lambda m: (m, 0)),
                  pl.BlockSpec((1, D), lambda m: (0, 0)), vec, vec],
        out_specs=pl.BlockSpec((tm, D), lambda m: (m, 0)),
        out_shape=jax.ShapeDtypeStruct((M, D), out_dtype),
        name="prenorm")(x, gain, sc, sh)


SUB = 256
ATT_HEADS = 2


def _qk(q, k):
    return lax.dot_general(q, k, (((1,), (1,)), ((), ())), preferred_element_type=F32)


def _sb_chains(ops, tri, states, masks):
    zs = [_qk(q, k) for q, k, v in ops]
    sps = []
    for z, mask in zip(zs, masks):
        sp = jnp.maximum(z, 0.0) + jnp.log2(1.0 + jnp.exp2(jnp.minimum(z, -z)))
        if mask is not None:
            sp = jnp.where(mask, sp, 0.0)
        sps.append(sp.astype(BF16))
    csums, carries = [], []
    for spb, (carry, acc) in zip(sps, states):
        parts = []
        for c in reversed(range(spb.shape[1] // SUB)):
            loc = jnp.dot(spb[:, c * SUB:(c + 1) * SUB], tri, preferred_element_type=F32)
            parts.insert(0, loc + carry)
            carry = carry + loc[:, 0:1]
        csums.append(parts[0] if len(parts) == 1 else jnp.concatenate(parts, axis=1))
        carries.append(carry)
    out = []
    for z, csum, mask, carry, (q, k, v), (_, acc) in zip(zs, csums, masks, carries, ops, states):
        a = jnp.exp2(z - csum)
        if mask is not None:
            a = jnp.where(mask, a, 0.0)
        out.append((carry, acc + jnp.dot(a.astype(BF16), v, preferred_element_type=F32)))
    return tuple(out)


def _sb_body(q_ref, k_ref, v_ref, o_ref, *, t):
    i = pl.program_id(2)
    halves = t // SUB
    r = lax.broadcasted_iota(jnp.int32, (SUB, SUB), 0)
    c = lax.broadcasted_iota(jnp.int32, (SUB, SUB), 1)
    tri = jnp.where(r >= c, 1.0, 0.0).astype(BF16)
    start = pl.multiple_of(i * t, t)
    chains = [(g, a) for g in range(ATT_HEADS) for a in range(halves)]

    def operands(g, a, row0, nrows):
        cols = slice(g * HEAD_DIM, (g + 1) * HEAD_DIM)
        return (q_ref[a * SUB:(a + 1) * SUB, cols], k_ref[pl.ds(row0, nrows), cols],
                v_ref[pl.ds(row0, nrows), cols])

    ops, masks, st0 = [], [], []
    for g, a in chains:
        n = (a + 1) * SUB
        rr = lax.broadcasted_iota(jnp.int32, (SUB, n), 0)
        cc = lax.broadcasted_iota(jnp.int32, (SUB, n), 1)
        ops.append(operands(g, a, start, n))
        masks.append(cc < rr + a * SUB)
        st0.append((jnp.zeros((SUB, 1), F32), jnp.zeros((SUB, HEAD_DIM), F32)))
    state = _sb_chains(ops, tri, st0, masks)

    def step(jj, st):
        row0 = pl.multiple_of((i - 1 - jj) * t, t)
        return _sb_chains([operands(g, a, row0, t) for g, a in chains], tri, st,
                          [None] * len(chains))

    state = lax.fori_loop(0, i, step, state)
    for n, (g, a) in enumerate(chains):
        o_ref[a * SUB:(a + 1) * SUB, g * HEAD_DIM:(g + 1) * HEAD_DIM] = state[n][1].astype(o_ref.dtype)


def _sb_attention(qkv, B, S):
    t = _tile(S, 2 * SUB)
    nq = S // t
    G = ATT_HEADS
    HG = SB_HEADS // G
    W = G * HEAD_DIM
    return _call(
        functools.partial(_sb_body, t=t), grid=(B, HG, nq),
        in_specs=[pl.BlockSpec((t, W), lambda b, h, i: (b * nq + i, h)),
                  pl.BlockSpec((S, W), lambda b, h, i: (b, HG + h)),
                  pl.BlockSpec((S, W), lambda b, h, i: (b, 2 * HG + h))],
        out_specs=pl.BlockSpec((t, W), lambda b, h, i: (b * nq + i, h)),
        out_shape=jax.ShapeDtypeStruct((B * S, SB_WIDTH), BF16),
        name="sb_attention")(qkv, qkv, qkv)


def _mla_chains(ops, states, masks):
    ss = []
    for (q, k, v), mask in zip(ops, masks):
        s = _qk(q, k)
        if mask is not None:
            s = jnp.where(mask, s, -jnp.inf)
        ss.append(s)
    ms = [jnp.maximum(m, jnp.max(s, axis=-1, keepdims=True)) for s, (m, l, acc) in zip(ss, states)]
    ps = [jnp.exp2(s - m_new) for s, m_new in zip(ss, ms)]
    out = []
    for p, m_new, (q, k, v), (m, l, acc) in zip(ps, ms, ops, states):
        alpha = jnp.exp2(m - m_new)
        l = alpha * l + jnp.sum(p, axis=-1, keepdims=True)
        acc = alpha * acc + jnp.dot(p.astype(BF16), v, preferred_element_type=F32)
        out.append((m_new, l, acc))
    return tuple(out)


def _mla_body(q_ref, kv_ref, kpe_ref, o_ref, kf_ref, *, t):
    i = pl.program_id(2)
    halves = t // SUB
    hw = MLA_NOPE + MLA_V

    @pl.when(i == 0)
    def _():
        for g in range(ATT_HEADS):
            kf_ref[g, :, 0:LANE] = kv_ref[:, g * hw:g * hw + MLA_NOPE]
            kf_ref[g, :, LANE:QPAD] = kpe_ref[...]

    start = pl.multiple_of(i * t, t)
    chains = [(g, a) for g in range(ATT_HEADS) for a in range(halves)]

    def operands(g, a, row0, nrows):
        return (q_ref[a * SUB:(a + 1) * SUB, g * QPAD:(g + 1) * QPAD],
                kf_ref[g, pl.ds(row0, nrows), :],
                kv_ref[pl.ds(row0, nrows), g * hw + MLA_NOPE:(g + 1) * hw])

    ops, masks, st0 = [], [], []
    for g, a in chains:
        n = (a + 1) * SUB
        rr = lax.broadcasted_iota(jnp.int32, (SUB, n), 0)
        cc = lax.broadcasted_iota(jnp.int32, (SUB, n), 1)
        ops.append(operands(g, a, start, n))
        masks.append(cc <= rr + a * SUB)
        st0.append((jnp.full((SUB, 1), -jnp.inf, F32), jnp.zeros((SUB, 1), F32),
                    jnp.zeros((SUB, MLA_V), F32)))
    state = _mla_chains(ops, st0, masks)

    def step(j, st):
        row0 = pl.multiple_of(j * t, t)
        return _mla_chains([operands(g, a, row0, t) for g, a in chains], st, [None] * len(chains))

    state = lax.fori_loop(0, i, step, state)
    for n, (g, a) in enumerate(chains):
        m, l, acc = state[n]
        o_ref[a * SUB:(a + 1) * SUB, g * MLA_V:(g + 1) * MLA_V] = (acc / l).astype(o_ref.dtype)


def _mla_attention(q_full, kv, kpe, B, S):
    t = _tile(S, 2 * SUB)
    nq = S // t
    G = ATT_HEADS
    HG = MLA_HEADS // G
    return _call(
        functools.partial(_mla_body, t=t), grid=(B, HG, nq),
        in_specs=[pl.BlockSpec((t, G * QPAD), lambda b, h, i: (b * nq + i, h)),
                  pl.BlockSpec((S, G * (MLA_NOPE + MLA_V)), lambda b, h, i: (b, h)),
                  pl.BlockSpec((S, LANE), lambda b, h, i: (b, 0))],
        out_specs=pl.BlockSpec((t, G * MLA_V), lambda b, h, i: (b * nq + i, h)),
        out_shape=jax.ShapeDtypeStruct((B * S, MLA_HEADS * MLA_V), BF16),
        scratch=[pltpu.VMEM((G, S, QPAD), BF16)],
        name="mla_attention")(q_full, kv, kpe)


def _merge_body(osb_ref, omla_ref, gate_sb_ref, gate_mla_ref, wsb_ref, wmla_ref, wo_ref, x_ref,
                g1_ref, gain_ref, sc_ref, sh_ref, xo_ref, ho_ref):
    y = (gate_sb_ref[...].astype(F32)
         * jnp.dot(osb_ref[...], wsb_ref[...], preferred_element_type=F32)
         + gate_mla_ref[...].astype(F32)
         * jnp.dot(omla_ref[...], wmla_ref[...], preferred_element_type=F32))
    out = jnp.dot(y.astype(BF16), wo_ref[...], preferred_element_type=F32)
    x_new = x_ref[...] + g1_ref[...] * out
    xo_ref[...] = x_new
    ho_ref[...] = _norm_mod(x_new, gain_ref[...], sc_ref[...], sh_ref[...]).astype(ho_ref.dtype)


def _merge(o_sb, o_mla, gates, w_sb, w_mla, w_o, layer, x, g1, gain, sc, sh, S, h_dtype):
    M, D = x.shape
    tm = _tile(S, 256)
    bpt = S // tm
    nd = D // D
    vec = pl.BlockSpec((None, 1, D), lambda m: (m // bpt, 0, 0))
    res = lambda k: pl.BlockSpec((None, k, D), lambda m: (layer, 0, 0),
                                 pipeline_mode=pl.Buffered(1))
    return _call(
        _merge_body, grid=(M // tm,),
        in_specs=[pl.BlockSpec((tm, SB_WIDTH), lambda m: (m, 0)),
                  pl.BlockSpec((tm, MLA_HEADS * MLA_V), lambda m: (m, 0)),
                  pl.BlockSpec((tm, D), lambda m: (m, 0)),
                  pl.BlockSpec((tm, D), lambda m: (m, nd)),
                  res(SB_WIDTH), res(MLA_HEADS * MLA_V), res(D),
                  pl.BlockSpec((tm, D), lambda m: (m, 0)),
                  vec, pl.BlockSpec((1, D), lambda m: (0, 0)), vec, vec],
        out_specs=[pl.BlockSpec((tm, D), lambda m: (m, 0)),
                   pl.BlockSpec((tm, D), lambda m: (m, 0))],
        out_shape=[jax.ShapeDtypeStruct((M, D), F32), jax.ShapeDtypeStruct((M, D), h_dtype)],
        name="merge_out")(o_sb, o_mla, gates, gates, w_sb, w_mla, w_o, x, g1, gain, sc, sh)


def _resnorm_body(f_ref, x_ref, g_ref, gain_ref, sc_ref, sh_ref, xo_ref, ho_ref):
    x_new = x_ref[...] + g_ref[...] * f_ref[...]
    xo_ref[...] = x_new
    ho_ref[...] = _norm_mod(x_new, gain_ref[...], sc_ref[...], sh_ref[...]).astype(ho_ref.dtype)


def _resnorm(f, x, g, gain, sc, sh, S, h_dtype):
    M, D = x.shape
    tm = _tile(S, 512)
    bpt = S // tm
    vec = pl.BlockSpec((None, 1, D), lambda m: (m // bpt, 0, 0))
    rows = pl.BlockSpec((tm, D), lambda m: (m, 0))
    return _call(
        _resnorm_body, grid=(M // tm,),
        in_specs=[rows, rows, vec, pl.BlockSpec((1, D), lambda m: (0, 0)), vec, vec],
        out_specs=[rows, rows],
        out_shape=[jax.ShapeDtypeStruct((M, D), F32), jax.ShapeDtypeStruct((M, D), h_dtype)],
        name="resnorm")(f, x, g, gain, sc, sh)


def _split_bf16(v):
    hi = v.astype(BF16)
    return hi, (v - hi.astype(F32)).astype(BF16)


def _router_body(h_ref, w_ref, idx_ref, wt_ref):
    hh, hl = _split_bf16(h_ref[...])
    wh, wl = _split_bf16(w_ref[...])
    dot = functools.partial(jnp.dot, preferred_element_type=F32)
    logits = dot(hh, wh) + (dot(hl, wh) + dot(hh, wl))
    lane = lax.broadcasted_iota(jnp.int32, logits.shape, 1).astype(F32)
    logits = jnp.where(lane < N_EXPERTS, logits, -jnp.inf)
    m1 = jnp.max(logits, axis=-1, keepdims=True)
    i1 = jnp.min(jnp.where(logits == m1, lane, float(LANE)), axis=-1, keepdims=True)
    rest = jnp.where(lane == i1, -jnp.inf, logits)
    m2 = jnp.max(rest, axis=-1, keepdims=True)
    i2 = jnp.min(jnp.where(rest == m2, lane, float(LANE)), axis=-1, keepdims=True)
    e = jnp.exp(m2 - m1)
    w1 = 1.0 / (1.0 + e)
    w2 = e * w1
    idx_ref[...] = jnp.where(lane == 0.0, i1, jnp.where(lane == 1.0, i2, 0.0)).astype(jnp.int32)
    wt_ref[...] = jnp.where(lane == 0.0, w1, jnp.where(lane == 1.0, w2, 0.0))


def _router(h, w_router_pad, j):
    M, D = h.shape
    tm = _tile(M, 512)
    return _call(
        _router_body, grid=(M // tm,),
        in_specs=[pl.BlockSpec((tm, D), lambda m: (m, 0)),
                  pl.BlockSpec((None, D, LANE), lambda m: (j, 0, 0))],
        out_specs=[pl.BlockSpec((tm, LANE), lambda m: (m, 0)),
                   pl.BlockSpec((tm, LANE), lambda m: (m, 0))],
        out_shape=[jax.ShapeDtypeStruct((M, LANE), jnp.int32),
                   jax.ShapeDtypeStruct((M, LANE), F32)],
        name="router")(h, w_router_pad)


MOE_SUB = 256
ROW_UNROLL = 8


def _gather_body(src_ref, h_hbm, o_ref, buf_ref, sem, *, tm):
    t = pl.program_id(0)

    def issue(tile):
        slot = tile % 2

        def blk(b, _):
            for u in range(ROW_UNROLL):
                r = b * ROW_UNROLL + u
                pltpu.make_async_copy(h_hbm.at[pl.ds(src_ref[tile * tm + r], 1)],
                                      buf_ref.at[slot, pl.ds(r, 1)],
                                      sem.at[slot]).start(priority=u % 2)
            return 0

        lax.fori_loop(0, tm // ROW_UNROLL, blk, 0)

    @pl.when(t == 0)
    def _():
        issue(t)

    @pl.when(t + 1 < pl.num_programs(0))
    def _():
        issue(t + 1)

    slot = t % 2
    pltpu.make_async_copy(h_hbm.at[pl.ds(0, tm)], buf_ref.at[slot], sem.at[slot]).wait()
    o_ref[...] = buf_ref[slot].astype(o_ref.dtype)


def _gather_rows(src_tok, h, n_tiles, tm):
    M, D = h.shape
    return _call(
        functools.partial(_gather_body, tm=tm), grid=(n_tiles,), nsp=1,
        in_specs=[pl.BlockSpec(memory_space=pl.ANY)],
        out_specs=pl.BlockSpec((tm, D), lambda t, src: (t, 0)),
        out_shape=jax.ShapeDtypeStruct((n_tiles * tm, D), BF16),
        scratch=[pltpu.VMEM((2, tm, D), F32), pltpu.SemaphoreType.DMA((2,))],
        name="moe_gather")(src_tok, h)


def _grouped_body(te_ref, a_ref, *refs, n_w, layer, epilogue):
    w_hbm = refs[:n_w]
    n_rest = len(refs) - 3 * n_w - 1
    rest = refs[n_w:n_w + n_rest]
    stage = refs[n_w + n_rest:2 * n_w + n_rest]
    wb_refs = refs[2 * n_w + n_rest:3 * n_w + n_rest]
    sem = refs[-1]
    c, t = pl.program_id(0), pl.program_id(1)
    nc, nt = pl.num_programs(0), pl.num_programs(1)
    expert = te_ref[t]
    n_valid = te_ref[nt + t]
    nxt = te_ref[2 * nt + t]
    changed = jnp.logical_or(t == 0, expert != te_ref[jnp.maximum(t - 1, 0)])
    tn = stage[0].shape[1]

    def block_copy(i, chunk, e):
        col0 = pl.multiple_of(chunk * tn, tn)
        return pltpu.make_async_copy(w_hbm[i].at[layer, e, :, pl.ds(col0, tn)], stage[i],
                                     sem.at[i])

    @pl.when(changed)
    def _():
        @pl.when(jnp.logical_and(c == 0, t == 0))
        def _():
            for i in range(n_w):
                block_copy(i, c, expert).start()

        for i in range(n_w):
            block_copy(i, c, expert).wait()
            wb_refs[i][...] = stage[i][...].astype(BF16)

        @pl.when(nxt >= 0)
        def _():
            for i in range(n_w):
                block_copy(i, c, nxt).start()

        @pl.when(jnp.logical_and(nxt < 0, c + 1 < nc))
        def _():
            for i in range(n_w):
                block_copy(i, c + 1, te_ref[0]).start()

    for s in range(a_ref.shape[0] // MOE_SUB):
        rows = slice(s * MOE_SUB, (s + 1) * MOE_SUB)

        @pl.when(s * MOE_SUB < n_valid)
        def _():
            a = a_ref[rows, :]
            accs = [jnp.dot(a, wb_ref[...], preferred_element_type=F32) for wb_ref in wb_refs]
            epilogue(accs, rest[:n_rest], rows)

        @pl.when(s * MOE_SUB >= n_valid)
        def _():
            for o_ref in rest[:n_rest]:
                o_ref[rows, :] = jnp.zeros((MOE_SUB, o_ref.shape[1]), o_ref.dtype)


def _up_epilogue(accs, refs, rows=slice(None)):
    g, u = accs
    refs[0][rows, :] = (g * jax.nn.sigmoid(g) * u).astype(refs[0].dtype)


def _grouped_up(tile_e, xs, w_gate, w_up, j, tm, tf):
    NP, D = xs.shape
    F = w_gate.shape[-1]
    hbm = pl.BlockSpec(memory_space=pl.ANY)
    return _call(
        functools.partial(_grouped_body, n_w=2, layer=j, epilogue=_up_epilogue),
        grid=(F // tf, NP // tm), nsp=1,
        in_specs=[pl.BlockSpec((tm, D), lambda f, t, te: (t, 0)), hbm, hbm],
        out_specs=pl.BlockSpec((tm, tf), lambda f, t, te: (t, f)),
        out_shape=jax.ShapeDtypeStruct((NP, F), BF16),
        scratch=[pltpu.VMEM((D, tf), F32), pltpu.VMEM((D, tf), F32),
                 pltpu.VMEM((D, tf), BF16), pltpu.VMEM((D, tf), BF16),
                 pltpu.SemaphoreType.DMA((2,))],
        name="moe_up")(tile_e, xs, w_gate, w_up)


def _down_epilogue(accs, refs, rows):
    refs[0][rows, :] = accs[0]


def _grouped_down(tile_e, hs, w_down, j, tm, tn):
    NP, F = hs.shape
    D = w_down.shape[-1]
    return _call(
        functools.partial(_grouped_body, n_w=1, layer=j, epilogue=_down_epilogue),
        grid=(D // tn, NP // tm), nsp=1,
        in_specs=[pl.BlockSpec((tm, F), lambda n, t, te: (t, 0)),
                  pl.BlockSpec(memory_space=pl.ANY)],
        out_specs=pl.BlockSpec((tm, tn), lambda n, t, te: (t, n)),
        out_shape=jax.ShapeDtypeStruct((NP, D), F32),
        scratch=[pltpu.VMEM((F, tn), F32), pltpu.VMEM((F, tn), BF16),
                 pltpu.SemaphoreType.DMA((1,))],
        name="moe_down")(tile_e, hs, w_down)


def _combine_body(pos_ref, y_hbm, wt_ref, x_ref, g2_ref, gain_ref, sc_ref, sh_ref, xo_ref, ho_ref,
                  buf_ref, sem, *, tm):
    t = pl.program_id(0)

    def issue(tile):
        slot = tile % 2

        def blk(b, _):
            for u in range(ROW_UNROLL):
                r = b * ROW_UNROLL + u
                for k in range(TOP_K):
                    pltpu.make_async_copy(
                        y_hbm.at[pl.ds(pos_ref[(tile * tm + r) * TOP_K + k], 1)],
                        buf_ref.at[slot, k, pl.ds(r, 1)], sem.at[slot]).start(priority=k % 2)
            return 0

        lax.fori_loop(0, tm // ROW_UNROLL, blk, 0)

    @pl.when(t == 0)
    def _():
        issue(t)

    @pl.when(t + 1 < pl.num_programs(0))
    def _():
        issue(t + 1)

    slot = t % 2
    for k in range(TOP_K):
        pltpu.make_async_copy(y_hbm.at[pl.ds(0, tm)], buf_ref.at[slot, k], sem.at[slot]).wait()
    wt = wt_ref[...]
    f = buf_ref[slot, 0] * wt[:, 0:1]
    for k in range(1, TOP_K):
        f = f + buf_ref[slot, k] * wt[:, k:k + 1]
    x_new = x_ref[...] + g2_ref[...] * f
    xo_ref[...] = x_new
    ho_ref[...] = _norm_mod(x_new, gain_ref[...], sc_ref[...], sh_ref[...]).astype(ho_ref.dtype)


def _combine(pos, y_sorted, wts, x, g2, gain, sc, sh, S, h_dtype):
    M, D = x.shape
    tm = _tile(S, 256)
    bpt = S // tm
    vec = pl.BlockSpec((None, 1, D), lambda m, p: (m // bpt, 0, 0))
    return _call(
        functools.partial(_combine_body, tm=tm), grid=(M // tm,), nsp=1,
        in_specs=[pl.BlockSpec(memory_space=pl.ANY),
                  pl.BlockSpec((tm, LANE), lambda m, p: (m, 0)),
                  pl.BlockSpec((tm, D), lambda m, p: (m, 0)),
                  vec, pl.BlockSpec((1, D), lambda m, p: (0, 0)), vec, vec],
        out_specs=[pl.BlockSpec((tm, D), lambda m, p: (m, 0)),
                   pl.BlockSpec((tm, D), lambda m, p: (m, 0))],
        out_shape=[jax.ShapeDtypeStruct((M, D), F32), jax.ShapeDtypeStruct((M, D), h_dtype)],
        scratch=[pltpu.VMEM((2, TOP_K, tm, D), F32), pltpu.SemaphoreType.DMA((2,))],
        name="moe_combine")(pos, y_sorted, wts, x, g2, gain, sc, sh)


def _dispatch_plan(top_i, tm):
    M = top_i.shape[0]
    E = N_EXPERTS
    n_ent = M * TOP_K
    e_flat = top_i.reshape(-1)
    onehot = (e_flat[:, None] == jnp.arange(E, dtype=jnp.int32)[None, :]).astype(jnp.int32)
    csum = jnp.cumsum(onehot, axis=0)
    rank = jnp.sum(onehot * csum, axis=1) - 1
    counts = csum[-1]
    pcounts = ((counts + tm - 1) // tm) * tm
    pend = jnp.cumsum(pcounts)
    pstart = pend - pcounts
    dest = (pstart[e_flat] + rank).astype(jnp.int32)
    n_tiles = n_ent // tm + E
    n_rows = n_tiles * tm
    src_tok = jnp.zeros((n_rows,), jnp.int32).at[dest].set(
        jnp.arange(n_ent, dtype=jnp.int32) // TOP_K)
    tile_start = jnp.arange(n_tiles, dtype=jnp.int32) * tm
    tile_e = jnp.sum((tile_start[:, None] >= pend[None, :]).astype(jnp.int32), axis=1)
    tile_e = jnp.minimum(tile_e, E - 1).astype(jnp.int32)
    n_valid = jnp.clip(pstart[tile_e] + counts[tile_e] - tile_start, 0, tm).astype(jnp.int32)
    n_active = pend[-1] // tm
    tile_e = jnp.where(tile_start < pend[-1], tile_e, tile_e[n_active - 1])
    tidx = jnp.arange(n_tiles, dtype=jnp.int32)
    later_other = (tidx[None, :] > tidx[:, None]) & (tile_e[None, :] != tile_e[:, None])
    nxt = jnp.where(jnp.any(later_other, axis=1), tile_e[jnp.argmax(later_other, axis=1)], -1)
    return src_tok, dest, jnp.concatenate([tile_e, n_valid, nxt]).astype(jnp.int32), n_tiles


def _rot_cols(w):
    half = MLA_ROPE // 2
    return jnp.concatenate([-w[..., half:], w[..., :half]], axis=-1)


def _rope_tables(positions):
    inv_freq = 1.0 / (ROPE_THETA ** (jnp.arange(0, MLA_ROPE, 2, dtype=F32) / MLA_ROPE))
    ang = positions.astype(F32)[..., None] * inv_freq
    cos, sin = jnp.cos(ang), jnp.sin(ang)
    M = positions.size
    cos2 = jnp.concatenate([cos, cos], axis=-1).reshape(M, MLA_ROPE)
    sin2 = jnp.concatenate([sin, sin], axis=-1).reshape(M, MLA_ROPE)
    zeros = jnp.zeros((M, LANE - MLA_ROPE), F32)
    cos_k = jnp.concatenate([cos2, zeros], axis=-1)
    sin_k = jnp.concatenate([sin2, zeros], axis=-1)
    return cos_k, sin_k, cos_k * MLA_SCALE, sin_k * MLA_SCALE


def kernel(x, c, positions, w_ada, b_ada, norm_mix_g, norm_ffn_g, w_in, q_norm_g, kv_norm_g, w_uq, w_ukv, w_sb_up, w_mla_up, w_o, w_ffn_gate, w_ffn_up, w_ffn_down, w_router, w_exp_gate, w_exp_up, w_exp_down, final_norm_g):
    B, S, D = x.shape
    L = w_ada.shape[0]
    M = B * S
    x = x.reshape(M, D)

    mod = _modulation(c, w_ada, b_ada)
    mod = mod.reshape(L, B, N_MOD, 1, D)
    sh1, sc1, g1, sh2, sc2, g2 = [mod[:, :, k] for k in range(N_MOD)]
    cos_k, sin_k, cos_q, sin_q = _rope_tables(positions)

    pe0 = 3 * SB_WIDTH + Q_RANK + KV_RANK
    w_pe = w_in[:, :, pe0:pe0 + MLA_ROPE]
    zpad = jnp.zeros(w_pe.shape[:2] + (LANE - MLA_ROPE,), F32)
    w_pe2 = jnp.concatenate([w_pe, zpad, _rot_cols(w_pe), zpad], axis=-1).astype(BF16)
    w_gates = w_in[:, :, pe0 + MLA_ROPE:].astype(BF16)
    c_gain = jnp.concatenate([q_norm_g, kv_norm_g], axis=-1)[:, None, :]

    wq = w_uq.reshape(L, Q_RANK, MLA_HEADS, MLA_QK)
    wq_n, wq_r = wq[..., :MLA_NOPE], wq[..., MLA_NOPE:]
    zq = jnp.zeros(wq_r.shape, F32)
    wq_main = jnp.concatenate([wq_n, wq_r, zq], axis=-1).reshape(
        L, Q_RANK, MLA_HEADS * QPAD).astype(BF16)
    wq_rot = jnp.concatenate([_rot_cols(wq_r), zq], axis=-1).reshape(
        L, Q_RANK, MLA_HEADS * LANE).astype(BF16)

    w_sb_b, w_mla_b, w_o_b = (w.astype(BF16) for w in (w_sb_up, w_mla_up, w_o))
    w_router_pad = jnp.concatenate(
        [w_router, jnp.zeros(w_router.shape[:2] + (LANE - N_EXPERTS,), F32)], axis=-1)

    tm = _tile(S, 1024)
    bpt = S // tm
    row = lambda width: pl.BlockSpec((tm, width), lambda n, m: (m, 0))
    sb_scale = HEAD_DIM ** -0.5 * LOG2E

    def h_dtype_for(layer_is_moe):
        return F32 if layer_is_moe else BF16

    h = _prenorm(x, norm_mix_g[0][None, :], sc1[0], sh1[0], S, BF16)

    for l in range(L):
        tn = _tile(min(SB_WIDTH, 2 * D), 1024)
        nq_blk = SB_WIDTH // tn

        def qkv_epi(accs, n, extra, outs):
            s = jnp.where(n < nq_blk, sb_scale, 1.0)
            outs[0][...] = (accs[0] * s).astype(BF16)

        (qkv,) = _matmul(h, [w_in], layer=l, col_blk0=0, n_cols=3 * SB_WIDTH, tn=tn, tm=tm,
                         epilogue=qkv_epi,
                         out_shape=[jax.ShapeDtypeStruct((M, 3 * SB_WIDTH), BF16)],
                         out_specs=[pl.BlockSpec((tm, tn), lambda n, m: (m, n))], name="in_qkv")

        def lat_epi(accs, n, extra, outs):
            a = accs[0]
            ms = jnp.mean(a * a, axis=-1, keepdims=True)
            outs[0][...] = (a * lax.rsqrt(ms + NORM_EPS) * extra[0][...]).astype(BF16)

        (lat,) = _matmul(h, [w_in], layer=l, col_blk0=3 * SB_WIDTH // Q_RANK,
                         n_cols=Q_RANK + KV_RANK, tn=Q_RANK, tm=tm, epilogue=lat_epi,
                         extras=[c_gain],
                         extra_specs=[pl.BlockSpec((None, 1, Q_RANK), lambda n, m: (l, 0, n))],
                         out_shape=[jax.ShapeDtypeStruct((M, Q_RANK + KV_RANK), BF16)],
                         out_specs=[pl.BlockSpec((tm, Q_RANK), lambda n, m: (m, n))],
                         name="in_latent")

        def pe_epi(accs, n, extra, outs):
            a = accs[0]
            outs[0][...] = (a[:, :LANE] * extra[0][...] + a[:, LANE:] * extra[1][...]).astype(BF16)

        (kpe,) = _matmul(h, [w_pe2], layer=l, col_blk0=0, n_cols=2 * LANE, tn=2 * LANE, tm=tm,
                         epilogue=pe_epi, extras=[cos_k, sin_k], extra_specs=[row(LANE), row(LANE)],
                         out_shape=[jax.ShapeDtypeStruct((M, LANE), BF16)],
                         out_specs=[pl.BlockSpec((tm, LANE), lambda n, m: (m, 0))], name="in_kpe")

        def gate_epi(accs, n, extra, outs):
            outs[0][...] = jax.nn.sigmoid(accs[0]).astype(BF16)

        (gates,) = _matmul(h, [w_gates], layer=l, col_blk0=0, n_cols=2 * D, tn=tn, tm=tm,
                           epilogue=gate_epi,
                           out_shape=[jax.ShapeDtypeStruct((M, 2 * D), BF16)],
                           out_specs=[pl.BlockSpec((tm, tn), lambda n, m: (m, n))],
                           name="in_gates")

        q_full, kv = _latent_up(lat, wq_main, wq_rot, w_ukv, l, cos_q, sin_q, tm)

        o_sb = _sb_attention(qkv, B, S)
        o_mla = _mla_attention(q_full, kv, kpe, B, S)

        moe = (l % 2 == 1)
        x, h = _merge(o_sb, o_mla, gates, w_sb_b, w_mla_b, w_o_b, l, x, g1[l],
                      norm_ffn_g[l][None, :], sc2[l], sh2[l], S, h_dtype_for(moe))

        last = (l == L - 1)
        if last:
            n_gain = final_norm_g[None, :]
            n_sc = jnp.zeros_like(sc1[0])
            n_sh = jnp.zeros_like(sh1[0])
            n_dtype = F32
        else:
            n_gain, n_sc, n_sh, n_dtype = norm_mix_g[l + 1][None, :], sc1[l + 1], sh1[l + 1], BF16
        j = l // 2
        if not moe:
            F = w_ffn_gate.shape[-1]
            tf = _tile(F, 512)

            def up_epi(accs, n, extra, outs):
                _up_epilogue(accs, outs)

            (hmid,) = _matmul(h, [w_ffn_gate, w_ffn_up], layer=j, col_blk0=0, n_cols=F, tn=tf,
                              tm=tm, epilogue=up_epi,
                              out_shape=[jax.ShapeDtypeStruct((M, F), BF16)],
                              out_specs=[pl.BlockSpec((tm, tf), lambda n, m: (m, n))],
                              name="ffn_up")
            def down_epi(accs, n, extra, outs):
                outs[0][...] = accs[0]

            tnd = _tile(D, 512)
            (f,) = _matmul(hmid, [w_ffn_down], layer=j, col_blk0=0, n_cols=D, tn=tnd,
                           tm=_tile(S, 512), epilogue=down_epi,
                           out_shape=[jax.ShapeDtypeStruct((M, D), F32)],
                           out_specs=[pl.BlockSpec((_tile(S, 512), tnd), lambda n, m: (m, n))],
                           name="ffn_down")
            x, h = _resnorm(f, x, g2[l], n_gain, n_sc, n_sh, S, n_dtype)
        else:
            F = w_exp_gate.shape[-1]
            tme = max(_tile(M, 2 * MOE_SUB), MOE_SUB)
            idx, wts = _router(h, w_router_pad, j)
            src_tok, dest, tile_e, n_tiles = _dispatch_plan(idx[:, :TOP_K], tme)
            xs = _gather_rows(src_tok, h, n_tiles, tme)
            tfe = F // 4 if F % (4 * LANE) == 0 else _tile(F, 512)
            hs = _grouped_up(tile_e, xs, w_exp_gate, w_exp_up, j, tme, tfe)
            ys = _grouped_down(tile_e, hs, w_exp_down, j, tme, _tile(D, 512))
            x, h = _combine(dest, ys, wts, x, g2[l], n_gain, n_sc, n_sh, S, n_dtype)

    return h.reshape(B, S, D)
```

```python
import functools

import jax
import jax.numpy as jnp
from jax import lax
from jax.experimental import pallas as pl
from jax.experimental.pallas import tpu as pltpu

BF16 = jnp.bfloat16
F32 = jnp.float32

SB_HEADS = 8
HEAD_DIM = 128
SB_WIDTH = SB_HEADS * HEAD_DIM
MLA_HEADS = 8
MLA_NOPE = 128
MLA_ROPE = 64
MLA_V = 128
MLA_QK = MLA_NOPE + MLA_ROPE
Q_RANK = 512
KV_RANK = 512
ROPE_THETA = 10000.0
N_EXPERTS = 8
TOP_K = 2
N_MOD = 6
NORM_EPS = 1e-6

LOG2E = 1.4426950408889634
MLA_SCALE = MLA_QK ** -0.5 * LOG2E

LANE = 128
GATE_SHIFT = MLA_ROPE
QPAD = 2 * LANE
VMEM_LIMIT = 56 << 20


def _call(body, *, grid, in_specs, out_specs, out_shape, scratch=(), nsp=0, name, flags=None):
    return pl.pallas_call(
        body,
        grid_spec=pltpu.PrefetchScalarGridSpec(
            num_scalar_prefetch=nsp, grid=grid, in_specs=in_specs,
            out_specs=out_specs, scratch_shapes=scratch),
        out_shape=out_shape,
        compiler_params=pltpu.CompilerParams(
            dimension_semantics=("arbitrary",) * len(grid),
            vmem_limit_bytes=VMEM_LIMIT, flags=flags),
        name=name)


def _tile(n, pref):
    t = min(n, pref)
    while n % t:
        t //= 2
    return t


def _norm_mod(x, gain, sc, sh):
    ms = jnp.mean(x * x, axis=-1, keepdims=True)
    y = x * lax.rsqrt(ms + NORM_EPS)
    return (y * gain) * (1.0 + sc) + sh


def _mm_body(*refs, n_w, n_extra, n_out, epilogue):
    a_ref = refs[0]
    w_refs = refs[1:1 + n_w]
    extra = refs[1 + n_w:1 + n_w + n_extra]
    outs = refs[1 + n_w + n_extra:1 + n_w + n_extra + n_out]
    wb_refs = refs[1 + n_w + n_extra + n_out:]

    @pl.when(pl.program_id(1) == 0)
    def _():
        for w_ref, wb_ref in zip(w_refs, wb_refs):
            wb_ref[...] = w_ref[...].astype(BF16)

    a = a_ref[...]
    accs = [jnp.dot(a, wb_ref[...], preferred_element_type=F32) for wb_ref in wb_refs]
    epilogue(accs, pl.program_id(0), extra, outs)


def _matmul(a, ws, *, layer, col_blk0, n_cols, tn, tm, epilogue, extras=(), extra_specs=(),
            out_shape, out_specs, name, a_blk=0):
    M = a.shape[0]
    K = ws[0].shape[1]
    grid = (n_cols // tn, M // tm)
    in_specs = [pl.BlockSpec((tm, K), lambda n, m: (m, a_blk))]
    for _ in ws:
        in_specs.append(pl.BlockSpec((None, K, tn), lambda n, m: (layer, 0, n + col_blk0)))
    in_specs += list(extra_specs)
    body = functools.partial(_mm_body, n_w=len(ws), n_extra=len(extras), n_out=len(out_shape),
                             epilogue=epilogue)
    return _call(body, grid=grid, in_specs=in_specs, out_specs=out_specs, out_shape=out_shape,
                 scratch=[pltpu.VMEM((K, tn), BF16) for _ in ws], name=name)(a, *ws, *extras)


def _mod_body(c_ref, w_ref, b_ref, o_ref):
    c = c_ref[...]
    c_act = (c * jax.nn.sigmoid(c)).astype(BF16)
    o_ref[...] = jnp.dot(c_act, w_ref[...].astype(BF16), preferred_element_type=F32) + b_ref[...]


def _modulation(c, w_ada, b_ada):
    L, D, N = w_ada.shape
    B = c.shape[0]
    rows = 8
    c_pad = jnp.zeros((rows, D), F32).at[:B].set(c)
    tn = _tile(N, 1024)
    out = _call(
        _mod_body, grid=(L, N // tn),
        in_specs=[pl.BlockSpec((rows, D), lambda l, n: (0, 0)),
                  pl.BlockSpec((None, D, tn), lambda l, n: (l, 0, n)),
                  pl.BlockSpec((None, 1, tn), lambda l, n: (l, 0, n))],
        out_specs=pl.BlockSpec((None, rows, tn), lambda l, n: (l, 0, n)),
        out_shape=jax.ShapeDtypeStruct((L, rows, N), F32),
        name="adaln_mod")(c_pad, w_ada, b_ada.reshape(L, 1, N))
    return out[:, :B, :]


def _latent_up_body(lat_ref, wqm_ref, wqr_ref, wkv_ref, cos_ref, sin_ref, q_ref, kv_ref, wkvb_ref):
    @pl.when(pl.program_id(0) == 0)
    def _():
        wkvb_ref[...] = wkv_ref[...].astype(BF16)

    cq = lat_ref[:, :Q_RANK]
    main = jnp.dot(cq, wqm_ref[...], preferred_element_type=F32)
    rot = jnp.dot(cq, wqr_ref[...], preferred_element_type=F32)
    cos, sin = cos_ref[...], sin_ref[...]
    for h in range(MLA_HEADS):
        c0 = h * QPAD
        q_ref[:, c0:c0 + LANE] = (main[:, c0:c0 + LANE] * MLA_SCALE).astype(q_ref.dtype)
        q_ref[:, c0 + LANE:c0 + QPAD] = (main[:, c0 + LANE:c0 + QPAD] * cos
                                         + rot[:, h * LANE:(h + 1) * LANE] * sin).astype(q_ref.dtype)
    kv_ref[...] = jnp.dot(lat_ref[:, Q_RANK:], wkvb_ref[...],
                          preferred_element_type=F32).astype(kv_ref.dtype)


def _latent_up(lat, wq_main, wq_rot, w_ukv, layer, cos_q, sin_q, tm):
    M = lat.shape[0]
    nq, nr, nkv = wq_main.shape[-1], wq_rot.shape[-1], w_ukv.shape[-1]
    whole = lambda k, n: pl.BlockSpec((None, k, n), lambda m: (layer, 0, 0))
    rows = lambda n: pl.BlockSpec((tm, n), lambda m: (m, 0))
    return _call(
        _latent_up_body, grid=(M // tm,),
        in_specs=[rows(Q_RANK + KV_RANK), whole(Q_RANK, nq), whole(Q_RANK, nr),
                  whole(KV_RANK, nkv), rows(LANE), rows(LANE)],
        out_specs=[rows(nq), rows(nkv)],
        out_shape=[jax.ShapeDtypeStruct((M, nq), BF16), jax.ShapeDtypeStruct((M, nkv), BF16)],
        scratch=[pltpu.VMEM((KV_RANK, nkv), BF16)],
        name="latent_up")(lat, wq_main, wq_rot, w_ukv, cos_q, sin_q)


def _prenorm_body(x_ref, g_ref, sc_ref, sh_ref, o_ref):
    o_ref[...] = _norm_mod(x_ref[...], g_ref[...], sc_ref[...], sh_ref[...]).astype(o_ref.dtype)


def _prenorm(x, gain, sc, sh, S, out_dtype):
    M, D = x.shape
    tm = _tile(S, 512)
    bpt = S // tm
    vec = pl.BlockSpec((None, 1, D), lambda m: (m // bpt, 0, 0))
    return _call(
        _prenorm_body, grid=(M // tm,),
        in_specs=[pl.BlockSpec((tm, D), lambda m: (m, 0)),
                  pl.BlockSpec((1, D), lambda m: (0, 0)), vec, vec],
        out_specs=pl.BlockSpec((tm, D), lambda m: (m, 0)),
        out_shape=jax.ShapeDtypeStruct((M, D), out_dtype),
        name="prenorm")(x, gain, sc, sh)


SUB = 256
ATT_HEADS = 2


def _qk(q, k):
    return lax.dot_general(q, k, (((1,), (1,)), ((), ())), preferred_element_type=F32)


def _sb_chains(ops, tri, states, masks):
    zs = [_qk(q, k) for q, k, v in ops]
    sps = []
    for z, mask in zip(zs, masks):
        sp = jnp.maximum(z, 0.0) + jnp.log2(1.0 + jnp.exp2(jnp.minimum(z, -z)))
        if mask is not None:
            sp = jnp.where(mask, sp, 0.0)
        sps.append(sp.astype(BF16))
    csums, carries = [], []
    for spb, (carry, acc) in zip(sps, states):
        parts = []
        for c in reversed(range(spb.shape[1] // SUB)):
            loc = jnp.dot(spb[:, c * SUB:(c + 1) * SUB], tri, preferred_element_type=F32)
            parts.insert(0, loc + carry)
            carry = carry + loc[:, 0:1]
        csums.append(parts[0] if len(parts) == 1 else jnp.concatenate(parts, axis=1))
        carries.append(carry)
    out = []
    for z, csum, mask, carry, (q, k, v), (_, acc) in zip(zs, csums, masks, carries, ops, states):
        a = jnp.exp2(z - csum)
        if mask is not None:
            a = jnp.where(mask, a, 0.0)
        out.append((carry, acc + jnp.dot(a.astype(BF16), v, preferred_element_type=F32)))
    return tuple(out)


def _sb_body(q_ref, k_ref, v_ref, o_ref, *, t):
    i = pl.program_id(2)
    halves = t // SUB
    r = lax.broadcasted_iota(jnp.int32, (SUB, SUB), 0)
    c = lax.broadcasted_iota(jnp.int32, (SUB, SUB), 1)
    tri = jnp.where(r >= c, 1.0, 0.0).astype(BF16)
    start = pl.multiple_of(i * t, t)
    chains = [(g, a) for g in range(ATT_HEADS) for a in range(halves)]

    def operands(g, a, row0, nrows):
        cols = slice(g * HEAD_DIM, (g + 1) * HEAD_DIM)
        return (q_ref[a * SUB:(a + 1) * SUB, cols], k_ref[pl.ds(row0, nrows), cols],
                v_ref[pl.ds(row0, nrows), cols])

    ops, masks, st0 = [], [], []
    for g, a in chains:
        n = (a + 1) * SUB
        rr = lax.broadcasted_iota(jnp.int32, (SUB, n), 0)
        cc = lax.broadcasted_iota(jnp.int32, (SUB, n), 1)
        ops.append(operands(g, a, start, n))
        masks.append(cc < rr + a * SUB)
        st0.append((jnp.zeros((SUB, 1), F32), jnp.zeros((SUB, HEAD_DIM), F32)))
    state = _sb_chains(ops, tri, st0, masks)

    def step(jj, st):
        row0 = pl.multiple_of((i - 1 - jj) * t, t)
        return _sb_chains([operands(g, a, row0, t) for g, a in chains], tri, st,
                          [None] * len(chains))

    state = lax.fori_loop(0, i, step, state)
    for n, (g, a) in enumerate(chains):
        o_ref[a * SUB:(a + 1) * SUB, g * HEAD_DIM:(g + 1) * HEAD_DIM] = state[n][1].astype(o_ref.dtype)


def _sb_attention(qkv, B, S):
    t = _tile(S, 2 * SUB)
    nq = S // t
    G = ATT_HEADS
    HG = SB_HEADS // G
    W = G * HEAD_DIM
    return _call(
        functools.partial(_sb_body, t=t), grid=(B, HG, nq),
        in_specs=[pl.BlockSpec((t, W), lambda b, h, i: (b * nq + i, h)),
                  pl.BlockSpec((S, W), lambda b, h, i: (b, HG + h)),
                  pl.BlockSpec((S, W), lambda b, h, i: (b, 2 * HG + h))],
        out_specs=pl.BlockSpec((t, W), lambda b, h, i: (b * nq + i, h)),
        out_shape=jax.ShapeDtypeStruct((B * S, SB_WIDTH), BF16),
        name="sb_attention")(qkv, qkv, qkv)


def _mla_chains(ops, states, masks):
    ss = []
    for (q, k, v), mask in zip(ops, masks):
        s = _qk(q, k)
        if mask is not None:
            s = jnp.where(mask, s, -jnp.inf)
        ss.append(s)
    ms = [jnp.maximum(m, jnp.max(s, axis=-1, keepdims=True)) for s, (m, l, acc) in zip(ss, states)]
    ps = [jnp.exp2(s - m_new) for s, m_new in zip(ss, ms)]
    out = []
    for p, m_new, (q, k, v), (m, l, acc) in zip(ps, ms, ops, states):
        alpha = jnp.exp2(m - m_new)
        l = alpha * l + jnp.sum(p, axis=-1, keepdims=True)
        acc = alpha * acc + jnp.dot(p.astype(BF16), v, preferred_element_type=F32)
        out.append((m_new, l, acc))
    return tuple(out)


def _mla_body(q_ref, kv_ref, kpe_ref, o_ref, kf_ref, *, t):
    i = pl.program_id(2)
    halves = t // SUB
    hw = MLA_NOPE + MLA_V

    @pl.when(i == 0)
    def _():
        for g in range(ATT_HEADS):
            kf_ref[g, :, 0:LANE] = kv_ref[:, g * hw:g * hw + MLA_NOPE]
            kf_ref[g, :, LANE:QPAD] = kpe_ref[...]

    start = pl.multiple_of(i * t, t)
    chains = [(g, a) for g in range(ATT_HEADS) for a in range(halves)]

    def operands(g, a, row0, nrows):
        return (q_ref[a * SUB:(a + 1) * SUB, g * QPAD:(g + 1) * QPAD],
                kf_ref[g, pl.ds(row0, nrows), :],
                kv_ref[pl.ds(row0, nrows), g * hw + MLA_NOPE:(g + 1) * hw])

    ops, masks, st0 = [], [], []
    for g, a in chains:
        n = (a + 1) * SUB
        rr = lax.broadcasted_iota(jnp.int32, (SUB, n), 0)
        cc = lax.broadcasted_iota(jnp.int32, (SUB, n), 1)
        ops.append(operands(g, a, start, n))
        masks.append(cc <= rr + a * SUB)
        st0.append((jnp.full((SUB, 1), -jnp.inf, F32), jnp.zeros((SUB, 1), F32),
                    jnp.zeros((SUB, MLA_V), F32)))
    state = _mla_chains(ops, st0, masks)

    def step(j, st):
        row0 = pl.multiple_of(j * t, t)
        return _mla_chains([operands(g, a, row0, t) for g, a in chains], st, [None] * len(chains))

    state = lax.fori_loop(0, i, step, state)
    for n, (g, a) in enumerate(chains):
        m, l, acc = state[n]
        o_ref[a * SUB:(a + 1) * SUB, g * MLA_V:(g + 1) * MLA_V] = (acc / l).astype(o_ref.dtype)


def _mla_attention(q_full, kv, kpe, B, S):
    t = _tile(S, 2 * SUB)
    nq = S // t
    G = ATT_HEADS
    HG = MLA_HEADS // G
    return _call(
        functools.partial(_mla_body, t=t), grid=(B, HG, nq),
        in_specs=[pl.BlockSpec((t, G * QPAD), lambda b, h, i: (b * nq + i, h)),
                  pl.BlockSpec((S, G * (MLA_NOPE + MLA_V)), lambda b, h, i: (b, h)),
                  pl.BlockSpec((S, LANE), lambda b, h, i: (b, 0))],
        out_specs=pl.BlockSpec((t, G * MLA_V), lambda b, h, i: (b * nq + i, h)),
        out_shape=jax.ShapeDtypeStruct((B * S, MLA_HEADS * MLA_V), BF16),
        scratch=[pltpu.VMEM((G, S, QPAD), BF16)],
        name="mla_attention")(q_full, kv, kpe)


def _merge_body(osb_ref, omla_ref, gsh_ref, gtail_ref, wsb_ref, wmla_ref, wo_ref, x_ref,
                g1_ref, gain_ref, sc_ref, sh_ref, xo_ref, ho_ref):
    D = x_ref.shape[1]
    gate_sb = gsh_ref[:, :D + LANE].astype(F32)
    gate_mla = jnp.concatenate([gsh_ref[:, D:], gtail_ref[...]], axis=1).astype(F32)
    y = (gate_sb * jnp.dot(osb_ref[...], wsb_ref[...], preferred_element_type=F32)
         + gate_mla * jnp.dot(omla_ref[...], wmla_ref[...], preferred_element_type=F32))
    out = jnp.dot(y.astype(BF16), wo_ref[...], preferred_element_type=F32)
    x_new = x_ref[...] + g1_ref[...] * out
    xo_ref[...] = x_new
    ho_ref[...] = _norm_mod(x_new, gain_ref[...], sc_ref[...], sh_ref[...]).astype(ho_ref.dtype)


def _merge(o_sb, o_mla, gsh, gtail, w_sb, w_mla, w_o, layer, x, g1, gain, sc, sh, S, h_dtype):
    M, D = x.shape
    tm = _tile(S, 256)
    bpt = S // tm
    vec = pl.BlockSpec((None, 1, D), lambda m: (m // bpt, 0, 0))
    res = lambda k, n: pl.BlockSpec((None, k, n), lambda m: (layer, 0, 0),
                                    pipeline_mode=pl.Buffered(1))
    return _call(
        _merge_body, grid=(M // tm,),
        in_specs=[pl.BlockSpec((tm, SB_WIDTH), lambda m: (m, 0)),
                  pl.BlockSpec((tm, MLA_HEADS * MLA_V), lambda m: (m, 0)),
                  pl.BlockSpec((tm, 2 * D), lambda m: (m, 0)),
                  pl.BlockSpec((tm, LANE), lambda m: (m, 0)),
                  res(SB_WIDTH, D + LANE), res(MLA_HEADS * MLA_V, D + LANE), res(D + LANE, D),
                  pl.BlockSpec((tm, D), lambda m: (m, 0)),
                  vec, pl.BlockSpec((1, D), lambda m: (0, 0)), vec, vec],
        out_specs=[pl.BlockSpec((tm, D), lambda m: (m, 0)),
                   pl.BlockSpec((tm, D), lambda m: (m, 0))],
        out_shape=[jax.ShapeDtypeStruct((M, D), F32), jax.ShapeDtypeStruct((M, D), h_dtype)],
        name="merge_out")(o_sb, o_mla, gsh, gtail, w_sb, w_mla, w_o, x, g1, gain, sc, sh)


def _resnorm_body(f_ref, x_ref, g_ref, gain_ref, sc_ref, sh_ref, xo_ref, ho_ref):
    x_new = x_ref[...] + g_ref[...] * f_ref[...]
    xo_ref[...] = x_new
    ho_ref[...] = _norm_mod(x_new, gain_ref[...], sc_ref[...], sh_ref[...]).astype(ho_ref.dtype)


def _resnorm(f, x, g, gain, sc, sh, S, h_dtype):
    M, D = x.shape
    tm = _tile(S, 512)
    bpt = S // tm
    vec = pl.BlockSpec((None, 1, D), lambda m: (m // bpt, 0, 0))
    rows = pl.BlockSpec((tm, D), lambda m: (m, 0))
    return _call(
        _resnorm_body, grid=(M // tm,),
        in_specs=[rows, rows, vec, pl.BlockSpec((1, D), lambda m: (0, 0)), vec, vec],
        out_specs=[rows, rows],
        out_shape=[jax.ShapeDtypeStruct((M, D), F32), jax.ShapeDtypeStruct((M, D), h_dtype)],
        name="resnorm")(f, x, g, gain, sc, sh)


def _split_bf16(v):
    hi = v.astype(BF16)
    return hi, (v - hi.astype(F32)).astype(BF16)


def _router_body(h_ref, w_ref, idx_ref, wt_ref):
    hh, hl = _split_bf16(h_ref[...])
    wh, wl = _split_bf16(w_ref[...])
    dot = functools.partial(jnp.dot, preferred_element_type=F32)
    logits = dot(hh, wh) + (dot(hl, wh) + dot(hh, wl))
    lane = lax.broadcasted_iota(jnp.int32, logits.shape, 1).astype(F32)
    logits = jnp.where(lane < N_EXPERTS, logits, -jnp.inf)
    m1 = jnp.max(logits, axis=-1, keepdims=True)
    i1 = jnp.min(jnp.where(logits == m1, lane, float(LANE)), axis=-1, keepdims=True)
    rest = jnp.where(lane == i1, -jnp.inf, logits)
    m2 = jnp.max(rest, axis=-1, keepdims=True)
    i2 = jnp.min(jnp.where(rest == m2, lane, float(LANE)), axis=-1, keepdims=True)
    e = jnp.exp(m2 - m1)
    w1 = 1.0 / (1.0 + e)
    w2 = e * w1
    idx_ref[...] = jnp.where(lane == 0.0, i1, jnp.where(lane == 1.0, i2, 0.0)).astype(jnp.int32)
    wt_ref[...] = jnp.where(lane == 0.0, w1, jnp.where(lane == 1.0, w2, 0.0))


def _router(h, w_router_pad, j):
    M, D = h.shape
    tm = _tile(M, 512)
    return _call(
        _router_body, grid=(M // tm,),
        in_specs=[pl.BlockSpec((tm, D), lambda m: (m, 0)),
                  pl.BlockSpec((None, D, LANE), lambda m: (j, 0, 0))],
        out_specs=[pl.BlockSpec((tm, LANE), lambda m: (m, 0)),
                   pl.BlockSpec((tm, LANE), lambda m: (m, 0))],
        out_shape=[jax.ShapeDtypeStruct((M, LANE), jnp.int32),
                   jax.ShapeDtypeStruct((M, LANE), F32)],
        name="router")(h, w_router_pad)


MOE_SUB = 256
ROW_UNROLL = 8


def _gather_body(src_ref, h_hbm, o_ref, buf_ref, sem, *, tm):
    t = pl.program_id(0)

    def issue(tile):
        slot = tile % 2

        def blk(b, _):
            for u in range(ROW_UNROLL):
                r = b * ROW_UNROLL + u
                pltpu.make_async_copy(h_hbm.at[pl.ds(src_ref[tile * tm + r], 1)],
                                      buf_ref.at[slot, pl.ds(r, 1)],
                                      sem.at[slot]).start(priority=u % 2)
            return 0

        lax.fori_loop(0, tm // ROW_UNROLL, blk, 0)

    @pl.when(t == 0)
    def _():
        issue(t)

    @pl.when(t + 1 < pl.num_programs(0))
    def _():
        issue(t + 1)

    slot = t % 2
    pltpu.make_async_copy(h_hbm.at[pl.ds(0, tm)], buf_ref.at[slot], sem.at[slot]).wait()
    o_ref[...] = buf_ref[slot].astype(o_ref.dtype)


def _gather_rows(src_tok, h, n_tiles, tm):
    M, D = h.shape
    return _call(
        functools.partial(_gather_body, tm=tm), grid=(n_tiles,), nsp=1,
        in_specs=[pl.BlockSpec(memory_space=pl.ANY)],
        out_specs=pl.BlockSpec((tm, D), lambda t, src: (t, 0)),
        out_shape=jax.ShapeDtypeStruct((n_tiles * tm, D), BF16),
        scratch=[pltpu.VMEM((2, tm, D), F32), pltpu.SemaphoreType.DMA((2,))],
        name="moe_gather")(src_tok, h)


def _grouped_body(te_ref, a_ref, *refs, n_w, layer, epilogue):
    w_hbm = refs[:n_w]
    n_rest = len(refs) - 3 * n_w - 1
    rest = refs[n_w:n_w + n_rest]
    stage = refs[n_w + n_rest:2 * n_w + n_rest]
    wb_refs = refs[2 * n_w + n_rest:3 * n_w + n_rest]
    sem = refs[-1]
    c, t = pl.program_id(0), pl.program_id(1)
    nc, nt = pl.num_programs(0), pl.num_programs(1)
    expert = te_ref[t]
    n_valid = te_ref[nt + t]
    nxt = te_ref[2 * nt + t]
    changed = jnp.logical_or(t == 0, expert != te_ref[jnp.maximum(t - 1, 0)])
    tn = stage[0].shape[1]

    def block_copy(i, chunk, e):
        col0 = pl.multiple_of(chunk * tn, tn)
        return pltpu.make_async_copy(w_hbm[i].at[layer, e, :, pl.ds(col0, tn)], stage[i],
                                     sem.at[i])

    @pl.when(changed)
    def _():
        @pl.when(jnp.logical_and(c == 0, t == 0))
        def _():
            for i in range(n_w):
                block_copy(i, c, expert).start()

        for i in range(n_w):
            block_copy(i, c, expert).wait()
            wb_refs[i][...] = stage[i][...].astype(BF16)

        @pl.when(nxt >= 0)
        def _():
            for i in range(n_w):
                block_copy(i, c, nxt).start()

        @pl.when(jnp.logical_and(nxt < 0, c + 1 < nc))
        def _():
            for i in range(n_w):
                block_copy(i, c + 1, te_ref[0]).start()

    for s in range(a_ref.shape[0] // MOE_SUB):
        rows = slice(s * MOE_SUB, (s + 1) * MOE_SUB)

        @pl.when(s * MOE_SUB < n_valid)
        def _():
            a = a_ref[rows, :]
            accs = [jnp.dot(a, wb_ref[...], preferred_element_type=F32) for wb_ref in wb_refs]
            epilogue(accs, rest[:n_rest], rows)

        @pl.when(s * MOE_SUB >= n_valid)
        def _():
            for o_ref in rest[:n_rest]:
                o_ref[rows, :] = jnp.zeros((MOE_SUB, o_ref.shape[1]), o_ref.dtype)


def _up_epilogue(accs, refs, rows=slice(None)):
    g, u = accs
    refs[0][rows, :] = (g * jax.nn.sigmoid(g) * u).astype(refs[0].dtype)


def _grouped_up(tile_e, xs, w_gate, w_up, j, tm, tf):
    NP, D = xs.shape
    F = w_gate.shape[-1]
    hbm = pl.BlockSpec(memory_space=pl.ANY)
    return _call(
        functools.partial(_grouped_body, n_w=2, layer=j, epilogue=_up_epilogue),
        grid=(F // tf, NP // tm), nsp=1,
        in_specs=[pl.BlockSpec((tm, D), lambda f, t, te: (t, 0)), hbm, hbm],
        out_specs=pl.BlockSpec((tm, tf), lambda f, t, te: (t, f)),
        out_shape=jax.ShapeDtypeStruct((NP, F), BF16),
        scratch=[pltpu.VMEM((D, tf), F32), pltpu.VMEM((D, tf), F32),
                 pltpu.VMEM((D, tf), BF16), pltpu.VMEM((D, tf), BF16),
                 pltpu.SemaphoreType.DMA((2,))],
        name="moe_up")(tile_e, xs, w_gate, w_up)


def _down_epilogue(accs, refs, rows):
    refs[0][rows, :] = accs[0]


def _grouped_down(tile_e, hs, w_down, j, tm, tn):
    NP, F = hs.shape
    D = w_down.shape[-1]
    return _call(
        functools.partial(_grouped_body, n_w=1, layer=j, epilogue=_down_epilogue),
        grid=(D // tn, NP // tm), nsp=1,
        in_specs=[pl.BlockSpec((tm, F), lambda n, t, te: (t, 0)),
                  pl.BlockSpec(memory_space=pl.ANY)],
        out_specs=pl.BlockSpec((tm, tn), lambda n, t, te: (t, n)),
        out_shape=jax.ShapeDtypeStruct((NP, D), F32),
        scratch=[pltpu.VMEM((F, tn), F32), pltpu.VMEM((F, tn), BF16),
                 pltpu.SemaphoreType.DMA((1,))],
        name="moe_down")(tile_e, hs, w_down)


def _combine_body(pos_ref, y_hbm, wt_ref, x_ref, g2_ref, gain_ref, sc_ref, sh_ref, xo_ref, ho_ref,
                  buf_ref, sem, *, tm):
    t = pl.program_id(0)

    def issue(tile):
        slot = tile % 2

        def blk(b, _):
            for u in range(ROW_UNROLL):
                r = b * ROW_UNROLL + u
                for k in range(TOP_K):
                    pltpu.make_async_copy(
                        y_hbm.at[pl.ds(pos_ref[(tile * tm + r) * TOP_K + k], 1)],
                        buf_ref.at[slot, k, pl.ds(r, 1)], sem.at[slot]).start(priority=k % 2)
            return 0

        lax.fori_loop(0, tm // ROW_UNROLL, blk, 0)

    @pl.when(t == 0)
    def _():
        issue(t)

    @pl.when(t + 1 < pl.num_programs(0))
    def _():
        issue(t + 1)

    slot = t % 2
    for k in range(TOP_K):
        pltpu.make_async_copy(y_hbm.at[pl.ds(0, tm)], buf_ref.at[slot, k], sem.at[slot]).wait()
    wt = wt_ref[...]
    f = buf_ref[slot, 0] * wt[:, 0:1]
    for k in range(1, TOP_K):
        f = f + buf_ref[slot, k] * wt[:, k:k + 1]
    x_new = x_ref[...] + g2_ref[...] * f
    xo_ref[...] = x_new
    ho_ref[...] = _norm_mod(x_new, gain_ref[...], sc_ref[...], sh_ref[...]).astype(ho_ref.dtype)


def _combine(pos, y_sorted, wts, x, g2, gain, sc, sh, S, h_dtype):
    M, D = x.shape
    tm = _tile(S, 256)
    bpt = S // tm
    vec = pl.BlockSpec((None, 1, D), lambda m, p: (m // bpt, 0, 0))
    return _call(
        functools.partial(_combine_body, tm=tm), grid=(M // tm,), nsp=1,
        in_specs=[pl.BlockSpec(memory_space=pl.ANY),
                  pl.BlockSpec((tm, LANE), lambda m, p: (m, 0)),
                  pl.BlockSpec((tm, D), lambda m, p: (m, 0)),
                  vec, pl.BlockSpec((1, D), lambda m, p: (0, 0)), vec, vec],
        out_specs=[pl.BlockSpec((tm, D), lambda m, p: (m, 0)),
                   pl.BlockSpec((tm, D), lambda m, p: (m, 0))],
        out_shape=[jax.ShapeDtypeStruct((M, D), F32), jax.ShapeDtypeStruct((M, D), h_dtype)],
        scratch=[pltpu.VMEM((2, TOP_K, tm, D), F32), pltpu.SemaphoreType.DMA((2,))],
        name="moe_combine")(pos, y_sorted, wts, x, g2, gain, sc, sh)


def _dispatch_plan(top_i, tm):
    M = top_i.shape[0]
    E = N_EXPERTS
    n_ent = M * TOP_K
    e_flat = top_i.reshape(-1)
    onehot = (e_flat[:, None] == jnp.arange(E, dtype=jnp.int32)[None, :]).astype(jnp.int32)
    csum = jnp.cumsum(onehot, axis=0)
    rank = jnp.sum(onehot * csum, axis=1) - 1
    counts = csum[-1]
    pcounts = ((counts + tm - 1) // tm) * tm
    pend = jnp.cumsum(pcounts)
    pstart = pend - pcounts
    dest = (pstart[e_flat] + rank).astype(jnp.int32)
    n_tiles = n_ent // tm + E
    n_rows = n_tiles * tm
    src_tok = jnp.zeros((n_rows,), jnp.int32).at[dest].set(
        jnp.arange(n_ent, dtype=jnp.int32) // TOP_K)
    tile_start = jnp.arange(n_tiles, dtype=jnp.int32) * tm
    tile_e = jnp.sum((tile_start[:, None] >= pend[None, :]).astype(jnp.int32), axis=1)
    tile_e = jnp.minimum(tile_e, E - 1).astype(jnp.int32)
    n_valid = jnp.clip(pstart[tile_e] + counts[tile_e] - tile_start, 0, tm).astype(jnp.int32)
    n_active = pend[-1] // tm
    tile_e = jnp.where(tile_start < pend[-1], tile_e, tile_e[n_active - 1])
    tidx = jnp.arange(n_tiles, dtype=jnp.int32)
    later_other = (tidx[None, :] > tidx[:, None]) & (tile_e[None, :] != tile_e[:, None])
    nxt = jnp.where(jnp.any(later_other, axis=1), tile_e[jnp.argmax(later_other, axis=1)], -1)
    return src_tok, dest, jnp.concatenate([tile_e, n_valid, nxt]).astype(jnp.int32), n_tiles


def _rot_cols(w):
    half = MLA_ROPE // 2
    return jnp.concatenate([-w[..., half:], w[..., :half]], axis=-1)


def _rope_tables(positions):
    inv_freq = 1.0 / (ROPE_THETA ** (jnp.arange(0, MLA_ROPE, 2, dtype=F32) / MLA_ROPE))
    ang = positions.astype(F32)[..., None] * inv_freq
    cos, sin = jnp.cos(ang), jnp.sin(ang)
    M = positions.size
    cos2 = jnp.concatenate([cos, cos], axis=-1).reshape(M, MLA_ROPE)
    sin2 = jnp.concatenate([sin, sin], axis=-1).reshape(M, MLA_ROPE)
    zeros = jnp.zeros((M, LANE - MLA_ROPE), F32)
    cos_k = jnp.concatenate([cos2, zeros], axis=-1)
    sin_k = jnp.concatenate([sin2, zeros], axis=-1)
    return cos_k, sin_k, cos_k * MLA_SCALE, sin_k * MLA_SCALE


def kernel(x, c, positions, w_ada, b_ada, norm_mix_g, norm_ffn_g, w_in, q_norm_g, kv_norm_g, w_uq, w_ukv, w_sb_up, w_mla_up, w_o, w_ffn_gate, w_ffn_up, w_ffn_down, w_router, w_exp_gate, w_exp_up, w_exp_down, final_norm_g):
    B, S, D = x.shape
    L = w_ada.shape[0]
    M = B * S
    x = x.reshape(M, D)

    mod = _modulation(c, w_ada, b_ada)
    mod = mod.reshape(L, B, N_MOD, 1, D)
    sh1, sc1, g1, sh2, sc2, g2 = [mod[:, :, k] for k in range(N_MOD)]
    cos_k, sin_k, cos_q, sin_q = _rope_tables(positions)

    pe0 = 3 * SB_WIDTH + Q_RANK + KV_RANK
    w_pe = w_in[:, :, pe0:pe0 + MLA_ROPE]
    w_tail = w_in[:, :, pe0 + 2 * D:]
    zpad = jnp.zeros(w_pe.shape[:2] + (LANE - MLA_ROPE,), F32)
    w_pe2 = jnp.concatenate([w_pe, zpad, _rot_cols(w_pe), zpad, w_tail, zpad],
                            axis=-1).astype(BF16)
    c_gain = jnp.concatenate([q_norm_g, kv_norm_g], axis=-1)[:, None, :]

    wq = w_uq.reshape(L, Q_RANK, MLA_HEADS, MLA_QK)
    wq_n, wq_r = wq[..., :MLA_NOPE], wq[..., MLA_NOPE:]
    zq = jnp.zeros(wq_r.shape, F32)
    wq_main = jnp.concatenate([wq_n, wq_r, zq], axis=-1).reshape(
        L, Q_RANK, MLA_HEADS * QPAD).astype(BF16)
    wq_rot = jnp.concatenate([_rot_cols(wq_r), zq], axis=-1).reshape(
        L, Q_RANK, MLA_HEADS * LANE).astype(BF16)

    lane_pad = ((0, 0), (0, 0), (GATE_SHIFT, LANE - GATE_SHIFT))
    w_sb_b = jnp.pad(w_sb_up, lane_pad).astype(BF16)
    w_mla_b = jnp.pad(w_mla_up, lane_pad).astype(BF16)
    w_o_b = jnp.pad(w_o, ((0, 0), (GATE_SHIFT, LANE - GATE_SHIFT), (0, 0))).astype(BF16)
    w_router_pad = jnp.concatenate(
        [w_router, jnp.zeros(w_router.shape[:2] + (LANE - N_EXPERTS,), F32)], axis=-1)

    tm = _tile(S, 1024)
    bpt = S // tm
    row = lambda width: pl.BlockSpec((tm, width), lambda n, m: (m, 0))
    sb_scale = HEAD_DIM ** -0.5 * LOG2E

    def h_dtype_for(layer_is_moe):
        return F32 if layer_is_moe else BF16

    h = _prenorm(x, norm_mix_g[0][None, :], sc1[0], sh1[0], S, BF16)

    for l in range(L):
        tn = _tile(min(SB_WIDTH, 2 * D), 1024)
        nq_blk = SB_WIDTH // tn

        def qkv_epi(accs, n, extra, outs):
            s = jnp.where(n < nq_blk, sb_scale, 1.0)
            outs[0][...] = (accs[0] * s).astype(BF16)

        (qkv,) = _matmul(h, [w_in], layer=l, col_blk0=0, n_cols=3 * SB_WIDTH, tn=tn, tm=tm,
                         epilogue=qkv_epi,
                         out_shape=[jax.ShapeDtypeStruct((M, 3 * SB_WIDTH), BF16)],
                         out_specs=[pl.BlockSpec((tm, tn), lambda n, m: (m, n))], name="in_qkv")

        def lat_epi(accs, n, extra, outs):
            a = accs[0]
            ms = jnp.mean(a * a, axis=-1, keepdims=True)
            outs[0][...] = (a * lax.rsqrt(ms + NORM_EPS) * extra[0][...]).astype(BF16)

        (lat,) = _matmul(h, [w_in], layer=l, col_blk0=3 * SB_WIDTH // Q_RANK,
                         n_cols=Q_RANK + KV_RANK, tn=Q_RANK, tm=tm, epilogue=lat_epi,
                         extras=[c_gain],
                         extra_specs=[pl.BlockSpec((None, 1, Q_RANK), lambda n, m: (l, 0, n))],
                         out_shape=[jax.ShapeDtypeStruct((M, Q_RANK + KV_RANK), BF16)],
                         out_specs=[pl.BlockSpec((tm, Q_RANK), lambda n, m: (m, n))],
                         name="in_latent")

        def pe_epi(accs, n, extra, outs):
            a = accs[0]
            outs[0][...] = (a[:, :LANE] * extra[0][...]
                            + a[:, LANE:2 * LANE] * extra[1][...]).astype(BF16)
            outs[1][...] = jax.nn.sigmoid(a[:, 2 * LANE:]).astype(BF16)

        lane_blk = pl.BlockSpec((tm, LANE), lambda n, m: (m, 0))
        kpe, gtail = _matmul(h, [w_pe2], layer=l, col_blk0=0, n_cols=3 * LANE, tn=3 * LANE, tm=tm,
                             epilogue=pe_epi, extras=[cos_k, sin_k],
                             extra_specs=[row(LANE), row(LANE)],
                             out_shape=[jax.ShapeDtypeStruct((M, LANE), BF16)] * 2,
                             out_specs=[lane_blk, lane_blk], name="in_kpe")

        def gate_epi(accs, n, extra, outs):
            outs[0][...] = jax.nn.sigmoid(accs[0]).astype(BF16)

        (gsh,) = _matmul(h, [w_in], layer=l, col_blk0=pe0 // tn, n_cols=2 * D, tn=tn, tm=tm,
                         epilogue=gate_epi,
                         out_shape=[jax.ShapeDtypeStruct((M, 2 * D), BF16)],
                         out_specs=[pl.BlockSpec((tm, tn), lambda n, m: (m, n))],
                         name="in_gates")

        q_full, kv = _latent_up(lat, wq_main, wq_rot, w_ukv, l, cos_q, sin_q, tm)

        o_sb = _sb_attention(qkv, B, S)
        o_mla = _mla_attention(q_full, kv, kpe, B, S)

        moe = (l % 2 == 1)
        x, h = _merge(o_sb, o_mla, gsh, gtail, w_sb_b, w_mla_b, w_o_b, l, x, g1[l],
                      norm_ffn_g[l][None, :], sc2[l], sh2[l], S, h_dtype_for(moe))

        last = (l == L - 1)
        if last:
            n_gain = final_norm_g[None, :]
            n_sc = jnp.zeros_like(sc1[0])
            n_sh = jnp.zeros_like(sh1[0])
            n_dtype = F32
        else:
            n_gain, n_sc, n_sh, n_dtype = norm_mix_g[l + 1][None, :], sc1[l + 1], sh1[l + 1], BF16
        j = l // 2
        if not moe:
            F = w_ffn_gate.shape[-1]
            tf = _tile(F, 512)

            def up_epi(accs, n, extra, outs):
                _up_epilogue(accs, outs)

            (hmid,) = _matmul(h, [w_ffn_gate, w_ffn_up], layer=j, col_blk0=0, n_cols=F, tn=tf,
                              tm=tm, epilogue=up_epi,
                              out_shape=[jax.ShapeDtypeStruct((M, F), BF16)],
                              out_specs=[pl.BlockSpec((tm, tf), lambda n, m: (m, n))],
                              name="ffn_up")
            def down_epi(accs, n, extra, outs):
                outs[0][...] = accs[0]

            tnd = _tile(D, 512)
            (f,) = _matmul(hmid, [w_ffn_down], layer=j, col_blk0=0, n_cols=D, tn=tnd,
                           tm=_tile(S, 512), epilogue=down_epi,
                           out_shape=[jax.ShapeDtypeStruct((M, D), F32)],
                           out_specs=[pl.BlockSpec((_tile(S, 512), tnd), lambda n, m: (m, n))],
                           name="ffn_down")
            x, h = _resnorm(f, x, g2[l], n_gain, n_sc, n_sh, S, n_dtype)
        else:
            F = w_exp_gate.shape[-1]
            tme = max(_tile(M, 2 * MOE_SUB), MOE_SUB)
            idx, wts = _router(h, w_router_pad, j)
            src_tok, dest, tile_e, n_tiles = _dispatch_plan(idx[:, :TOP_K], tme)
            xs = _gather_rows(src_tok, h, n_tiles, tme)
            tfe = F // 4 if F % (4 * LANE) == 0 else _tile(F, 512)
            hs = _grouped_up(tile_e, xs, w_exp_gate, w_exp_up, j, tme, tfe)
            ys = _grouped_down(tile_e, hs, w_exp_down, j, tme, _tile(D, 512))
            x, h = _combine(dest, ys, wts, x, g2[l], n_gain, n_sc, n_sh, S, n_dtype)

    return h.reshape(B, S, D)
```

```python
import functools

import jax
import jax.numpy as jnp
from jax import lax
from jax.experimental import pallas as pl
from jax.experimental.pallas import tpu as pltpu

BF16 = jnp.bfloat16
F32 = jnp.float32

SB_HEADS = 8
HEAD_DIM = 128
SB_WIDTH = SB_HEADS * HEAD_DIM
MLA_HEADS = 8
MLA_NOPE = 128
MLA_ROPE = 64
MLA_V = 128
MLA_QK = MLA_NOPE + MLA_ROPE
Q_RANK = 512
KV_RANK = 512
ROPE_THETA = 10000.0
N_EXPERTS = 8
TOP_K = 2
N_MOD = 6
NORM_EPS = 1e-6

LOG2E = 1.4426950408889634
MLA_SCALE = MLA_QK ** -0.5 * LOG2E

LANE = 128
GATE_SHIFT = MLA_ROPE
QPAD = 2 * LANE
VMEM_LIMIT = 56 << 20


def _call(body, *, grid, in_specs, out_specs, out_shape, scratch=(), nsp=0, name, flags=None):
    return pl.pallas_call(
        body,
        grid_spec=pltpu.PrefetchScalarGridSpec(
            num_scalar_prefetch=nsp, grid=grid, in_specs=in_specs,
            out_specs=out_specs, scratch_shapes=scratch),
        out_shape=out_shape,
        compiler_params=pltpu.CompilerParams(
            dimension_semantics=("arbitrary",) * len(grid),
            vmem_limit_bytes=VMEM_LIMIT, flags=flags),
        name=name)


def _tile(n, pref):
    t = min(n, pref)
    while n % t:
        t //= 2
    return t


def _norm_mod(x, gain, sc, sh):
    ms = jnp.mean(x * x, axis=-1, keepdims=True)
    y = x * lax.rsqrt(ms + NORM_EPS)
    return (y * gain) * (1.0 + sc) + sh


def _mm_body(*refs, n_w, n_extra, n_out, epilogue, transposed):
    a_ref = refs[0]
    w_refs = refs[1:1 + n_w]
    extra = refs[1 + n_w:1 + n_w + n_extra]
    outs = refs[1 + n_w + n_extra:1 + n_w + n_extra + n_out]
    wb_refs = refs[1 + n_w + n_extra + n_out:]

    @pl.when(pl.program_id(1) == 0)
    def _():
        for w_ref, wb_ref in zip(w_refs, wb_refs):
            wb_ref[...] = w_ref[...].astype(BF16)

    a = a_ref[...]
    dims = (((1,), (1,)), ((), ())) if transposed else (((1,), (0,)), ((), ()))
    accs = [lax.dot_general(a, wb_ref[...], dims, preferred_element_type=F32)
            for wb_ref in wb_refs]
    epilogue(accs, pl.program_id(0), extra, outs)


def _matmul(a, ws, *, layer, col_blk0, n_cols, tn, tm, epilogue, extras=(), extra_specs=(),
            out_shape, out_specs, name, a_blk=0, transposed=False):
    M = a.shape[0]
    K = ws[0].shape[2 if transposed else 1]
    grid = (n_cols // tn, M // tm)
    in_specs = [pl.BlockSpec((tm, K), lambda n, m: (m, a_blk))]
    for _ in ws:
        if transposed:
            in_specs.append(pl.BlockSpec((None, tn, K), lambda n, m: (layer, n + col_blk0, 0)))
        else:
            in_specs.append(pl.BlockSpec((None, K, tn), lambda n, m: (layer, 0, n + col_blk0)))
    in_specs += list(extra_specs)
    body = functools.partial(_mm_body, n_w=len(ws), n_extra=len(extras), n_out=len(out_shape),
                             epilogue=epilogue, transposed=transposed)
    wb_shape = (tn, K) if transposed else (K, tn)
    return _call(body, grid=grid, in_specs=in_specs, out_specs=out_specs, out_shape=out_shape,
                 scratch=[pltpu.VMEM(wb_shape, BF16) for _ in ws], name=name)(a, *ws, *extras)


def _mod_body(c_ref, w_ref, b_ref, o_ref):
    c = c_ref[...]
    c_act = (c * jax.nn.sigmoid(c)).astype(BF16)
    o_ref[...] = jnp.dot(c_act, w_ref[...].astype(BF16), preferred_element_type=F32) + b_ref[...]


def _modulation(c, w_ada, b_ada):
    L, D, N = w_ada.shape
    B = c.shape[0]
    rows = 8
    c_pad = jnp.zeros((rows, D), F32).at[:B].set(c)
    tn = _tile(N, 1024)
    out = _call(
        _mod_body, grid=(L, N // tn),
        in_specs=[pl.BlockSpec((rows, D), lambda l, n: (0, 0)),
                  pl.BlockSpec((None, D, tn), lambda l, n: (l, 0, n)),
                  pl.BlockSpec((None, 1, tn), lambda l, n: (l, 0, n))],
        out_specs=pl.BlockSpec((None, rows, tn), lambda l, n: (l, 0, n)),
        out_shape=jax.ShapeDtypeStruct((L, rows, N), F32),
        name="adaln_mod")(c_pad, w_ada, b_ada.reshape(L, 1, N))
    return out[:, :B, :]


def _latent_up_body(lat_ref, wqm_ref, wqr_ref, wkv_ref, cos_ref, sin_ref, q_ref, kv_ref, wkvb_ref):
    @pl.when(pl.program_id(0) == 0)
    def _():
        wkvb_ref[...] = wkv_ref[...].astype(BF16)

    cq = lat_ref[:, :Q_RANK]
    main = jnp.dot(cq, wqm_ref[...], preferred_element_type=F32)
    rot = jnp.dot(cq, wqr_ref[...], preferred_element_type=F32)
    cos, sin = cos_ref[...], sin_ref[...]
    for h in range(MLA_HEADS):
        c0 = h * QPAD
        q_ref[:, c0:c0 + LANE] = (main[:, c0:c0 + LANE] * MLA_SCALE).astype(q_ref.dtype)
        q_ref[:, c0 + LANE:c0 + QPAD] = (main[:, c0 + LANE:c0 + QPAD] * cos
                                         + rot[:, h * LANE:(h + 1) * LANE] * sin).astype(q_ref.dtype)
    kv_ref[...] = jnp.dot(lat_ref[:, Q_RANK:], wkvb_ref[...],
                          preferred_element_type=F32).astype(kv_ref.dtype)


def _latent_up(lat, wq_main, wq_rot, w_ukv, layer, cos_q, sin_q, tm):
    M = lat.shape[0]
    nq, nr, nkv = wq_main.shape[-1], wq_rot.shape[-1], w_ukv.shape[-1]
    whole = lambda k, n: pl.BlockSpec((None, k, n), lambda m: (layer, 0, 0))
    rows = lambda n: pl.BlockSpec((tm, n), lambda m: (m, 0))
    return _call(
        _latent_up_body, grid=(M // tm,),
        in_specs=[rows(Q_RANK + KV_RANK), whole(Q_RANK, nq), whole(Q_RANK, nr),
                  whole(KV_RANK, nkv), rows(LANE), rows(LANE)],
        out_specs=[rows(nq), rows(nkv)],
        out_shape=[jax.ShapeDtypeStruct((M, nq), BF16), jax.ShapeDtypeStruct((M, nkv), BF16)],
        scratch=[pltpu.VMEM((KV_RANK, nkv), BF16)],
        name="latent_up")(lat, wq_main, wq_rot, w_ukv, cos_q, sin_q)


def _prenorm_body(x_ref, g_ref, sc_ref, sh_ref, o_ref):
    o_ref[...] = _norm_mod(x_ref[...], g_ref[...], sc_ref[...], sh_ref[...]).astype(o_ref.dtype)


def _prenorm(x, gain, sc, sh, S, out_dtype):
    M, D = x.shape
    tm = _tile(S, 512)
    bpt = S // tm
    vec = pl.BlockSpec((None, 1, D), lambda m: (m // bpt, 0, 0))
    return _call(
        _prenorm_body, grid=(M // tm,),
        in_specs=[pl.BlockSpec((tm, D), lambda m: (m, 0)),
                  pl.BlockSpec((1, D), lambda m: (0, 0)), vec, vec],
        out_specs=pl.BlockSpec((tm, D), lambda m: (m, 0)),
        out_shape=jax.ShapeDtypeStruct((M, D), out_dtype),
        name="prenorm")(x, gain, sc, sh)


SUB = 256
ATT_HEADS = 2


def _qk(q, k):
    return lax.dot_general(q, k, (((1,), (1,)), ((), ())), preferred_element_type=F32)


def _sb_chains(ops, tri, states, masks):
    zs = [_qk(q, k) for q, k, v in ops]
    sps = []
    for z, mask in zip(zs, masks):
        sp = jnp.maximum(z, 0.0) + jnp.log2(1.0 + jnp.exp2(jnp.minimum(z, -z)))
        if mask is not None:
            sp = jnp.where(mask, sp, 0.0)
        sps.append(sp.astype(BF16))
    csums, carries = [], []
    for spb, (carry, acc) in zip(sps, states):
        parts = []
        for c in reversed(range(spb.shape[1] // SUB)):
            loc = jnp.dot(spb[:, c * SUB:(c + 1) * SUB], tri, preferred_element_type=F32)
            parts.insert(0, loc + carry)
            carry = carry + loc[:, 0:1]
        csums.append(parts[0] if len(parts) == 1 else jnp.concatenate(parts, axis=1))
        carries.append(carry)
    out = []
    for z, csum, mask, carry, (q, k, v), (_, acc) in zip(zs, csums, masks, carries, ops, states):
        a = jnp.exp2(z - csum)
        if mask is not None:
            a = jnp.where(mask, a, 0.0)
        out.append((carry, acc + jnp.dot(a.astype(BF16), v, preferred_element_type=F32)))
    return tuple(out)


def _sb_body(q_ref, k_ref, v_ref, o_ref, *, t):
    i = pl.program_id(2)
    halves = t // SUB
    r = lax.broadcasted_iota(jnp.int32, (SUB, SUB), 0)
    c = lax.broadcasted_iota(jnp.int32, (SUB, SUB), 1)
    tri = jnp.where(r >= c, 1.0, 0.0).astype(BF16)
    start = pl.multiple_of(i * t, t)
    chains = [(g, a) for g in range(ATT_HEADS) for a in range(halves)]

    def operands(g, a, row0, nrows):
        cols = slice(g * HEAD_DIM, (g + 1) * HEAD_DIM)
        return (q_ref[a * SUB:(a + 1) * SUB, cols], k_ref[pl.ds(row0, nrows), cols],
                v_ref[pl.ds(row0, nrows), cols])

    ops, masks, st0 = [], [], []
    for g, a in chains:
        n = (a + 1) * SUB
        rr = lax.broadcasted_iota(jnp.int32, (SUB, n), 0)
        cc = lax.broadcasted_iota(jnp.int32, (SUB, n), 1)
        ops.append(operands(g, a, start, n))
        masks.append(cc < rr + a * SUB)
        st0.append((jnp.zeros((SUB, 1), F32), jnp.zeros((SUB, HEAD_DIM), F32)))
    state = _sb_chains(ops, tri, st0, masks)

    def step(jj, st):
        row0 = pl.multiple_of((i - 1 - jj) * t, t)
        return _sb_chains([operands(g, a, row0, t) for g, a in chains], tri, st,
                          [None] * len(chains))

    state = lax.fori_loop(0, i, step, state)
    for n, (g, a) in enumerate(chains):
        o_ref[a * SUB:(a + 1) * SUB, g * HEAD_DIM:(g + 1) * HEAD_DIM] = state[n][1].astype(o_ref.dtype)


def _sb_attention(qkv, B, S):
    t = _tile(S, 2 * SUB)
    nq = S // t
    G = ATT_HEADS
    HG = SB_HEADS // G
    W = G * HEAD_DIM
    return _call(
        functools.partial(_sb_body, t=t), grid=(B, HG, nq),
        in_specs=[pl.BlockSpec((t, W), lambda b, h, i: (b * nq + i, h)),
                  pl.BlockSpec((S, W), lambda b, h, i: (b, HG + h)),
                  pl.BlockSpec((S, W), lambda b, h, i: (b, 2 * HG + h))],
        out_specs=pl.BlockSpec((t, W), lambda b, h, i: (b * nq + i, h)),
        out_shape=jax.ShapeDtypeStruct((B * S, SB_WIDTH), BF16),
        name="sb_attention")(qkv, qkv, qkv)


def _mla_chains(ops, states, masks):
    ss = []
    for (q, k, v), mask in zip(ops, masks):
        s = _qk(q, k)
        if mask is not None:
            s = jnp.where(mask, s, -jnp.inf)
        ss.append(s)
    ms = [jnp.maximum(m, jnp.max(s, axis=-1, keepdims=True)) for s, (m, l, acc) in zip(ss, states)]
    ps = [jnp.exp2(s - m_new) for s, m_new in zip(ss, ms)]
    out = []
    for p, m_new, (q, k, v), (m, l, acc) in zip(ps, ms, ops, states):
        alpha = jnp.exp2(m - m_new)
        l = alpha * l + jnp.sum(p, axis=-1, keepdims=True)
        acc = alpha * acc + jnp.dot(p.astype(BF16), v, preferred_element_type=F32)
        out.append((m_new, l, acc))
    return tuple(out)


def _mla_body(q_ref, kv_ref, kpe_ref, o_ref, kf_ref, *, t):
    i = pl.program_id(2)
    halves = t // SUB
    hw = MLA_NOPE + MLA_V

    @pl.when(i == 0)
    def _():
        for g in range(ATT_HEADS):
            kf_ref[g, :, 0:LANE] = kv_ref[:, g * hw:g * hw + MLA_NOPE]
            kf_ref[g, :, LANE:QPAD] = kpe_ref[...]

    start = pl.multiple_of(i * t, t)
    chains = [(g, a) for g in range(ATT_HEADS) for a in range(halves)]

    def operands(g, a, row0, nrows):
        return (q_ref[a * SUB:(a + 1) * SUB, g * QPAD:(g + 1) * QPAD],
                kf_ref[g, pl.ds(row0, nrows), :],
                kv_ref[pl.ds(row0, nrows), g * hw + MLA_NOPE:(g + 1) * hw])

    ops, masks, st0 = [], [], []
    for g, a in chains:
        n = (a + 1) * SUB
        rr = lax.broadcasted_iota(jnp.int32, (SUB, n), 0)
        cc = lax.broadcasted_iota(jnp.int32, (SUB, n), 1)
        ops.append(operands(g, a, start, n))
        masks.append(cc <= rr + a * SUB)
        st0.append((jnp.full((SUB, 1), -jnp.inf, F32), jnp.zeros((SUB, 1), F32),
                    jnp.zeros((SUB, MLA_V), F32)))
    state = _mla_chains(ops, st0, masks)

    def step(j, st):
        row0 = pl.multiple_of(j * t, t)
        return _mla_chains([operands(g, a, row0, t) for g, a in chains], st, [None] * len(chains))

    state = lax.fori_loop(0, i, step, state)
    for n, (g, a) in enumerate(chains):
        m, l, acc = state[n]
        o_ref[a * SUB:(a + 1) * SUB, g * MLA_V:(g + 1) * MLA_V] = (acc / l).astype(o_ref.dtype)


def _mla_attention(q_full, kv, kpe, B, S):
    t = _tile(S, 2 * SUB)
    nq = S // t
    G = ATT_HEADS
    HG = MLA_HEADS // G
    return _call(
        functools.partial(_mla_body, t=t), grid=(B, HG, nq),
        in_specs=[pl.BlockSpec((t, G * QPAD), lambda b, h, i: (b * nq + i, h)),
                  pl.BlockSpec((S, G * (MLA_NOPE + MLA_V)), lambda b, h, i: (b, h)),
                  pl.BlockSpec((S, LANE), lambda b, h, i: (b, 0))],
        out_specs=pl.BlockSpec((t, G * MLA_V), lambda b, h, i: (b * nq + i, h)),
        out_shape=jax.ShapeDtypeStruct((B * S, MLA_HEADS * MLA_V), BF16),
        scratch=[pltpu.VMEM((G, S, QPAD), BF16)],
        name="mla_attention")(q_full, kv, kpe)


def _merge_body(osb_ref, omla_ref, gsh_ref, gtail_ref, wsb_ref, wmla_ref, wo_ref, x_ref,
                g1_ref, gain_ref, sc_ref, sh_ref, xo_ref, ho_ref):
    D = x_ref.shape[1]
    gate_sb = gsh_ref[:, :D + LANE].astype(F32)
    gate_mla = jnp.concatenate([gsh_ref[:, D:], gtail_ref[...]], axis=1).astype(F32)
    y = (gate_sb * jnp.dot(osb_ref[...], wsb_ref[...], preferred_element_type=F32)
         + gate_mla * jnp.dot(omla_ref[...], wmla_ref[...], preferred_element_type=F32))
    out = jnp.dot(y.astype(BF16), wo_ref[...], preferred_element_type=F32)
    x_new = x_ref[...] + g1_ref[...] * out
    xo_ref[...] = x_new
    ho_ref[...] = _norm_mod(x_new, gain_ref[...], sc_ref[...], sh_ref[...]).astype(ho_ref.dtype)


def _merge(o_sb, o_mla, gsh, gtail, w_sb, w_mla, w_o, layer, x, g1, gain, sc, sh, S, h_dtype):
    M, D = x.shape
    tm = _tile(S, 256)
    bpt = S // tm
    vec = pl.BlockSpec((None, 1, D), lambda m: (m // bpt, 0, 0))
    res = lambda k, n: pl.BlockSpec((None, k, n), lambda m: (layer, 0, 0),
                                    pipeline_mode=pl.Buffered(1))
    return _call(
        _merge_body, grid=(M // tm,),
        in_specs=[pl.BlockSpec((tm, SB_WIDTH), lambda m: (m, 0)),
                  pl.BlockSpec((tm, MLA_HEADS * MLA_V), lambda m: (m, 0)),
                  pl.BlockSpec((tm, 2 * D), lambda m: (m, 0)),
                  pl.BlockSpec((tm, LANE), lambda m: (m, 0)),
                  res(SB_WIDTH, D + LANE), res(MLA_HEADS * MLA_V, D + LANE), res(D + LANE, D),
                  pl.BlockSpec((tm, D), lambda m: (m, 0)),
                  vec, pl.BlockSpec((1, D), lambda m: (0, 0)), vec, vec],
        out_specs=[pl.BlockSpec((tm, D), lambda m: (m, 0)),
                   pl.BlockSpec((tm, D), lambda m: (m, 0))],
        out_shape=[jax.ShapeDtypeStruct((M, D), F32), jax.ShapeDtypeStruct((M, D), h_dtype)],
        name="merge_out")(o_sb, o_mla, gsh, gtail, w_sb, w_mla, w_o, x, g1, gain, sc, sh)


def _resnorm_body(f_ref, x_ref, g_ref, gain_ref, sc_ref, sh_ref, xo_ref, ho_ref):
    x_new = x_ref[...] + g_ref[...] * f_ref[...]
    xo_ref[...] = x_new
    ho_ref[...] = _norm_mod(x_new, gain_ref[...], sc_ref[...], sh_ref[...]).astype(ho_ref.dtype)


def _resnorm(f, x, g, gain, sc, sh, S, h_dtype):
    M, D = x.shape
    tm = _tile(S, 512)
    bpt = S // tm
    vec = pl.BlockSpec((None, 1, D), lambda m: (m // bpt, 0, 0))
    rows = pl.BlockSpec((tm, D), lambda m: (m, 0))
    return _call(
        _resnorm_body, grid=(M // tm,),
        in_specs=[rows, rows, vec, pl.BlockSpec((1, D), lambda m: (0, 0)), vec, vec],
        out_specs=[rows, rows],
        out_shape=[jax.ShapeDtypeStruct((M, D), F32), jax.ShapeDtypeStruct((M, D), h_dtype)],
        name="resnorm")(f, x, g, gain, sc, sh)


def _split_bf16(v):
    hi = v.astype(BF16)
    return hi, (v - hi.astype(F32)).astype(BF16)


def _router_body(h_ref, w_ref, idx_ref, wt_ref):
    hh, hl = _split_bf16(h_ref[...])
    wh, wl = _split_bf16(w_ref[...])
    dot = functools.partial(jnp.dot, preferred_element_type=F32)
    logits = dot(hh, wh) + (dot(hl, wh) + dot(hh, wl))
    lane = lax.broadcasted_iota(jnp.int32, logits.shape, 1).astype(F32)
    logits = jnp.where(lane < N_EXPERTS, logits, -jnp.inf)
    m1 = jnp.max(logits, axis=-1, keepdims=True)
    i1 = jnp.min(jnp.where(logits == m1, lane, float(LANE)), axis=-1, keepdims=True)
    rest = jnp.where(lane == i1, -jnp.inf, logits)
    m2 = jnp.max(rest, axis=-1, keepdims=True)
    i2 = jnp.min(jnp.where(rest == m2, lane, float(LANE)), axis=-1, keepdims=True)
    e = jnp.exp(m2 - m1)
    w1 = 1.0 / (1.0 + e)
    w2 = e * w1
    idx_ref[...] = jnp.where(lane == 0.0, i1, jnp.where(lane == 1.0, i2, 0.0)).astype(jnp.int32)
    wt_ref[...] = jnp.where(lane == 0.0, w1, jnp.where(lane == 1.0, w2, 0.0))


def _router(h, w_router_pad, j):
    M, D = h.shape
    tm = _tile(M, 512)
    return _call(
        _router_body, grid=(M // tm,),
        in_specs=[pl.BlockSpec((tm, D), lambda m: (m, 0)),
                  pl.BlockSpec((None, D, LANE), lambda m: (j, 0, 0))],
        out_specs=[pl.BlockSpec((tm, LANE), lambda m: (m, 0)),
                   pl.BlockSpec((tm, LANE), lambda m: (m, 0))],
        out_shape=[jax.ShapeDtypeStruct((M, LANE), jnp.int32),
                   jax.ShapeDtypeStruct((M, LANE), F32)],
        name="router")(h, w_router_pad)


MOE_SUB = 256
ROW_UNROLL = 8


def _gather_body(src_ref, h_hbm, o_ref, buf_ref, sem, *, tm):
    t = pl.program_id(0)

    def issue(tile):
        slot = tile % 2

        def blk(b, _):
            for u in range(ROW_UNROLL):
                r = b * ROW_UNROLL + u
                pltpu.make_async_copy(h_hbm.at[pl.ds(src_ref[tile * tm + r], 1)],
                                      buf_ref.at[slot, pl.ds(r, 1)],
                                      sem.at[slot]).start(priority=u % 2)
            return 0

        lax.fori_loop(0, tm // ROW_UNROLL, blk, 0)

    @pl.when(t == 0)
    def _():
        issue(t)

    @pl.when(t + 1 < pl.num_programs(0))
    def _():
        issue(t + 1)

    slot = t % 2
    pltpu.make_async_copy(h_hbm.at[pl.ds(0, tm)], buf_ref.at[slot], sem.at[slot]).wait()
    o_ref[...] = buf_ref[slot].astype(o_ref.dtype)


def _gather_rows(src_tok, h, n_tiles, tm):
    M, D = h.shape
    return _call(
        functools.partial(_gather_body, tm=tm), grid=(n_tiles,), nsp=1,
        in_specs=[pl.BlockSpec(memory_space=pl.ANY)],
        out_specs=pl.BlockSpec((tm, D), lambda t, src: (t, 0)),
        out_shape=jax.ShapeDtypeStruct((n_tiles * tm, D), BF16),
        scratch=[pltpu.VMEM((2, tm, D), F32), pltpu.SemaphoreType.DMA((2,))],
        name="moe_gather")(src_tok, h)


def _grouped_body(te_ref, a_ref, *refs, n_w, layer, epilogue):
    w_hbm = refs[:n_w]
    n_rest = len(refs) - 3 * n_w - 1
    rest = refs[n_w:n_w + n_rest]
    stage = refs[n_w + n_rest:2 * n_w + n_rest]
    wb_refs = refs[2 * n_w + n_rest:3 * n_w + n_rest]
    sem = refs[-1]
    c, t = pl.program_id(0), pl.program_id(1)
    nc, nt = pl.num_programs(0), pl.num_programs(1)
    expert = te_ref[t]
    n_valid = te_ref[nt + t]
    nxt = te_ref[2 * nt + t]
    changed = jnp.logical_or(t == 0, expert != te_ref[jnp.maximum(t - 1, 0)])
    tn = stage[0].shape[1]

    def block_copy(i, chunk, e):
        col0 = pl.multiple_of(chunk * tn, tn)
        return pltpu.make_async_copy(w_hbm[i].at[layer, e, :, pl.ds(col0, tn)], stage[i],
                                     sem.at[i])

    @pl.when(changed)
    def _():
        @pl.when(jnp.logical_and(c == 0, t == 0))
        def _():
            for i in range(n_w):
                block_copy(i, c, expert).start()

        for i in range(n_w):
            block_copy(i, c, expert).wait()
            wb_refs[i][...] = stage[i][...].astype(BF16)

        @pl.when(nxt >= 0)
        def _():
            for i in range(n_w):
                block_copy(i, c, nxt).start()

        @pl.when(jnp.logical_and(nxt < 0, c + 1 < nc))
        def _():
            for i in range(n_w):
                block_copy(i, c + 1, te_ref[0]).start()

    for s in range(a_ref.shape[0] // MOE_SUB):
        rows = slice(s * MOE_SUB, (s + 1) * MOE_SUB)

        @pl.when(s * MOE_SUB < n_valid)
        def _():
            a = a_ref[rows, :]
            accs = [jnp.dot(a, wb_ref[...], preferred_element_type=F32) for wb_ref in wb_refs]
            epilogue(accs, rest[:n_rest], rows)

        @pl.when(s * MOE_SUB >= n_valid)
        def _():
            for o_ref in rest[:n_rest]:
                o_ref[rows, :] = jnp.zeros((MOE_SUB, o_ref.shape[1]), o_ref.dtype)


def _up_epilogue(accs, refs, rows=slice(None)):
    g, u = accs
    refs[0][rows, :] = (g * jax.nn.sigmoid(g) * u).astype(refs[0].dtype)


def _grouped_up(tile_e, xs, w_gate, w_up, j, tm, tf):
    NP, D = xs.shape
    F = w_gate.shape[-1]
    hbm = pl.BlockSpec(memory_space=pl.ANY)
    return _call(
        functools.partial(_grouped_body, n_w=2, layer=j, epilogue=_up_epilogue),
        grid=(F // tf, NP // tm), nsp=1,
        in_specs=[pl.BlockSpec((tm, D), lambda f, t, te: (t, 0)), hbm, hbm],
        out_specs=pl.BlockSpec((tm, tf), lambda f, t, te: (t, f)),
        out_shape=jax.ShapeDtypeStruct((NP, F), BF16),
        scratch=[pltpu.VMEM((D, tf), F32), pltpu.VMEM((D, tf), F32),
                 pltpu.VMEM((D, tf), BF16), pltpu.VMEM((D, tf), BF16),
                 pltpu.SemaphoreType.DMA((2,))],
        name="moe_up")(tile_e, xs, w_gate, w_up)


def _down_epilogue(accs, refs, rows):
    refs[0][rows, :] = accs[0]


def _grouped_down(tile_e, hs, w_down, j, tm, tn):
    NP, F = hs.shape
    D = w_down.shape[-1]
    return _call(
        functools.partial(_grouped_body, n_w=1, layer=j, epilogue=_down_epilogue),
        grid=(D // tn, NP // tm), nsp=1,
        in_specs=[pl.BlockSpec((tm, F), lambda n, t, te: (t, 0)),
                  pl.BlockSpec(memory_space=pl.ANY)],
        out_specs=pl.BlockSpec((tm, tn), lambda n, t, te: (t, n)),
        out_shape=jax.ShapeDtypeStruct((NP, D), F32),
        scratch=[pltpu.VMEM((F, tn), F32), pltpu.VMEM((F, tn), BF16),
                 pltpu.SemaphoreType.DMA((1,))],
        name="moe_down")(tile_e, hs, w_down)


def _combine_body(pos_ref, y_hbm, wt_ref, x_ref, g2_ref, gain_ref, sc_ref, sh_ref, xo_ref, ho_ref,
                  buf_ref, sem, *, tm):
    t = pl.program_id(0)

    def issue(tile):
        slot = tile % 2

        def blk(b, _):
            for u in range(ROW_UNROLL):
                r = b * ROW_UNROLL + u
                for k in range(TOP_K):
                    pltpu.make_async_copy(
                        y_hbm.at[pl.ds(pos_ref[(tile * tm + r) * TOP_K + k], 1)],
                        buf_ref.at[slot, k, pl.ds(r, 1)], sem.at[slot]).start(priority=k % 2)
            return 0

        lax.fori_loop(0, tm // ROW_UNROLL, blk, 0)

    @pl.when(t == 0)
    def _():
        issue(t)

    @pl.when(t + 1 < pl.num_programs(0))
    def _():
        issue(t + 1)

    slot = t % 2
    for k in range(TOP_K):
        pltpu.make_async_copy(y_hbm.at[pl.ds(0, tm)], buf_ref.at[slot, k], sem.at[slot]).wait()
    wt = wt_ref[...]
    f = buf_ref[slot, 0] * wt[:, 0:1]
    for k in range(1, TOP_K):
        f = f + buf_ref[slot, k] * wt[:, k:k + 1]
    x_new = x_ref[...] + g2_ref[...] * f
    xo_ref[...] = x_new
    ho_ref[...] = _norm_mod(x_new, gain_ref[...], sc_ref[...], sh_ref[...]).astype(ho_ref.dtype)


def _combine(pos, y_sorted, wts, x, g2, gain, sc, sh, S, h_dtype):
    M, D = x.shape
    tm = _tile(S, 256)
    bpt = S // tm
    vec = pl.BlockSpec((None, 1, D), lambda m, p: (m // bpt, 0, 0))
    return _call(
        functools.partial(_combine_body, tm=tm), grid=(M // tm,), nsp=1,
        in_specs=[pl.BlockSpec(memory_space=pl.ANY),
                  pl.BlockSpec((tm, LANE), lambda m, p: (m, 0)),
                  pl.BlockSpec((tm, D), lambda m, p: (m, 0)),
                  vec, pl.BlockSpec((1, D), lambda m, p: (0, 0)), vec, vec],
        out_specs=[pl.BlockSpec((tm, D), lambda m, p: (m, 0)),
                   pl.BlockSpec((tm, D), lambda m, p: (m, 0))],
        out_shape=[jax.ShapeDtypeStruct((M, D), F32), jax.ShapeDtypeStruct((M, D), h_dtype)],
        scratch=[pltpu.VMEM((2, TOP_K, tm, D), F32), pltpu.SemaphoreType.DMA((2,))],
        name="moe_combine")(pos, y_sorted, wts, x, g2, gain, sc, sh)


def _dispatch_plan(top_i, tm):
    M = top_i.shape[0]
    E = N_EXPERTS
    n_ent = M * TOP_K
    e_flat = top_i.reshape(-1)
    onehot = (e_flat[:, None] == jnp.arange(E, dtype=jnp.int32)[None, :]).astype(jnp.int32)
    csum = jnp.cumsum(onehot, axis=0)
    rank = jnp.sum(onehot * csum, axis=1) - 1
    counts = csum[-1]
    pcounts = ((counts + tm - 1) // tm) * tm
    pend = jnp.cumsum(pcounts)
    pstart = pend - pcounts
    dest = (pstart[e_flat] + rank).astype(jnp.int32)
    n_tiles = n_ent // tm + E
    n_rows = n_tiles * tm
    src_tok = jnp.zeros((n_rows,), jnp.int32).at[dest].set(
        jnp.arange(n_ent, dtype=jnp.int32) // TOP_K)
    tile_start = jnp.arange(n_tiles, dtype=jnp.int32) * tm
    tile_e = jnp.sum((tile_start[:, None] >= pend[None, :]).astype(jnp.int32), axis=1)
    tile_e = jnp.minimum(tile_e, E - 1).astype(jnp.int32)
    n_valid = jnp.clip(pstart[tile_e] + counts[tile_e] - tile_start, 0, tm).astype(jnp.int32)
    n_active = pend[-1] // tm
    tile_e = jnp.where(tile_start < pend[-1], tile_e, tile_e[n_active - 1])
    tidx = jnp.arange(n_tiles, dtype=jnp.int32)
    later_other = (tidx[None, :] > tidx[:, None]) & (tile_e[None, :] != tile_e[:, None])
    nxt = jnp.where(jnp.any(later_other, axis=1), tile_e[jnp.argmax(later_other, axis=1)], -1)
    return src_tok, dest, jnp.concatenate([tile_e, n_valid, nxt]).astype(jnp.int32), n_tiles


def _rot_cols(w):
    half = MLA_ROPE // 2
    return jnp.concatenate([-w[..., half:], w[..., :half]], axis=-1)


def _rope_tables(positions):
    inv_freq = 1.0 / (ROPE_THETA ** (jnp.arange(0, MLA_ROPE, 2, dtype=F32) / MLA_ROPE))
    ang = positions.astype(F32)[..., None] * inv_freq
    cos, sin = jnp.cos(ang), jnp.sin(ang)
    M = positions.size
    cos2 = jnp.concatenate([cos, cos], axis=-1).reshape(M, MLA_ROPE)
    sin2 = jnp.concatenate([sin, sin], axis=-1).reshape(M, MLA_ROPE)
    zeros = jnp.zeros((M, LANE - MLA_ROPE), F32)
    cos_k = jnp.concatenate([cos2, zeros], axis=-1)
    sin_k = jnp.concatenate([sin2, zeros], axis=-1)
    return cos_k, sin_k, cos_k * MLA_SCALE, sin_k * MLA_SCALE


def kernel(x, c, positions, w_ada, b_ada, norm_mix_g, norm_ffn_g, w_in, q_norm_g, kv_norm_g, w_uq, w_ukv, w_sb_up, w_mla_up, w_o, w_ffn_gate, w_ffn_up, w_ffn_down, w_router, w_exp_gate, w_exp_up, w_exp_down, final_norm_g):
    B, S, D = x.shape
    L = w_ada.shape[0]
    M = B * S
    x = x.reshape(M, D)

    mod = _modulation(c, w_ada, b_ada)
    mod = mod.reshape(L, B, N_MOD, 1, D)
    sh1, sc1, g1, sh2, sc2, g2 = [mod[:, :, k] for k in range(N_MOD)]
    cos_k, sin_k, cos_q, sin_q = _rope_tables(positions)

    pe0 = 3 * SB_WIDTH + Q_RANK + KV_RANK
    w_in_t = jnp.swapaxes(w_in, 1, 2)
    w_pe = w_in_t[:, pe0:pe0 + MLA_ROPE]
    w_tail = w_in_t[:, pe0 + 2 * D:]
    zpad = jnp.zeros((L, LANE - MLA_ROPE, D), F32)
    half = MLA_ROPE // 2
    w_pe_rot = jnp.concatenate([-w_pe[:, half:], w_pe[:, :half]], axis=1)
    w_pe2 = jnp.concatenate([w_pe, zpad, w_pe_rot, zpad, w_tail, zpad], axis=1)
    c_gain = jnp.concatenate([q_norm_g, kv_norm_g], axis=-1)[:, None, :]

    wq = w_uq.reshape(L, Q_RANK, MLA_HEADS, MLA_QK)
    wq_n, wq_r = wq[..., :MLA_NOPE], wq[..., MLA_NOPE:]
    zq = jnp.zeros(wq_r.shape, F32)
    wq_main = jnp.concatenate([wq_n, wq_r, zq], axis=-1).reshape(
        L, Q_RANK, MLA_HEADS * QPAD).astype(BF16)
    wq_rot = jnp.concatenate([_rot_cols(wq_r), zq], axis=-1).reshape(
        L, Q_RANK, MLA_HEADS * LANE).astype(BF16)

    lane_pad = ((0, 0), (0, 0), (GATE_SHIFT, LANE - GATE_SHIFT))
    w_sb_b = jnp.pad(w_sb_up, lane_pad).astype(BF16)
    w_mla_b = jnp.pad(w_mla_up, lane_pad).astype(BF16)
    w_o_b = jnp.pad(w_o, ((0, 0), (GATE_SHIFT, LANE - GATE_SHIFT), (0, 0))).astype(BF16)
    w_router_pad = jnp.concatenate(
        [w_router, jnp.zeros(w_router.shape[:2] + (LANE - N_EXPERTS,), F32)], axis=-1)

    tm = _tile(S, 1024)
    bpt = S // tm
    row = lambda width: pl.BlockSpec((tm, width), lambda n, m: (m, 0))
    sb_scale = HEAD_DIM ** -0.5 * LOG2E

    def h_dtype_for(layer_is_moe):
        return F32 if layer_is_moe else BF16

    h = _prenorm(x, norm_mix_g[0][None, :], sc1[0], sh1[0], S, BF16)

    for l in range(L):
        tn = _tile(min(SB_WIDTH, 2 * D), 1024)
        nq_blk = SB_WIDTH // tn

        def qkv_epi(accs, n, extra, outs):
            s = jnp.where(n < nq_blk, sb_scale, 1.0)
            outs[0][...] = (accs[0] * s).astype(BF16)

        (qkv,) = _matmul(h, [w_in_t], layer=l, col_blk0=0, n_cols=3 * SB_WIDTH, tn=tn, tm=tm,
                         epilogue=qkv_epi, transposed=True,
                         out_shape=[jax.ShapeDtypeStruct((M, 3 * SB_WIDTH), BF16)],
                         out_specs=[pl.BlockSpec((tm, tn), lambda n, m: (m, n))], name="in_qkv")

        def lat_epi(accs, n, extra, outs):
            a = accs[0]
            ms = jnp.mean(a * a, axis=-1, keepdims=True)
            outs[0][...] = (a * lax.rsqrt(ms + NORM_EPS) * extra[0][...]).astype(BF16)

        (lat,) = _matmul(h, [w_in_t], layer=l, col_blk0=3 * SB_WIDTH // Q_RANK, transposed=True,
                         n_cols=Q_RANK + KV_RANK, tn=Q_RANK, tm=tm, epilogue=lat_epi,
                         extras=[c_gain],
                         extra_specs=[pl.BlockSpec((None, 1, Q_RANK), lambda n, m: (l, 0, n))],
                         out_shape=[jax.ShapeDtypeStruct((M, Q_RANK + KV_RANK), BF16)],
                         out_specs=[pl.BlockSpec((tm, Q_RANK), lambda n, m: (m, n))],
                         name="in_latent")

        def pe_epi(accs, n, extra, outs):
            a = accs[0]
            outs[0][...] = (a[:, :LANE] * extra[0][...]
                            + a[:, LANE:2 * LANE] * extra[1][...]).astype(BF16)
            outs[1][...] = jax.nn.sigmoid(a[:, 2 * LANE:]).astype(BF16)

        lane_blk = pl.BlockSpec((tm, LANE), lambda n, m: (m, 0))
        kpe, gtail = _matmul(h, [w_pe2], layer=l, col_blk0=0, n_cols=3 * LANE, tn=3 * LANE, tm=tm,
                             epilogue=pe_epi, extras=[cos_k, sin_k], transposed=True,
                             extra_specs=[row(LANE), row(LANE)],
                             out_shape=[jax.ShapeDtypeStruct((M, LANE), BF16)] * 2,
                             out_specs=[lane_blk, lane_blk], name="in_kpe")

        def gate_epi(accs, n, extra, outs):
            outs[0][...] = jax.nn.sigmoid(accs[0]).astype(BF16)

        (gsh,) = _matmul(h, [w_in_t], layer=l, col_blk0=pe0 // tn, n_cols=2 * D, tn=tn, tm=tm,
                         epilogue=gate_epi, transposed=True,
                         out_shape=[jax.ShapeDtypeStruct((M, 2 * D), BF16)],
                         out_specs=[pl.BlockSpec((tm, tn), lambda n, m: (m, n))],
                         name="in_gates")

        q_full, kv = _latent_up(lat, wq_main, wq_rot, w_ukv, l, cos_q, sin_q, tm)

        o_sb = _sb_attention(qkv, B, S)
        o_mla = _mla_attention(q_full, kv, kpe, B, S)

        moe = (l % 2 == 1)
        x, h = _merge(o_sb, o_mla, gsh, gtail, w_sb_b, w_mla_b, w_o_b, l, x, g1[l],
                      norm_ffn_g[l][None, :], sc2[l], sh2[l], S, h_dtype_for(moe))

        last = (l == L - 1)
        if last:
            n_gain = final_norm_g[None, :]
            n_sc = jnp.zeros_like(sc1[0])
            n_sh = jnp.zeros_like(sh1[0])
            n_dtype = F32
        else:
            n_gain, n_sc, n_sh, n_dtype = norm_mix_g[l + 1][None, :], sc1[l + 1], sh1[l + 1], BF16
        j = l // 2
        if not moe:
            F = w_ffn_gate.shape[-1]
            tf = _tile(F, 512)

            def up_epi(accs, n, extra, outs):
                _up_epilogue(accs, outs)

            (hmid,) = _matmul(h, [w_ffn_gate, w_ffn_up], layer=j, col_blk0=0, n_cols=F, tn=tf,
                              tm=tm, epilogue=up_epi,
                              out_shape=[jax.ShapeDtypeStruct((M, F), BF16)],
                              out_specs=[pl.BlockSpec((tm, tf), lambda n, m: (m, n))],
                              name="ffn_up")
            def down_epi(accs, n, extra, outs):
                outs[0][...] = accs[0]

            tnd = _tile(D, 512)
            (f,) = _matmul(hmid, [w_ffn_down], layer=j, col_blk0=0, n_cols=D, tn=tnd,
                           tm=_tile(S, 512), epilogue=down_epi,
                           out_shape=[jax.ShapeDtypeStruct((M, D), F32)],
                           out_specs=[pl.BlockSpec((_tile(S, 512), tnd), lambda n, m: (m, n))],
                           name="ffn_down")
            x, h = _resnorm(f, x, g2[l], n_gain, n_sc, n_sh, S, n_dtype)
        else:
            F = w_exp_gate.shape[-1]
            tme = max(_tile(M, 2 * MOE_SUB), MOE_SUB)
            idx, wts = _router(h, w_router_pad, j)
            src_tok, dest, tile_e, n_tiles = _dispatch_plan(idx[:, :TOP_K], tme)
            xs = _gather_rows(src_tok, h, n_tiles, tme)
            tfe = F // 4 if F % (4 * LANE) == 0 else _tile(F, 512)
            hs = _grouped_up(tile_e, xs, w_exp_gate, w_exp_up, j, tme, tfe)
            ys = _grouped_down(tile_e, hs, w_exp_down, j, tme, _tile(D, 512))
            x, h = _combine(dest, ys, wts, x, g2[l], n_gain, n_sc, n_sh, S, n_dtype)

    return h.reshape(B, S, D)
```

```python
import functools

import jax
import jax.numpy as jnp
from jax import lax
from jax.experimental import pallas as pl
from jax.experimental.pallas import tpu as pltpu

BF16 = jnp.bfloat16
F32 = jnp.float32

SB_HEADS = 8
HEAD_DIM = 128
SB_WIDTH = SB_HEADS * HEAD_DIM
MLA_HEADS = 8
MLA_NOPE = 128
MLA_ROPE = 64
MLA_V = 128
MLA_QK = MLA_NOPE + MLA_ROPE
Q_RANK = 512
KV_RANK = 512
ROPE_THETA = 10000.0
N_EXPERTS = 8
TOP_K = 2
N_MOD = 6
NORM_EPS = 1e-6

LOG2E = 1.4426950408889634
MLA_SCALE = MLA_QK ** -0.5 * LOG2E

LANE = 128
QPAD = 2 * LANE
VMEM_LIMIT = 56 << 20


def _call(body, *, grid, in_specs, out_specs, out_shape, scratch=(), nsp=0, name, flags=None):
    return pl.pallas_call(
        body,
        grid_spec=pltpu.PrefetchScalarGridSpec(
            num_scalar_prefetch=nsp, grid=grid, in_specs=in_specs,
            out_specs=out_specs, scratch_shapes=scratch),
        out_shape=out_shape,
        compiler_params=pltpu.CompilerParams(
            dimension_semantics=("arbitrary",) * len(grid),
            vmem_limit_bytes=VMEM_LIMIT, flags=flags),
        name=name)


def _tile(n, pref):
    t = min(n, pref)
    while n % t:
        t //= 2
    return t


def _norm_mod(x, gain, sc, sh):
    ms = jnp.mean(x * x, axis=-1, keepdims=True)
    y = x * lax.rsqrt(ms + NORM_EPS)
    return (y * gain) * (1.0 + sc) + sh


def _mm_body(*refs, n_w, n_extra, n_out, epilogue, transposed):
    a_ref = refs[0]
    w_refs = refs[1:1 + n_w]
    extra = refs[1 + n_w:1 + n_w + n_extra]
    outs = refs[1 + n_w + n_extra:1 + n_w + n_extra + n_out]
    wb_refs = refs[1 + n_w + n_extra + n_out:]

    @pl.when(pl.program_id(1) == 0)
    def _():
        for w_ref, wb_ref in zip(w_refs, wb_refs):
            wb_ref[...] = w_ref[...].astype(BF16)

    a = a_ref[...]
    dims = (((1,), (1,)), ((), ())) if transposed else (((1,), (0,)), ((), ()))
    accs = [lax.dot_general(a, wb_ref[...], dims, preferred_element_type=F32)
            for wb_ref in wb_refs]
    epilogue(accs, pl.program_id(0), extra, outs)


def _matmul(a, ws, *, layer, col_blk0, n_cols, tn, tm, epilogue, extras=(), extra_specs=(),
            out_shape, out_specs, name, a_blk=0, transposed=False):
    M = a.shape[0]
    K = ws[0].shape[2 if transposed else 1]
    grid = (n_cols // tn, M // tm)
    in_specs = [pl.BlockSpec((tm, K), lambda n, m: (m, a_blk))]
    for _ in ws:
        if transposed:
            in_specs.append(pl.BlockSpec((None, tn, K), lambda n, m: (layer, n + col_blk0, 0)))
        else:
            in_specs.append(pl.BlockSpec((None, K, tn), lambda n, m: (layer, 0, n + col_blk0)))
    in_specs += list(extra_specs)
    body = functools.partial(_mm_body, n_w=len(ws), n_extra=len(extras), n_out=len(out_shape),
                             epilogue=epilogue, transposed=transposed)
    wb_shape = (tn, K) if transposed else (K, tn)
    return _call(body, grid=grid, in_specs=in_specs, out_specs=out_specs, out_shape=out_shape,
                 scratch=[pltpu.VMEM(wb_shape, BF16) for _ in ws], name=name)(a, *ws, *extras)


def _mm_rows_body(a_ref, w_hbm, o_ref, stage_ref, wb_ref, sem, *, layer, row0, epilogue):
    n, m = pl.program_id(0), pl.program_id(1)
    tn = stage_ref.shape[0]

    def chunk_copy(chunk):
        r0 = pl.multiple_of(row0 + chunk * tn, 8)
        return pltpu.make_async_copy(w_hbm.at[layer, pl.ds(r0, tn), :], stage_ref, sem.at[0])

    @pl.when(m == 0)
    def _():
        @pl.when(n == 0)
        def _():
            chunk_copy(n).start()

        chunk_copy(n).wait()
        wb_ref[...] = stage_ref[...].astype(BF16)

        @pl.when(n + 1 < pl.num_programs(0))
        def _():
            chunk_copy(n + 1).start()

    acc = lax.dot_general(a_ref[...], wb_ref[...], (((1,), (1,)), ((), ())),
                          preferred_element_type=F32)
    epilogue([acc], n, (), [o_ref])


def _matmul_rows(a, w_t, *, layer, row0, n_cols, tn, tm, epilogue, out_dtype, name):
    M, K = a.shape
    body = functools.partial(_mm_rows_body, layer=layer, row0=row0, epilogue=epilogue)
    return _call(body, grid=(n_cols // tn, M // tm),
                 in_specs=[pl.BlockSpec((tm, K), lambda n, m: (m, 0)),
                           pl.BlockSpec(memory_space=pl.ANY)],
                 out_specs=pl.BlockSpec((tm, tn), lambda n, m: (m, n)),
                 out_shape=jax.ShapeDtypeStruct((M, n_cols), out_dtype),
                 scratch=[pltpu.VMEM((tn, K), F32), pltpu.VMEM((tn, K), BF16),
                          pltpu.SemaphoreType.DMA((1,))],
                 name=name)(a, w_t)


def _mod_body(c_ref, w_ref, b_ref, o_ref):
    c = c_ref[...]
    c_act = (c * jax.nn.sigmoid(c)).astype(BF16)
    o_ref[...] = jnp.dot(c_act, w_ref[...].astype(BF16), preferred_element_type=F32) + b_ref[...]


def _modulation(c, w_ada, b_ada):
    L, D, N = w_ada.shape
    B = c.shape[0]
    rows = 8
    c_pad = jnp.zeros((rows, D), F32).at[:B].set(c)
    tn = _tile(N, 1024)
    out = _call(
        _mod_body, grid=(L, N // tn),
        in_specs=[pl.BlockSpec((rows, D), lambda l, n: (0, 0)),
                  pl.BlockSpec((None, D, tn), lambda l, n: (l, 0, n)),
                  pl.BlockSpec((None, 1, tn), lambda l, n: (l, 0, n))],
        out_specs=pl.BlockSpec((None, rows, tn), lambda l, n: (l, 0, n)),
        out_shape=jax.ShapeDtypeStruct((L, rows, N), F32),
        name="adaln_mod")(c_pad, w_ada, b_ada.reshape(L, 1, N))
    return out[:, :B, :]


def _latent_up_body(lat_ref, wqm_ref, wqr_ref, wkv_ref, cos_ref, sin_ref, q_ref, kv_ref, wkvb_ref):
    @pl.when(pl.program_id(0) == 0)
    def _():
        wkvb_ref[...] = wkv_ref[...].astype(BF16)

    cq = lat_ref[:, :Q_RANK]
    main = jnp.dot(cq, wqm_ref[...], preferred_element_type=F32)
    rot = jnp.dot(cq, wqr_ref[...], preferred_element_type=F32)
    cos, sin = cos_ref[...], sin_ref[...]
    for h in range(MLA_HEADS):
        c0 = h * QPAD
        q_ref[:, c0:c0 + LANE] = (main[:, c0:c0 + LANE] * MLA_SCALE).astype(q_ref.dtype)
        q_ref[:, c0 + LANE:c0 + QPAD] = (main[:, c0 + LANE:c0 + QPAD] * cos
                                         + rot[:, h * LANE:(h + 1) * LANE] * sin).astype(q_ref.dtype)
    kv_ref[...] = jnp.dot(lat_ref[:, Q_RANK:], wkvb_ref[...],
                          preferred_element_type=F32).astype(kv_ref.dtype)


def _latent_up(lat, wq_main, wq_rot, w_ukv, layer, cos_q, sin_q, tm):
    M = lat.shape[0]
    nq, nr, nkv = wq_main.shape[-1], wq_rot.shape[-1], w_ukv.shape[-1]
    whole = lambda k, n: pl.BlockSpec((None, k, n), lambda m: (layer, 0, 0))
    rows = lambda n: pl.BlockSpec((tm, n), lambda m: (m, 0))
    return _call(
        _latent_up_body, grid=(M // tm,),
        in_specs=[rows(Q_RANK + KV_RANK), whole(Q_RANK, nq), whole(Q_RANK, nr),
                  whole(KV_RANK, nkv), rows(LANE), rows(LANE)],
        out_specs=[rows(nq), rows(nkv)],
        out_shape=[jax.ShapeDtypeStruct((M, nq), BF16), jax.ShapeDtypeStruct((M, nkv), BF16)],
        scratch=[pltpu.VMEM((KV_RANK, nkv), BF16)],
        name="latent_up")(lat, wq_main, wq_rot, w_ukv, cos_q, sin_q)


def _prenorm_body(x_ref, g_ref, sc_ref, sh_ref, o_ref):
    o_ref[...] = _norm_mod(x_ref[...], g_ref[...], sc_ref[...], sh_ref[...]).astype(o_ref.dtype)


def _prenorm(x, gain, sc, sh, S, out_dtype):
    M, D = x.shape
    tm = _tile(S, 512)
    bpt = S // tm
    vec = pl.BlockSpec((None, 1, D), lambda m: (m // bpt, 0, 0))
    return _call(
        _prenorm_body, grid=(M // tm,),
        in_specs=[pl.BlockSpec((tm, D), lambda m: (m, 0)),
                  pl.BlockSpec((1, D), lambda m: (0, 0)), vec, vec],
        out_specs=pl.BlockSpec((tm, D), lambda m: (m, 0)),
        out_shape=jax.ShapeDtypeStruct((M, D), out_dtype),
        name="prenorm")(x, gain, sc, sh)


SUB = 256
ATT_HEADS = 2


def _qk(q, k):
    return lax.dot_general(q, k, (((1,), (1,)), ((), ())), preferred_element_type=F32)


def _sb_chains(ops, tri, states, masks):
    zs = [_qk(q, k) for q, k, v in ops]
    sps = []
    for z, mask in zip(zs, masks):
        sp = jnp.maximum(z, 0.0) + jnp.log2(1.0 + jnp.exp2(jnp.minimum(z, -z)))
        if mask is not None:
            sp = jnp.where(mask, sp, 0.0)
        sps.append(sp.astype(BF16))
    csums, carries = [], []
    for spb, (carry, acc) in zip(sps, states):
        parts = []
        for c in reversed(range(spb.shape[1] // SUB)):
            loc = jnp.dot(spb[:, c * SUB:(c + 1) * SUB], tri, preferred_element_type=F32)
            parts.insert(0, loc + carry)
            carry = carry + loc[:, 0:1]
        csums.append(parts[0] if len(parts) == 1 else jnp.concatenate(parts, axis=1))
        carries.append(carry)
    out = []
    for z, csum, mask, carry, (q, k, v), (_, acc) in zip(zs, csums, masks, carries, ops, states):
        a = jnp.exp2(z - csum)
        if mask is not None:
            a = jnp.where(mask, a, 0.0)
        out.append((carry, acc + jnp.dot(a.astype(BF16), v, preferred_element_type=F32)))
    return tuple(out)


def _sb_body(q_ref, k_ref, v_ref, o_ref, *, t):
    i = pl.program_id(2)
    halves = t // SUB
    r = lax.broadcasted_iota(jnp.int32, (SUB, SUB), 0)
    c = lax.broadcasted_iota(jnp.int32, (SUB, SUB), 1)
    tri = jnp.where(r >= c, 1.0, 0.0).astype(BF16)
    start = pl.multiple_of(i * t, t)
    chains = [(g, a) for g in range(ATT_HEADS) for a in range(halves)]

    def operands(g, a, row0, nrows):
        cols = slice(g * HEAD_DIM, (g + 1) * HEAD_DIM)
        return (q_ref[a * SUB:(a + 1) * SUB, cols], k_ref[pl.ds(row0, nrows), cols],
                v_ref[pl.ds(row0, nrows), cols])

    ops, masks, st0 = [], [], []
    for g, a in chains:
        n = (a + 1) * SUB
        rr = lax.broadcasted_iota(jnp.int32, (SUB, n), 0)
        cc = lax.broadcasted_iota(jnp.int32, (SUB, n), 1)
        ops.append(operands(g, a, start, n))
        masks.append(cc < rr + a * SUB)
        st0.append((jnp.zeros((SUB, 1), F32), jnp.zeros((SUB, HEAD_DIM), F32)))
    state = _sb_chains(ops, tri, st0, masks)

    def step(jj, st):
        row0 = pl.multiple_of((i - 1 - jj) * t, t)
        return _sb_chains([operands(g, a, row0, t) for g, a in chains], tri, st,
                          [None] * len(chains))

    state = lax.fori_loop(0, i, step, state)
    for n, (g, a) in enumerate(chains):
        o_ref[a * SUB:(a + 1) * SUB, g * HEAD_DIM:(g + 1) * HEAD_DIM] = state[n][1].astype(o_ref.dtype)


def _sb_attention(qkv, B, S):
    t = _tile(S, 2 * SUB)
    nq = S // t
    G = ATT_HEADS
    HG = SB_HEADS // G
    W = G * HEAD_DIM
    return _call(
        functools.partial(_sb_body, t=t), grid=(B, HG, nq),
        in_specs=[pl.BlockSpec((t, W), lambda b, h, i: (b * nq + i, h)),
                  pl.BlockSpec((S, W), lambda b, h, i: (b, HG + h)),
                  pl.BlockSpec((S, W), lambda b, h, i: (b, 2 * HG + h))],
        out_specs=pl.BlockSpec((t, W), lambda b, h, i: (b * nq + i, h)),
        out_shape=jax.ShapeDtypeStruct((B * S, SB_WIDTH), BF16),
        name="sb_attention")(qkv, qkv, qkv)


def _mla_chains(ops, states, masks):
    ss = []
    for (q, k, v), mask in zip(ops, masks):
        s = _qk(q, k)
        if mask is not None:
            s = jnp.where(mask, s, -jnp.inf)
        ss.append(s)
    ms = [jnp.maximum(m, jnp.max(s, axis=-1, keepdims=True)) for s, (m, l, acc) in zip(ss, states)]
    ps = [jnp.exp2(s - m_new) for s, m_new in zip(ss, ms)]
    out = []
    for p, m_new, (q, k, v), (m, l, acc) in zip(ps, ms, ops, states):
        alpha = jnp.exp2(m - m_new)
        l = alpha * l + jnp.sum(p, axis=-1, keepdims=True)
        acc = alpha * acc + jnp.dot(p.astype(BF16), v, preferred_element_type=F32)
        out.append((m_new, l, acc))
    return tuple(out)


def _mla_body(q_ref, kv_ref, kpe_ref, o_ref, kf_ref, *, t):
    i = pl.program_id(2)
    halves = t // SUB
    hw = MLA_NOPE + MLA_V

    @pl.when(i == 0)
    def _():
        for g in range(ATT_HEADS):
            kf_ref[g, :, 0:LANE] = kv_ref[:, g * hw:g * hw + MLA_NOPE]
            kf_ref[g, :, LANE:QPAD] = kpe_ref[...]

    start = pl.multiple_of(i * t, t)
    chains = [(g, a) for g in range(ATT_HEADS) for a in range(halves)]

    def operands(g, a, row0, nrows):
        return (q_ref[a * SUB:(a + 1) * SUB, g * QPAD:(g + 1) * QPAD],
                kf_ref[g, pl.ds(row0, nrows), :],
                kv_ref[pl.ds(row0, nrows), g * hw + MLA_NOPE:(g + 1) * hw])

    ops, masks, st0 = [], [], []
    for g, a in chains:
        n = (a + 1) * SUB
        rr = lax.broadcasted_iota(jnp.int32, (SUB, n), 0)
        cc = lax.broadcasted_iota(jnp.int32, (SUB, n), 1)
        ops.append(operands(g, a, start, n))
        masks.append(cc <= rr + a * SUB)
        st0.append((jnp.full((SUB, 1), -jnp.inf, F32), jnp.zeros((SUB, 1), F32),
                    jnp.zeros((SUB, MLA_V), F32)))
    state = _mla_chains(ops, st0, masks)

    def step(j, st):
        row0 = pl.multiple_of(j * t, t)
        return _mla_chains([operands(g, a, row0, t) for g, a in chains], st, [None] * len(chains))

    state = lax.fori_loop(0, i, step, state)
    for n, (g, a) in enumerate(chains):
        m, l, acc = state[n]
        o_ref[a * SUB:(a + 1) * SUB, g * MLA_V:(g + 1) * MLA_V] = (acc / l).astype(o_ref.dtype)


def _mla_attention(q_full, kv, kpe, B, S):
    t = _tile(S, 2 * SUB)
    nq = S // t
    G = ATT_HEADS
    HG = MLA_HEADS // G
    return _call(
        functools.partial(_mla_body, t=t), grid=(B, HG, nq),
        in_specs=[pl.BlockSpec((t, G * QPAD), lambda b, h, i: (b * nq + i, h)),
                  pl.BlockSpec((S, G * (MLA_NOPE + MLA_V)), lambda b, h, i: (b, h)),
                  pl.BlockSpec((S, LANE), lambda b, h, i: (b, 0))],
        out_specs=pl.BlockSpec((t, G * MLA_V), lambda b, h, i: (b * nq + i, h)),
        out_shape=jax.ShapeDtypeStruct((B * S, MLA_HEADS * MLA_V), BF16),
        scratch=[pltpu.VMEM((G, S, QPAD), BF16)],
        name="mla_attention")(q_full, kv, kpe)


def _merge_body(osb_ref, omla_ref, gate_sb_ref, gate_mla_ref, wsb_ref, wmla_ref, wo_ref, x_ref,
                g1_ref, gain_ref, sc_ref, sh_ref, xo_ref, ho_ref):
    y = (gate_sb_ref[...].astype(F32)
         * jnp.dot(osb_ref[...], wsb_ref[...], preferred_element_type=F32)
         + gate_mla_ref[...].astype(F32)
         * jnp.dot(omla_ref[...], wmla_ref[...], preferred_element_type=F32))
    out = jnp.dot(y.astype(BF16), wo_ref[...], preferred_element_type=F32)
    x_new = x_ref[...] + g1_ref[...] * out
    xo_ref[...] = x_new
    ho_ref[...] = _norm_mod(x_new, gain_ref[...], sc_ref[...], sh_ref[...]).astype(ho_ref.dtype)


def _merge(o_sb, o_mla, gates, w_sb, w_mla, w_o, layer, x, g1, gain, sc, sh, S, h_dtype):
    M, D = x.shape
    tm = _tile(S, 256)
    bpt = S // tm
    vec = pl.BlockSpec((None, 1, D), lambda m: (m // bpt, 0, 0))
    res = lambda k: pl.BlockSpec((None, k, D), lambda m: (layer, 0, 0),
                                 pipeline_mode=pl.Buffered(1))
    return _call(
        _merge_body, grid=(M // tm,),
        in_specs=[pl.BlockSpec((tm, SB_WIDTH), lambda m: (m, 0)),
                  pl.BlockSpec((tm, MLA_HEADS * MLA_V), lambda m: (m, 0)),
                  pl.BlockSpec((tm, D), lambda m: (m, 0)),
                  pl.BlockSpec((tm, D), lambda m: (m, 1)),
                  res(SB_WIDTH), res(MLA_HEADS * MLA_V), res(D),
                  pl.BlockSpec((tm, D), lambda m: (m, 0)),
                  vec, pl.BlockSpec((1, D), lambda m: (0, 0)), vec, vec],
        out_specs=[pl.BlockSpec((tm, D), lambda m: (m, 0)),
                   pl.BlockSpec((tm, D), lambda m: (m, 0))],
        out_shape=[jax.ShapeDtypeStruct((M, D), F32), jax.ShapeDtypeStruct((M, D), h_dtype)],
        name="merge_out")(o_sb, o_mla, gates, gates, w_sb, w_mla, w_o, x, g1, gain, sc, sh)


def _resnorm_body(f_ref, x_ref, g_ref, gain_ref, sc_ref, sh_ref, xo_ref, ho_ref):
    x_new = x_ref[...] + g_ref[...] * f_ref[...]
    xo_ref[...] = x_new
    ho_ref[...] = _norm_mod(x_new, gain_ref[...], sc_ref[...], sh_ref[...]).astype(ho_ref.dtype)


def _resnorm(f, x, g, gain, sc, sh, S, h_dtype):
    M, D = x.shape
    tm = _tile(S, 512)
    bpt = S // tm
    vec = pl.BlockSpec((None, 1, D), lambda m: (m // bpt, 0, 0))
    rows = pl.BlockSpec((tm, D), lambda m: (m, 0))
    return _call(
        _resnorm_body, grid=(M // tm,),
        in_specs=[rows, rows, vec, pl.BlockSpec((1, D), lambda m: (0, 0)), vec, vec],
        out_specs=[rows, rows],
        out_shape=[jax.ShapeDtypeStruct((M, D), F32), jax.ShapeDtypeStruct((M, D), h_dtype)],
        name="resnorm")(f, x, g, gain, sc, sh)


def _split_bf16(v):
    hi = v.astype(BF16)
    return hi, (v - hi.astype(F32)).astype(BF16)


def _router_body(h_ref, w_ref, idx_ref, wt_ref):
    hh, hl = _split_bf16(h_ref[...])
    wh, wl = _split_bf16(w_ref[...])
    dot = functools.partial(jnp.dot, preferred_element_type=F32)
    logits = dot(hh, wh) + (dot(hl, wh) + dot(hh, wl))
    lane = lax.broadcasted_iota(jnp.int32, logits.shape, 1).astype(F32)
    logits = jnp.where(lane < N_EXPERTS, logits, -jnp.inf)
    m1 = jnp.max(logits, axis=-1, keepdims=True)
    i1 = jnp.min(jnp.where(logits == m1, lane, float(LANE)), axis=-1, keepdims=True)
    rest = jnp.where(lane == i1, -jnp.inf, logits)
    m2 = jnp.max(rest, axis=-1, keepdims=True)
    i2 = jnp.min(jnp.where(rest == m2, lane, float(LANE)), axis=-1, keepdims=True)
    e = jnp.exp(m2 - m1)
    w1 = 1.0 / (1.0 + e)
    w2 = e * w1
    idx_ref[...] = jnp.where(lane == 0.0, i1, jnp.where(lane == 1.0, i2, 0.0)).astype(jnp.int32)
    wt_ref[...] = jnp.where(lane == 0.0, w1, jnp.where(lane == 1.0, w2, 0.0))


def _router(h, w_router_pad, j):
    M, D = h.shape
    tm = _tile(M, 512)
    return _call(
        _router_body, grid=(M // tm,),
        in_specs=[pl.BlockSpec((tm, D), lambda m: (m, 0)),
                  pl.BlockSpec((None, D, LANE), lambda m: (j, 0, 0))],
        out_specs=[pl.BlockSpec((tm, LANE), lambda m: (m, 0)),
                   pl.BlockSpec((tm, LANE), lambda m: (m, 0))],
        out_shape=[jax.ShapeDtypeStruct((M, LANE), jnp.int32),
                   jax.ShapeDtypeStruct((M, LANE), F32)],
        name="router")(h, w_router_pad)


MOE_SUB = 256
ROW_UNROLL = 8


def _gather_body(src_ref, h_hbm, o_ref, buf_ref, sem, *, tm):
    t = pl.program_id(0)

    def issue(tile):
        slot = tile % 2

        def blk(b, _):
            for u in range(ROW_UNROLL):
                r = b * ROW_UNROLL + u
                pltpu.make_async_copy(h_hbm.at[pl.ds(src_ref[tile * tm + r], 1)],
                                      buf_ref.at[slot, pl.ds(r, 1)],
                                      sem.at[slot]).start(priority=u % 2)
            return 0

        lax.fori_loop(0, tm // ROW_UNROLL, blk, 0)

    @pl.when(t == 0)
    def _():
        issue(t)

    @pl.when(t + 1 < pl.num_programs(0))
    def _():
        issue(t + 1)

    slot = t % 2
    pltpu.make_async_copy(h_hbm.at[pl.ds(0, tm)], buf_ref.at[slot], sem.at[slot]).wait()
    o_ref[...] = buf_ref[slot].astype(o_ref.dtype)


def _gather_rows(src_tok, h, n_tiles, tm):
    M, D = h.shape
    return _call(
        functools.partial(_gather_body, tm=tm), grid=(n_tiles,), nsp=1,
        in_specs=[pl.BlockSpec(memory_space=pl.ANY)],
        out_specs=pl.BlockSpec((tm, D), lambda t, src: (t, 0)),
        out_shape=jax.ShapeDtypeStruct((n_tiles * tm, D), BF16),
        scratch=[pltpu.VMEM((2, tm, D), F32), pltpu.SemaphoreType.DMA((2,))],
        name="moe_gather")(src_tok, h)


def _grouped_body(te_ref, a_ref, *refs, n_w, layer, epilogue):
    w_hbm = refs[:n_w]
    n_rest = len(refs) - 3 * n_w - 1
    rest = refs[n_w:n_w + n_rest]
    stage = refs[n_w + n_rest:2 * n_w + n_rest]
    wb_refs = refs[2 * n_w + n_rest:3 * n_w + n_rest]
    sem = refs[-1]
    c, t = pl.program_id(0), pl.program_id(1)
    nc, nt = pl.num_programs(0), pl.num_programs(1)
    expert = te_ref[t]
    n_valid = te_ref[nt + t]
    nxt = te_ref[2 * nt + t]
    changed = jnp.logical_or(t == 0, expert != te_ref[jnp.maximum(t - 1, 0)])
    tn = stage[0].shape[1]

    def block_copy(i, chunk, e):
        col0 = pl.multiple_of(chunk * tn, tn)
        return pltpu.make_async_copy(w_hbm[i].at[layer, e, :, pl.ds(col0, tn)], stage[i],
                                     sem.at[i])

    @pl.when(changed)
    def _():
        @pl.when(jnp.logical_and(c == 0, t == 0))
        def _():
            for i in range(n_w):
                block_copy(i, c, expert).start()

        for i in range(n_w):
            block_copy(i, c, expert).wait()
            wb_refs[i][...] = stage[i][...].astype(BF16)

        @pl.when(nxt >= 0)
        def _():
            for i in range(n_w):
                block_copy(i, c, nxt).start()

        @pl.when(jnp.logical_and(nxt < 0, c + 1 < nc))
        def _():
            for i in range(n_w):
                block_copy(i, c + 1, te_ref[0]).start()

    for s in range(a_ref.shape[0] // MOE_SUB):
        rows = slice(s * MOE_SUB, (s + 1) * MOE_SUB)

        @pl.when(s * MOE_SUB < n_valid)
        def _():
            a = a_ref[rows, :]
            accs = [jnp.dot(a, wb_ref[...], preferred_element_type=F32) for wb_ref in wb_refs]
            epilogue(accs, rest[:n_rest], rows)

        @pl.when(s * MOE_SUB >= n_valid)
        def _():
            for o_ref in rest[:n_rest]:
                o_ref[rows, :] = jnp.zeros((MOE_SUB, o_ref.shape[1]), o_ref.dtype)


def _up_epilogue(accs, refs, rows=slice(None)):
    g, u = accs
    refs[0][rows, :] = (g * jax.nn.sigmoid(g) * u).astype(refs[0].dtype)


def _grouped_up(tile_e, xs, w_gate, w_up, j, tm, tf):
    NP, D = xs.shape
    F = w_gate.shape[-1]
    hbm = pl.BlockSpec(memory_space=pl.ANY)
    return _call(
        functools.partial(_grouped_body, n_w=2, layer=j, epilogue=_up_epilogue),
        grid=(F // tf, NP // tm), nsp=1,
        in_specs=[pl.BlockSpec((tm, D), lambda f, t, te: (t, 0)), hbm, hbm],
        out_specs=pl.BlockSpec((tm, tf), lambda f, t, te: (t, f)),
        out_shape=jax.ShapeDtypeStruct((NP, F), BF16),
        scratch=[pltpu.VMEM((D, tf), F32), pltpu.VMEM((D, tf), F32),
                 pltpu.VMEM((D, tf), BF16), pltpu.VMEM((D, tf), BF16),
                 pltpu.SemaphoreType.DMA((2,))],
        name="moe_up")(tile_e, xs, w_gate, w_up)


def _down_epilogue(accs, refs, rows):
    refs[0][rows, :] = accs[0]


def _grouped_down(tile_e, hs, w_down, j, tm, tn):
    NP, F = hs.shape
    D = w_down.shape[-1]
    return _call(
        functools.partial(_grouped_body, n_w=1, layer=j, epilogue=_down_epilogue),
        grid=(D // tn, NP // tm), nsp=1,
        in_specs=[pl.BlockSpec((tm, F), lambda n, t, te: (t, 0)),
                  pl.BlockSpec(memory_space=pl.ANY)],
        out_specs=pl.BlockSpec((tm, tn), lambda n, t, te: (t, n)),
        out_shape=jax.ShapeDtypeStruct((NP, D), F32),
        scratch=[pltpu.VMEM((F, tn), F32), pltpu.VMEM((F, tn), BF16),
                 pltpu.SemaphoreType.DMA((1,))],
        name="moe_down")(tile_e, hs, w_down)


def _combine_body(pos_ref, y_hbm, wt_ref, x_ref, g2_ref, gain_ref, sc_ref, sh_ref, xo_ref, ho_ref,
                  buf_ref, sem, *, tm):
    t = pl.program_id(0)

    def issue(tile):
        slot = tile % 2

        def blk(b, _):
            for u in range(ROW_UNROLL):
                r = b * ROW_UNROLL + u
                for k in range(TOP_K):
                    pltpu.make_async_copy(
                        y_hbm.at[pl.ds(pos_ref[(tile * tm + r) * TOP_K + k], 1)],
                        buf_ref.at[slot, k, pl.ds(r, 1)], sem.at[slot]).start(priority=k % 2)
            return 0

        lax.fori_loop(0, tm // ROW_UNROLL, blk, 0)

    @pl.when(t == 0)
    def _():
        issue(t)

    @pl.when(t + 1 < pl.num_programs(0))
    def _():
        issue(t + 1)

    slot = t % 2
    for k in range(TOP_K):
        pltpu.make_async_copy(y_hbm.at[pl.ds(0, tm)], buf_ref.at[slot, k], sem.at[slot]).wait()
    wt = wt_ref[...]
    f = buf_ref[slot, 0] * wt[:, 0:1]
    for k in range(1, TOP_K):
        f = f + buf_ref[slot, k] * wt[:, k:k + 1]
    x_new = x_ref[...] + g2_ref[...] * f
    xo_ref[...] = x_new
    ho_ref[...] = _norm_mod(x_new, gain_ref[...], sc_ref[...], sh_ref[...]).astype(ho_ref.dtype)


def _combine(pos, y_sorted, wts, x, g2, gain, sc, sh, S, h_dtype):
    M, D = x.shape
    tm = _tile(S, 256)
    bpt = S // tm
    vec = pl.BlockSpec((None, 1, D), lambda m, p: (m // bpt, 0, 0))
    return _call(
        functools.partial(_combine_body, tm=tm), grid=(M // tm,), nsp=1,
        in_specs=[pl.BlockSpec(memory_space=pl.ANY),
                  pl.BlockSpec((tm, LANE), lambda m, p: (m, 0)),
                  pl.BlockSpec((tm, D), lambda m, p: (m, 0)),
                  vec, pl.BlockSpec((1, D), lambda m, p: (0, 0)), vec, vec],
        out_specs=[pl.BlockSpec((tm, D), lambda m, p: (m, 0)),
                   pl.BlockSpec((tm, D), lambda m, p: (m, 0))],
        out_shape=[jax.ShapeDtypeStruct((M, D), F32), jax.ShapeDtypeStruct((M, D), h_dtype)],
        scratch=[pltpu.VMEM((2, TOP_K, tm, D), F32), pltpu.SemaphoreType.DMA((2,))],
        name="moe_combine")(pos, y_sorted, wts, x, g2, gain, sc, sh)


def _dispatch_plan(top_i, tm):
    M = top_i.shape[0]
    E = N_EXPERTS
    n_ent = M * TOP_K
    e_flat = top_i.reshape(-1)
    onehot = (e_flat[:, None] == jnp.arange(E, dtype=jnp.int32)[None, :]).astype(jnp.int32)
    csum = jnp.cumsum(onehot, axis=0)
    rank = jnp.sum(onehot * csum, axis=1) - 1
    counts = csum[-1]
    pcounts = ((counts + tm - 1) // tm) * tm
    pend = jnp.cumsum(pcounts)
    pstart = pend - pcounts
    dest = (pstart[e_flat] + rank).astype(jnp.int32)
    n_tiles = n_ent // tm + E
    n_rows = n_tiles * tm
    src_tok = jnp.zeros((n_rows,), jnp.int32).at[dest].set(
        jnp.arange(n_ent, dtype=jnp.int32) // TOP_K)
    tile_start = jnp.arange(n_tiles, dtype=jnp.int32) * tm
    tile_e = jnp.sum((tile_start[:, None] >= pend[None, :]).astype(jnp.int32), axis=1)
    tile_e = jnp.minimum(tile_e, E - 1).astype(jnp.int32)
    n_valid = jnp.clip(pstart[tile_e] + counts[tile_e] - tile_start, 0, tm).astype(jnp.int32)
    n_active = pend[-1] // tm
    tile_e = jnp.where(tile_start < pend[-1], tile_e, tile_e[n_active - 1])
    tidx = jnp.arange(n_tiles, dtype=jnp.int32)
    later_other = (tidx[None, :] > tidx[:, None]) & (tile_e[None, :] != tile_e[:, None])
    nxt = jnp.where(jnp.any(later_other, axis=1), tile_e[jnp.argmax(later_other, axis=1)], -1)
    return src_tok, dest, jnp.concatenate([tile_e, n_valid, nxt]).astype(jnp.int32), n_tiles


def _rot_cols(w):
    half = MLA_ROPE // 2
    return jnp.concatenate([-w[..., half:], w[..., :half]], axis=-1)


def _rope_tables(positions):
    inv_freq = 1.0 / (ROPE_THETA ** (jnp.arange(0, MLA_ROPE, 2, dtype=F32) / MLA_ROPE))
    ang = positions.astype(F32)[..., None] * inv_freq
    cos, sin = jnp.cos(ang), jnp.sin(ang)
    M = positions.size
    cos2 = jnp.concatenate([cos, cos], axis=-1).reshape(M, MLA_ROPE)
    sin2 = jnp.concatenate([sin, sin], axis=-1).reshape(M, MLA_ROPE)
    zeros = jnp.zeros((M, LANE - MLA_ROPE), F32)
    cos_k = jnp.concatenate([cos2, zeros], axis=-1)
    sin_k = jnp.concatenate([sin2, zeros], axis=-1)
    return cos_k, sin_k, cos_k * MLA_SCALE, sin_k * MLA_SCALE


def kernel(x, c, positions, w_ada, b_ada, norm_mix_g, norm_ffn_g, w_in, q_norm_g, kv_norm_g, w_uq, w_ukv, w_sb_up, w_mla_up, w_o, w_ffn_gate, w_ffn_up, w_ffn_down, w_router, w_exp_gate, w_exp_up, w_exp_down, final_norm_g):
    B, S, D = x.shape
    L = w_ada.shape[0]
    M = B * S
    x = x.reshape(M, D)

    mod = _modulation(c, w_ada, b_ada)
    mod = mod.reshape(L, B, N_MOD, 1, D)
    sh1, sc1, g1, sh2, sc2, g2 = [mod[:, :, k] for k in range(N_MOD)]
    cos_k, sin_k, cos_q, sin_q = _rope_tables(positions)

    pe0 = 3 * SB_WIDTH + Q_RANK + KV_RANK
    w_in_t = jnp.swapaxes(w_in, 1, 2)
    w_pe = w_in_t[:, pe0:pe0 + MLA_ROPE]
    zpad = jnp.zeros((L, LANE - MLA_ROPE, D), F32)
    half = MLA_ROPE // 2
    w_pe_rot = jnp.concatenate([-w_pe[:, half:], w_pe[:, :half]], axis=1)
    w_pe2 = jnp.concatenate([w_pe, zpad, w_pe_rot, zpad], axis=1)
    c_gain = jnp.concatenate([q_norm_g, kv_norm_g], axis=-1)[:, None, :]

    wq = w_uq.reshape(L, Q_RANK, MLA_HEADS, MLA_QK)
    wq_n, wq_r = wq[..., :MLA_NOPE], wq[..., MLA_NOPE:]
    zq = jnp.zeros(wq_r.shape, F32)
    wq_main = jnp.concatenate([wq_n, wq_r, zq], axis=-1).reshape(
        L, Q_RANK, MLA_HEADS * QPAD).astype(BF16)
    wq_rot = jnp.concatenate([_rot_cols(wq_r), zq], axis=-1).reshape(
        L, Q_RANK, MLA_HEADS * LANE).astype(BF16)

    w_sb_b, w_mla_b, w_o_b = (w.astype(BF16) for w in (w_sb_up, w_mla_up, w_o))
    w_router_pad = jnp.concatenate(
        [w_router, jnp.zeros(w_router.shape[:2] + (LANE - N_EXPERTS,), F32)], axis=-1)

    tm = _tile(S, 1024)
    bpt = S // tm
    row = lambda width: pl.BlockSpec((tm, width), lambda n, m: (m, 0))
    sb_scale = HEAD_DIM ** -0.5 * LOG2E

    def h_dtype_for(layer_is_moe):
        return F32 if layer_is_moe else BF16

    h = _prenorm(x, norm_mix_g[0][None, :], sc1[0], sh1[0], S, BF16)

    for l in range(L):
        tn = _tile(min(SB_WIDTH, 2 * D), 1024)
        nq_blk = SB_WIDTH // tn

        def qkv_epi(accs, n, extra, outs):
            s = jnp.where(n < nq_blk, sb_scale, 1.0)
            outs[0][...] = (accs[0] * s).astype(BF16)

        (qkv,) = _matmul(h, [w_in_t], layer=l, col_blk0=0, n_cols=3 * SB_WIDTH, tn=tn, tm=tm,
                         epilogue=qkv_epi, transposed=True,
                         out_shape=[jax.ShapeDtypeStruct((M, 3 * SB_WIDTH), BF16)],
                         out_specs=[pl.BlockSpec((tm, tn), lambda n, m: (m, n))], name="in_qkv")

        def lat_epi(accs, n, extra, outs):
            a = accs[0]
            ms = jnp.mean(a * a, axis=-1, keepdims=True)
            outs[0][...] = (a * lax.rsqrt(ms + NORM_EPS) * extra[0][...]).astype(BF16)

        (lat,) = _matmul(h, [w_in_t], layer=l, col_blk0=3 * SB_WIDTH // Q_RANK, transposed=True,
                         n_cols=Q_RANK + KV_RANK, tn=Q_RANK, tm=tm, epilogue=lat_epi,
                         extras=[c_gain],
                         extra_specs=[pl.BlockSpec((None, 1, Q_RANK), lambda n, m: (l, 0, n))],
                         out_shape=[jax.ShapeDtypeStruct((M, Q_RANK + KV_RANK), BF16)],
                         out_specs=[pl.BlockSpec((tm, Q_RANK), lambda n, m: (m, n))],
                         name="in_latent")

        def pe_epi(accs, n, extra, outs):
            a = accs[0]
            outs[0][...] = (a[:, :LANE] * extra[0][...] + a[:, LANE:] * extra[1][...]).astype(BF16)

        (kpe,) = _matmul(h, [w_pe2], layer=l, col_blk0=0, n_cols=2 * LANE, tn=2 * LANE, tm=tm,
                         epilogue=pe_epi, extras=[cos_k, sin_k], transposed=True,
                         extra_specs=[row(LANE), row(LANE)],
                         out_shape=[jax.ShapeDtypeStruct((M, LANE), BF16)],
                         out_specs=[pl.BlockSpec((tm, LANE), lambda n, m: (m, 0))],
                         name="in_kpe")

        def gate_epi(accs, n, extra, outs):
            outs[0][...] = jax.nn.sigmoid(accs[0]).astype(BF16)

        gates = _matmul_rows(h, w_in_t, layer=l, row0=pe0 + MLA_ROPE, n_cols=2 * D, tn=tn, tm=tm,
                             epilogue=gate_epi, out_dtype=BF16, name="in_gates")

        q_full, kv = _latent_up(lat, wq_main, wq_rot, w_ukv, l, cos_q, sin_q, tm)

        o_sb = _sb_attention(qkv, B, S)
        o_mla = _mla_attention(q_full, kv, kpe, B, S)

        moe = (l % 2 == 1)
        x, h = _merge(o_sb, o_mla, gates, w_sb_b, w_mla_b, w_o_b, l, x, g1[l],
                      norm_ffn_g[l][None, :], sc2[l], sh2[l], S, h_dtype_for(moe))

        last = (l == L - 1)
        if last:
            n_gain = final_norm_g[None, :]
            n_sc = jnp.zeros_like(sc1[0])
            n_sh = jnp.zeros_like(sh1[0])
            n_dtype = F32
        else:
            n_gain, n_sc, n_sh, n_dtype = norm_mix_g[l + 1][None, :], sc1[l + 1], sh1[l + 1], BF16
        j = l // 2
        if not moe:
            F = w_ffn_gate.shape[-1]
            tf = _tile(F, 512)

            def up_epi(accs, n, extra, outs):
                _up_epilogue(accs, outs)

            (hmid,) = _matmul(h, [w_ffn_gate, w_ffn_up], layer=j, col_blk0=0, n_cols=F, tn=tf,
                              tm=tm, epilogue=up_epi,
                              out_shape=[jax.ShapeDtypeStruct((M, F), BF16)],
                              out_specs=[pl.BlockSpec((tm, tf), lambda n, m: (m, n))],
                              name="ffn_up")
            def down_epi(accs, n, extra, outs):
                outs[0][...] = accs[0]

            tnd = _tile(D, 512)
            (f,) = _matmul(hmid, [w_ffn_down], layer=j, col_blk0=0, n_cols=D, tn=tnd,
                           tm=_tile(S, 512), epilogue=down_epi,
                           out_shape=[jax.ShapeDtypeStruct((M, D), F32)],
                           out_specs=[pl.BlockSpec((_tile(S, 512), tnd), lambda n, m: (m, n))],
                           name="ffn_down")
            x, h = _resnorm(f, x, g2[l], n_gain, n_sc, n_sh, S, n_dtype)
        else:
            F = w_exp_gate.shape[-1]
            tme = max(_tile(M, 2 * MOE_SUB), MOE_SUB)
            idx, wts = _router(h, w_router_pad, j)
            src_tok, dest, tile_e, n_tiles = _dispatch_plan(idx[:, :TOP_K], tme)
            xs = _gather_rows(src_tok, h, n_tiles, tme)
            tfe = F // 4 if F % (4 * LANE) == 0 else _tile(F, 512)
            hs = _grouped_up(tile_e, xs, w_exp_gate, w_exp_up, j, tme, tfe)
            ys = _grouped_down(tile_e, hs, w_exp_down, j, tme, _tile(D, 512))
            x, h = _combine(dest, ys, wts, x, g2[l], n_gain, n_sc, n_sh, S, n_dtype)

    return h.reshape(B, S, D)
```

```python
import functools

import jax
import jax.numpy as jnp
from jax import lax
from jax.experimental import pallas as pl
from jax.experimental.pallas import tpu as pltpu

BF16 = jnp.bfloat16
F32 = jnp.float32

SB_HEADS = 8
HEAD_DIM = 128
SB_WIDTH = SB_HEADS * HEAD_DIM
MLA_HEADS = 8
MLA_NOPE = 128
MLA_ROPE = 64
MLA_V = 128
MLA_QK = MLA_NOPE + MLA_ROPE
Q_RANK = 512
KV_RANK = 512
ROPE_THETA = 10000.0
N_EXPERTS = 8
TOP_K = 2
N_MOD = 6
NORM_EPS = 1e-6

LOG2E = 1.4426950408889634
MLA_SCALE = MLA_QK ** -0.5 * LOG2E

LANE = 128
QPAD = 2 * LANE
VMEM_LIMIT = 56 << 20


def _call(body, *, grid, in_specs, out_specs, out_shape, scratch=(), nsp=0, name, flags=None):
    return pl.pallas_call(
        body,
        grid_spec=pltpu.PrefetchScalarGridSpec(
            num_scalar_prefetch=nsp, grid=grid, in_specs=in_specs,
            out_specs=out_specs, scratch_shapes=scratch),
        out_shape=out_shape,
        compiler_params=pltpu.CompilerParams(
            dimension_semantics=("arbitrary",) * len(grid),
            vmem_limit_bytes=VMEM_LIMIT, flags=flags),
        name=name)


def _tile(n, pref):
    t = min(n, pref)
    while n % t:
        t //= 2
    return t


def _norm_mod(x, gain, sc, sh):
    ms = jnp.mean(x * x, axis=-1, keepdims=True)
    y = x * lax.rsqrt(ms + NORM_EPS)
    return (y * gain) * (1.0 + sc) + sh


def _mm_body(*refs, n_w, n_extra, n_out, epilogue, transposed):
    a_ref = refs[0]
    w_refs = refs[1:1 + n_w]
    extra = refs[1 + n_w:1 + n_w + n_extra]
    outs = refs[1 + n_w + n_extra:1 + n_w + n_extra + n_out]
    wb_refs = refs[1 + n_w + n_extra + n_out:]

    @pl.when(pl.program_id(1) == 0)
    def _():
        for w_ref, wb_ref in zip(w_refs, wb_refs):
            wb_ref[...] = w_ref[...].astype(BF16)

    a = a_ref[...]
    dims = (((1,), (1,)), ((), ())) if transposed else (((1,), (0,)), ((), ()))
    accs = [lax.dot_general(a, wb_ref[...], dims, preferred_element_type=F32)
            for wb_ref in wb_refs]
    epilogue(accs, pl.program_id(0), extra, outs)


def _matmul(a, ws, *, layer, col_blk0, n_cols, tn, tm, epilogue, extras=(), extra_specs=(),
            out_shape, out_specs, name, a_blk=0, transposed=False):
    M = a.shape[0]
    K = ws[0].shape[2 if transposed else 1]
    grid = (n_cols // tn, M // tm)
    in_specs = [pl.BlockSpec((tm, K), lambda n, m: (m, a_blk))]
    for _ in ws:
        if transposed:
            in_specs.append(pl.BlockSpec((None, tn, K), lambda n, m: (layer, n + col_blk0, 0)))
        else:
            in_specs.append(pl.BlockSpec((None, K, tn), lambda n, m: (layer, 0, n + col_blk0)))
    in_specs += list(extra_specs)
    body = functools.partial(_mm_body, n_w=len(ws), n_extra=len(extras), n_out=len(out_shape),
                             epilogue=epilogue, transposed=transposed)
    wb_shape = (tn, K) if transposed else (K, tn)
    return _call(body, grid=grid, in_specs=in_specs, out_specs=out_specs, out_shape=out_shape,
                 scratch=[pltpu.VMEM(wb_shape, BF16) for _ in ws], name=name)(a, *ws, *extras)


def _mm_rows_body(a_ref, w_hbm, o_ref, stage_ref, wb_ref, sem, *, layer, row0, epilogue):
    n, m = pl.program_id(0), pl.program_id(1)
    tn = stage_ref.shape[0]

    def chunk_copy(chunk):
        r0 = pl.multiple_of(row0 + chunk * tn, 8)
        return pltpu.make_async_copy(w_hbm.at[layer, pl.ds(r0, tn), :], stage_ref, sem.at[0])

    @pl.when(m == 0)
    def _():
        @pl.when(n == 0)
        def _():
            chunk_copy(n).start()

        chunk_copy(n).wait()
        wb_ref[...] = stage_ref[...].astype(BF16)

        @pl.when(n + 1 < pl.num_programs(0))
        def _():
            chunk_copy(n + 1).start()

    acc = lax.dot_general(a_ref[...], wb_ref[...], (((1,), (1,)), ((), ())),
                          preferred_element_type=F32)
    epilogue([acc], n, (), [o_ref])


def _matmul_rows(a, w_t, *, layer, row0, n_cols, tn, tm, epilogue, out_dtype, name):
    M, K = a.shape
    body = functools.partial(_mm_rows_body, layer=layer, row0=row0, epilogue=epilogue)
    return _call(body, grid=(n_cols // tn, M // tm),
                 in_specs=[pl.BlockSpec((tm, K), lambda n, m: (m, 0)),
                           pl.BlockSpec(memory_space=pl.ANY)],
                 out_specs=pl.BlockSpec((tm, tn), lambda n, m: (m, n)),
                 out_shape=jax.ShapeDtypeStruct((M, n_cols), out_dtype),
                 scratch=[pltpu.VMEM((tn, K), F32), pltpu.VMEM((tn, K), BF16),
                          pltpu.SemaphoreType.DMA((1,))],
                 name=name)(a, w_t)


def _mod_body(c_ref, w_ref, b_ref, o_ref):
    c = c_ref[...]
    c_act = (c * jax.nn.sigmoid(c)).astype(BF16)
    o_ref[...] = jnp.dot(c_act, w_ref[...].astype(BF16), preferred_element_type=F32) + b_ref[...]


def _modulation(c, w_ada, b_ada):
    L, D, N = w_ada.shape
    B = c.shape[0]
    rows = 8
    c_pad = jnp.zeros((rows, D), F32).at[:B].set(c)
    tn = _tile(N, 1024)
    out = _call(
        _mod_body, grid=(L, N // tn),
        in_specs=[pl.BlockSpec((rows, D), lambda l, n: (0, 0)),
                  pl.BlockSpec((None, D, tn), lambda l, n: (l, 0, n)),
                  pl.BlockSpec((None, 1, tn), lambda l, n: (l, 0, n))],
        out_specs=pl.BlockSpec((None, rows, tn), lambda l, n: (l, 0, n)),
        out_shape=jax.ShapeDtypeStruct((L, rows, N), F32),
        name="adaln_mod")(c_pad, w_ada, b_ada.reshape(L, 1, N))
    return out[:, :B, :]


def _latent_in_body(h_ref, wl_ref, wpe_ref, gain_ref, cos_ref, sin_ref, lat_ref, kpe_ref,
                    wlb_ref, wpeb_ref):
    @pl.when(pl.program_id(0) == 0)
    def _():
        wlb_ref[...] = wl_ref[...].astype(BF16)
        wpeb_ref[...] = wpe_ref[...].astype(BF16)

    h = h_ref[...]
    nt = (((1,), (1,)), ((), ()))
    c = lax.dot_general(h, wlb_ref[...], nt, preferred_element_type=F32)
    for k0 in range(0, c.shape[1], Q_RANK):
        a = c[:, k0:k0 + Q_RANK]
        ms = jnp.mean(a * a, axis=-1, keepdims=True)
        lat_ref[:, k0:k0 + Q_RANK] = (a * lax.rsqrt(ms + NORM_EPS)
                                      * gain_ref[:, k0:k0 + Q_RANK]).astype(lat_ref.dtype)
    p = lax.dot_general(h, wpeb_ref[...], nt, preferred_element_type=F32)
    kpe_ref[...] = (p[:, :LANE] * cos_ref[...] + p[:, LANE:] * sin_ref[...]).astype(kpe_ref.dtype)


def _latent_in(h, w_in_t, w_pe2, c_gain, cos_k, sin_k, layer, tm):
    M, D = h.shape
    n_lat = Q_RANK + KV_RANK
    lat_blk = 3 * SB_WIDTH // n_lat
    rows = lambda n: pl.BlockSpec((tm, n), lambda m: (m, 0))
    return _call(
        _latent_in_body, grid=(M // tm,),
        in_specs=[rows(D),
                  pl.BlockSpec((None, n_lat, D), lambda m: (layer, lat_blk, 0)),
                  pl.BlockSpec((None, 2 * LANE, D), lambda m: (layer, 0, 0)),
                  pl.BlockSpec((None, 1, n_lat), lambda m: (layer, 0, 0)),
                  rows(LANE), rows(LANE)],
        out_specs=[rows(n_lat), rows(LANE)],
        out_shape=[jax.ShapeDtypeStruct((M, n_lat), BF16), jax.ShapeDtypeStruct((M, LANE), BF16)],
        scratch=[pltpu.VMEM((n_lat, D), BF16), pltpu.VMEM((2 * LANE, D), BF16)],
        name="latent_in")(h, w_in_t, w_pe2, c_gain, cos_k, sin_k)


def _latent_up_body(lat_ref, wqm_ref, wqr_ref, wkv_ref, cos_ref, sin_ref, q_ref, kv_ref, wkvb_ref):
    @pl.when(pl.program_id(0) == 0)
    def _():
        wkvb_ref[...] = wkv_ref[...].astype(BF16)

    cq = lat_ref[:, :Q_RANK]
    main = jnp.dot(cq, wqm_ref[...], preferred_element_type=F32)
    rot = jnp.dot(cq, wqr_ref[...], preferred_element_type=F32)
    cos, sin = cos_ref[...], sin_ref[...]
    for h in range(MLA_HEADS):
        c0 = h * QPAD
        q_ref[:, c0:c0 + LANE] = (main[:, c0:c0 + LANE] * MLA_SCALE).astype(q_ref.dtype)
        q_ref[:, c0 + LANE:c0 + QPAD] = (main[:, c0 + LANE:c0 + QPAD] * cos
                                         + rot[:, h * LANE:(h + 1) * LANE] * sin).astype(q_ref.dtype)
    kv_ref[...] = jnp.dot(lat_ref[:, Q_RANK:], wkvb_ref[...],
                          preferred_element_type=F32).astype(kv_ref.dtype)


def _latent_up(lat, wq_main, wq_rot, w_ukv, layer, cos_q, sin_q, tm):
    M = lat.shape[0]
    nq, nr, nkv = wq_main.shape[-1], wq_rot.shape[-1], w_ukv.shape[-1]
    whole = lambda k, n: pl.BlockSpec((None, k, n), lambda m: (layer, 0, 0))
    rows = lambda n: pl.BlockSpec((tm, n), lambda m: (m, 0))
    return _call(
        _latent_up_body, grid=(M // tm,),
        in_specs=[rows(Q_RANK + KV_RANK), whole(Q_RANK, nq), whole(Q_RANK, nr),
                  whole(KV_RANK, nkv), rows(LANE), rows(LANE)],
        out_specs=[rows(nq), rows(nkv)],
        out_shape=[jax.ShapeDtypeStruct((M, nq), BF16), jax.ShapeDtypeStruct((M, nkv), BF16)],
        scratch=[pltpu.VMEM((KV_RANK, nkv), BF16)],
        name="latent_up")(lat, wq_main, wq_rot, w_ukv, cos_q, sin_q)


def _prenorm_body(x_ref, g_ref, sc_ref, sh_ref, o_ref):
    o_ref[...] = _norm_mod(x_ref[...], g_ref[...], sc_ref[...], sh_ref[...]).astype(o_ref.dtype)


def _prenorm(x, gain, sc, sh, S, out_dtype):
    M, D = x.shape
    tm = _tile(S, 512)
    bpt = S // tm
    vec = pl.BlockSpec((None, 1, D), lambda m: (m // bpt, 0, 0))
    return _call(
        _prenorm_body, grid=(M // tm,),
        in_specs=[pl.BlockSpec((tm, D), lambda m: (m, 0)),
                  pl.BlockSpec((1, D), lambda m: (0, 0)), vec, vec],
        out_specs=pl.BlockSpec((tm, D), lambda m: (m, 0)),
        out_shape=jax.ShapeDtypeStruct((M, D), out_dtype),
        name="prenorm")(x, gain, sc, sh)


SUB = 256
ATT_HEADS = 2


def _qk(q, k):
    return lax.dot_general(q, k, (((1,), (1,)), ((), ())), preferred_element_type=F32)


def _sb_chains(ops, tri, states, masks):
    zs = [_qk(q, k) for q, k, v in ops]
    sps = []
    for z, mask in zip(zs, masks):
        sp = jnp.maximum(z, 0.0) + jnp.log2(1.0 + jnp.exp2(jnp.minimum(z, -z)))
        if mask is not None:
            sp = jnp.where(mask, sp, 0.0)
        sps.append(sp.astype(BF16))
    csums, carries = [], []
    for spb, (carry, acc) in zip(sps, states):
        parts = []
        for c in reversed(range(spb.shape[1] // SUB)):
            loc = jnp.dot(spb[:, c * SUB:(c + 1) * SUB], tri, preferred_element_type=F32)
            parts.insert(0, loc + carry)
            carry = carry + loc[:, 0:1]
        csums.append(parts[0] if len(parts) == 1 else jnp.concatenate(parts, axis=1))
        carries.append(carry)
    out = []
    for z, csum, mask, carry, (q, k, v), (_, acc) in zip(zs, csums, masks, carries, ops, states):
        a = jnp.exp2(z - csum)
        if mask is not None:
            a = jnp.where(mask, a, 0.0)
        out.append((carry, acc + jnp.dot(a.astype(BF16), v, preferred_element_type=F32)))
    return tuple(out)


def _sb_body(q_ref, k_ref, v_ref, o_ref, *, t):
    i = pl.program_id(2)
    halves = t // SUB
    r = lax.broadcasted_iota(jnp.int32, (SUB, SUB), 0)
    c = lax.broadcasted_iota(jnp.int32, (SUB, SUB), 1)
    tri = jnp.where(r >= c, 1.0, 0.0).astype(BF16)
    start = pl.multiple_of(i * t, t)
    chains = [(g, a) for g in range(ATT_HEADS) for a in range(halves)]

    def operands(g, a, row0, nrows):
        cols = slice(g * HEAD_DIM, (g + 1) * HEAD_DIM)
        return (q_ref[a * SUB:(a + 1) * SUB, cols], k_ref[pl.ds(row0, nrows), cols],
                v_ref[pl.ds(row0, nrows), cols])

    ops, masks, st0 = [], [], []
    for g, a in chains:
        n = (a + 1) * SUB
        rr = lax.broadcasted_iota(jnp.int32, (SUB, n), 0)
        cc = lax.broadcasted_iota(jnp.int32, (SUB, n), 1)
        ops.append(operands(g, a, start, n))
        masks.append(cc < rr + a * SUB)
        st0.append((jnp.zeros((SUB, 1), F32), jnp.zeros((SUB, HEAD_DIM), F32)))
    state = _sb_chains(ops, tri, st0, masks)

    def step(jj, st):
        row0 = pl.multiple_of((i - 1 - jj) * t, t)
        return _sb_chains([operands(g, a, row0, t) for g, a in chains], tri, st,
                          [None] * len(chains))

    state = lax.fori_loop(0, i, step, state)
    for n, (g, a) in enumerate(chains):
        o_ref[a * SUB:(a + 1) * SUB, g * HEAD_DIM:(g + 1) * HEAD_DIM] = state[n][1].astype(o_ref.dtype)


def _sb_attention(qkv, B, S):
    t = _tile(S, 2 * SUB)
    nq = S // t
    G = ATT_HEADS
    HG = SB_HEADS // G
    W = G * HEAD_DIM
    return _call(
        functools.partial(_sb_body, t=t), grid=(B, HG, nq),
        in_specs=[pl.BlockSpec((t, W), lambda b, h, i: (b * nq + i, h)),
                  pl.BlockSpec((S, W), lambda b, h, i: (b, HG + h)),
                  pl.BlockSpec((S, W), lambda b, h, i: (b, 2 * HG + h))],
        out_specs=pl.BlockSpec((t, W), lambda b, h, i: (b * nq + i, h)),
        out_shape=jax.ShapeDtypeStruct((B * S, SB_WIDTH), BF16),
        name="sb_attention")(qkv, qkv, qkv)


def _mla_chains(ops, states, masks):
    ss = []
    for (q, k, v), mask in zip(ops, masks):
        s = _qk(q, k)
        if mask is not None:
            s = jnp.where(mask, s, -jnp.inf)
        ss.append(s)
    ms = [jnp.maximum(m, jnp.max(s, axis=-1, keepdims=True)) for s, (m, l, acc) in zip(ss, states)]
    ps = [jnp.exp2(s - m_new) for s, m_new in zip(ss, ms)]
    out = []
    for p, m_new, (q, k, v), (m, l, acc) in zip(ps, ms, ops, states):
        alpha = jnp.exp2(m - m_new)
        l = alpha * l + jnp.sum(p, axis=-1, keepdims=True)
        acc = alpha * acc + jnp.dot(p.astype(BF16), v, preferred_element_type=F32)
        out.append((m_new, l, acc))
    return tuple(out)


def _mla_body(q_ref, kv_ref, kpe_ref, o_ref, kf_ref, *, t):
    i = pl.program_id(2)
    halves = t // SUB
    hw = MLA_NOPE + MLA_V

    @pl.when(i == 0)
    def _():
        for g in range(ATT_HEADS):
            kf_ref[g, :, 0:LANE] = kv_ref[:, g * hw:g * hw + MLA_NOPE]
            kf_ref[g, :, LANE:QPAD] = kpe_ref[...]

    start = pl.multiple_of(i * t, t)
    chains = [(g, a) for g in range(ATT_HEADS) for a in range(halves)]

    def operands(g, a, row0, nrows):
        return (q_ref[a * SUB:(a + 1) * SUB, g * QPAD:(g + 1) * QPAD],
                kf_ref[g, pl.ds(row0, nrows), :],
                kv_ref[pl.ds(row0, nrows), g * hw + MLA_NOPE:(g + 1) * hw])

    ops, masks, st0 = [], [], []
    for g, a in chains:
        n = (a + 1) * SUB
        rr = lax.broadcasted_iota(jnp.int32, (SUB, n), 0)
        cc = lax.broadcasted_iota(jnp.int32, (SUB, n), 1)
        ops.append(operands(g, a, start, n))
        masks.append(cc <= rr + a * SUB)
        st0.append((jnp.full((SUB, 1), -jnp.inf, F32), jnp.zeros((SUB, 1), F32),
                    jnp.zeros((SUB, MLA_V), F32)))
    state = _mla_chains(ops, st0, masks)

    def step(j, st):
        row0 = pl.multiple_of(j * t, t)
        return _mla_chains([operands(g, a, row0, t) for g, a in chains], st, [None] * len(chains))

    state = lax.fori_loop(0, i, step, state)
    for n, (g, a) in enumerate(chains):
        m, l, acc = state[n]
        o_ref[a * SUB:(a + 1) * SUB, g * MLA_V:(g + 1) * MLA_V] = (acc / l).astype(o_ref.dtype)


def _mla_attention(q_full, kv, kpe, B, S):
    t = _tile(S, 2 * SUB)
    nq = S // t
    G = ATT_HEADS
    HG = MLA_HEADS // G
    return _call(
        functools.partial(_mla_body, t=t), grid=(B, HG, nq),
        in_specs=[pl.BlockSpec((t, G * QPAD), lambda b, h, i: (b * nq + i, h)),
                  pl.BlockSpec((S, G * (MLA_NOPE + MLA_V)), lambda b, h, i: (b, h)),
                  pl.BlockSpec((S, LANE), lambda b, h, i: (b, 0))],
        out_specs=pl.BlockSpec((t, G * MLA_V), lambda b, h, i: (b * nq + i, h)),
        out_shape=jax.ShapeDtypeStruct((B * S, MLA_HEADS * MLA_V), BF16),
        scratch=[pltpu.VMEM((G, S, QPAD), BF16)],
        name="mla_attention")(q_full, kv, kpe)


def _merge_body(osb_ref, omla_ref, gate_sb_ref, gate_mla_ref, wsb_ref, wmla_ref, wo_ref, x_ref,
                g1_ref, gain_ref, sc_ref, sh_ref, xo_ref, ho_ref):
    y = (gate_sb_ref[...].astype(F32)
         * jnp.dot(osb_ref[...], wsb_ref[...], preferred_element_type=F32)
         + gate_mla_ref[...].astype(F32)
         * jnp.dot(omla_ref[...], wmla_ref[...], preferred_element_type=F32))
    out = jnp.dot(y.astype(BF16), wo_ref[...], preferred_element_type=F32)
    x_new = x_ref[...] + g1_ref[...] * out
    xo_ref[...] = x_new
    ho_ref[...] = _norm_mod(x_new, gain_ref[...], sc_ref[...], sh_ref[...]).astype(ho_ref.dtype)


def _merge(o_sb, o_mla, gates, w_sb, w_mla, w_o, layer, x, g1, gain, sc, sh, S, h_dtype):
    M, D = x.shape
    tm = _tile(S, 256)
    bpt = S // tm
    vec = pl.BlockSpec((None, 1, D), lambda m: (m // bpt, 0, 0))
    res = lambda k: pl.BlockSpec((None, k, D), lambda m: (layer, 0, 0),
                                 pipeline_mode=pl.Buffered(1))
    return _call(
        _merge_body, grid=(M // tm,),
        in_specs=[pl.BlockSpec((tm, SB_WIDTH), lambda m: (m, 0)),
                  pl.BlockSpec((tm, MLA_HEADS * MLA_V), lambda m: (m, 0)),
                  pl.BlockSpec((tm, D), lambda m: (m, 0)),
                  pl.BlockSpec((tm, D), lambda m: (m, 1)),
                  res(SB_WIDTH), res(MLA_HEADS * MLA_V), res(D),
                  pl.BlockSpec((tm, D), lambda m: (m, 0)),
                  vec, pl.BlockSpec((1, D), lambda m: (0, 0)), vec, vec],
        out_specs=[pl.BlockSpec((tm, D), lambda m: (m, 0)),
                   pl.BlockSpec((tm, D), lambda m: (m, 0))],
        out_shape=[jax.ShapeDtypeStruct((M, D), F32), jax.ShapeDtypeStruct((M, D), h_dtype)],
        name="merge_out")(o_sb, o_mla, gates, gates, w_sb, w_mla, w_o, x, g1, gain, sc, sh)


def _resnorm_body(f_ref, x_ref, g_ref, gain_ref, sc_ref, sh_ref, xo_ref, ho_ref):
    x_new = x_ref[...] + g_ref[...] * f_ref[...]
    xo_ref[...] = x_new
    ho_ref[...] = _norm_mod(x_new, gain_ref[...], sc_ref[...], sh_ref[...]).astype(ho_ref.dtype)


def _resnorm(f, x, g, gain, sc, sh, S, h_dtype):
    M, D = x.shape
    tm = _tile(S, 512)
    bpt = S // tm
    vec = pl.BlockSpec((None, 1, D), lambda m: (m // bpt, 0, 0))
    rows = pl.BlockSpec((tm, D), lambda m: (m, 0))
    return _call(
        _resnorm_body, grid=(M // tm,),
        in_specs=[rows, rows, vec, pl.BlockSpec((1, D), lambda m: (0, 0)), vec, vec],
        out_specs=[rows, rows],
        out_shape=[jax.ShapeDtypeStruct((M, D), F32), jax.ShapeDtypeStruct((M, D), h_dtype)],
        name="resnorm")(f, x, g, gain, sc, sh)


def _split_bf16(v):
    hi = v.astype(BF16)
    return hi, (v - hi.astype(F32)).astype(BF16)


def _router_body(h_ref, w_ref, idx_ref, wt_ref):
    hh, hl = _split_bf16(h_ref[...])
    wh, wl = _split_bf16(w_ref[...])
    dot = functools.partial(jnp.dot, preferred_element_type=F32)
    logits = dot(hh, wh) + (dot(hl, wh) + dot(hh, wl))
    lane = lax.broadcasted_iota(jnp.int32, logits.shape, 1).astype(F32)
    logits = jnp.where(lane < N_EXPERTS, logits, -jnp.inf)
    m1 = jnp.max(logits, axis=-1, keepdims=True)
    i1 = jnp.min(jnp.where(logits == m1, lane, float(LANE)), axis=-1, keepdims=True)
    rest = jnp.where(lane == i1, -jnp.inf, logits)
    m2 = jnp.max(rest, axis=-1, keepdims=True)
    i2 = jnp.min(jnp.where(rest == m2, lane, float(LANE)), axis=-1, keepdims=True)
    e = jnp.exp(m2 - m1)
    w1 = 1.0 / (1.0 + e)
    w2 = e * w1
    idx_ref[...] = jnp.where(lane == 0.0, i1, jnp.where(lane == 1.0, i2, 0.0)).astype(jnp.int32)
    wt_ref[...] = jnp.where(lane == 0.0, w1, jnp.where(lane == 1.0, w2, 0.0))


def _router(h, w_router_pad, j):
    M, D = h.shape
    tm = _tile(M, 512)
    return _call(
        _router_body, grid=(M // tm,),
        in_specs=[pl.BlockSpec((tm, D), lambda m: (m, 0)),
                  pl.BlockSpec((None, D, LANE), lambda m: (j, 0, 0))],
        out_specs=[pl.BlockSpec((tm, LANE), lambda m: (m, 0)),
                   pl.BlockSpec((tm, LANE), lambda m: (m, 0))],
        out_shape=[jax.ShapeDtypeStruct((M, LANE), jnp.int32),
                   jax.ShapeDtypeStruct((M, LANE), F32)],
        name="router")(h, w_router_pad)


MOE_SUB = 256
ROW_UNROLL = 8


def _gather_body(src_ref, h_hbm, o_ref, buf_ref, sem, *, tm):
    t = pl.program_id(0)

    def issue(tile):
        slot = tile % 2

        def blk(b, _):
            for u in range(ROW_UNROLL):
                r = b * ROW_UNROLL + u
                pltpu.make_async_copy(h_hbm.at[pl.ds(src_ref[tile * tm + r], 1)],
                                      buf_ref.at[slot, pl.ds(r, 1)],
                                      sem.at[slot]).start(priority=u % 2)
            return 0

        lax.fori_loop(0, tm // ROW_UNROLL, blk, 0)

    n_active = src_ref[pl.num_programs(0) * tm]

    @pl.when(t == 0)
    def _():
        issue(t)

    @pl.when(t + 1 < n_active)
    def _():
        issue(t + 1)

    @pl.when(t < n_active)
    def _():
        slot = t % 2
        pltpu.make_async_copy(h_hbm.at[pl.ds(0, tm)], buf_ref.at[slot], sem.at[slot]).wait()
        o_ref[...] = buf_ref[slot].astype(o_ref.dtype)

    @pl.when(t >= n_active)
    def _():
        o_ref[...] = jnp.zeros_like(o_ref)


def _gather_rows(src_tok, h, n_tiles, tm):
    M, D = h.shape
    return _call(
        functools.partial(_gather_body, tm=tm), grid=(n_tiles,), nsp=1,
        in_specs=[pl.BlockSpec(memory_space=pl.ANY)],
        out_specs=pl.BlockSpec((tm, D), lambda t, src: (t, 0)),
        out_shape=jax.ShapeDtypeStruct((n_tiles * tm, D), BF16),
        scratch=[pltpu.VMEM((2, tm, D), F32), pltpu.SemaphoreType.DMA((2,))],
        name="moe_gather")(src_tok, h)


def _grouped_body(te_ref, a_ref, *refs, n_w, layer, epilogue):
    w_hbm = refs[:n_w]
    n_rest = len(refs) - 3 * n_w - 1
    rest = refs[n_w:n_w + n_rest]
    stage = refs[n_w + n_rest:2 * n_w + n_rest]
    wb_refs = refs[2 * n_w + n_rest:3 * n_w + n_rest]
    sem = refs[-1]
    c, t = pl.program_id(0), pl.program_id(1)
    nc, nt = pl.num_programs(0), pl.num_programs(1)
    expert = te_ref[t]
    n_valid = te_ref[nt + t]
    nxt = te_ref[2 * nt + t]
    changed = jnp.logical_or(t == 0, expert != te_ref[jnp.maximum(t - 1, 0)])
    tn = stage[0].shape[1]

    def block_copy(i, chunk, e):
        col0 = pl.multiple_of(chunk * tn, tn)
        return pltpu.make_async_copy(w_hbm[i].at[layer, e, :, pl.ds(col0, tn)], stage[i],
                                     sem.at[i])

    @pl.when(changed)
    def _():
        @pl.when(jnp.logical_and(c == 0, t == 0))
        def _():
            for i in range(n_w):
                block_copy(i, c, expert).start()

        for i in range(n_w):
            block_copy(i, c, expert).wait()
            wb_refs[i][...] = stage[i][...].astype(BF16)

        @pl.when(nxt >= 0)
        def _():
            for i in range(n_w):
                block_copy(i, c, nxt).start()

        @pl.when(jnp.logical_and(nxt < 0, c + 1 < nc))
        def _():
            for i in range(n_w):
                block_copy(i, c + 1, te_ref[0]).start()

    for s in range(a_ref.shape[0] // MOE_SUB):
        rows = slice(s * MOE_SUB, (s + 1) * MOE_SUB)

        @pl.when(s * MOE_SUB < n_valid)
        def _():
            a = a_ref[rows, :]
            accs = [jnp.dot(a, wb_ref[...], preferred_element_type=F32) for wb_ref in wb_refs]
            epilogue(accs, rest[:n_rest], rows)

        @pl.when(s * MOE_SUB >= n_valid)
        def _():
            for o_ref in rest[:n_rest]:
                o_ref[rows, :] = jnp.zeros((MOE_SUB, o_ref.shape[1]), o_ref.dtype)


def _up_epilogue(accs, refs, rows=slice(None)):
    g, u = accs
    refs[0][rows, :] = (g * jax.nn.sigmoid(g) * u).astype(refs[0].dtype)


def _grouped_up(tile_e, xs, w_gate, w_up, j, tm, tf):
    NP, D = xs.shape
    F = w_gate.shape[-1]
    hbm = pl.BlockSpec(memory_space=pl.ANY)
    return _call(
        functools.partial(_grouped_body, n_w=2, layer=j, epilogue=_up_epilogue),
        grid=(F // tf, NP // tm), nsp=1,
        in_specs=[pl.BlockSpec((tm, D), lambda f, t, te: (t, 0)), hbm, hbm],
        out_specs=pl.BlockSpec((tm, tf), lambda f, t, te: (t, f)),
        out_shape=jax.ShapeDtypeStruct((NP, F), BF16),
        scratch=[pltpu.VMEM((D, tf), F32), pltpu.VMEM((D, tf), F32),
                 pltpu.VMEM((D, tf), BF16), pltpu.VMEM((D, tf), BF16),
                 pltpu.SemaphoreType.DMA((2,))],
        name="moe_up")(tile_e, xs, w_gate, w_up)


def _down_epilogue(accs, refs, rows):
    refs[0][rows, :] = accs[0]


def _grouped_down(tile_e, hs, w_down, j, tm, tn):
    NP, F = hs.shape
    D = w_down.shape[-1]
    return _call(
        functools.partial(_grouped_body, n_w=1, layer=j, epilogue=_down_epilogue),
        grid=(D // tn, NP // tm), nsp=1,
        in_specs=[pl.BlockSpec((tm, F), lambda n, t, te: (t, 0)),
                  pl.BlockSpec(memory_space=pl.ANY)],
        out_specs=pl.BlockSpec((tm, tn), lambda n, t, te: (t, n)),
        out_shape=jax.ShapeDtypeStruct((NP, D), F32),
        scratch=[pltpu.VMEM((F, tn), F32), pltpu.VMEM((F, tn), BF16),
                 pltpu.SemaphoreType.DMA((1,))],
        name="moe_down")(tile_e, hs, w_down)


def _combine_body(pos_ref, y_hbm, wt_ref, x_ref, g2_ref, gain_ref, sc_ref, sh_ref, xo_ref, ho_ref,
                  buf_ref, sem, *, tm):
    t = pl.program_id(0)

    def issue(tile):
        slot = tile % 2

        def blk(b, _):
            for u in range(ROW_UNROLL):
                r = b * ROW_UNROLL + u
                for k in range(TOP_K):
                    pltpu.make_async_copy(
                        y_hbm.at[pl.ds(pos_ref[(tile * tm + r) * TOP_K + k], 1)],
                        buf_ref.at[slot, k, pl.ds(r, 1)], sem.at[slot]).start(priority=k % 2)
            return 0

        lax.fori_loop(0, tm // ROW_UNROLL, blk, 0)

    @pl.when(t == 0)
    def _():
        issue(t)

    @pl.when(t + 1 < pl.num_programs(0))
    def _():
        issue(t + 1)

    slot = t % 2
    for k in range(TOP_K):
        pltpu.make_async_copy(y_hbm.at[pl.ds(0, tm)], buf_ref.at[slot, k], sem.at[slot]).wait()
    wt = wt_ref[...]
    f = buf_ref[slot, 0] * wt[:, 0:1]
    for k in range(1, TOP_K):
        f = f + buf_ref[slot, k] * wt[:, k:k + 1]
    x_new = x_ref[...] + g2_ref[...] * f
    xo_ref[...] = x_new
    ho_ref[...] = _norm_mod(x_new, gain_ref[...], sc_ref[...], sh_ref[...]).astype(ho_ref.dtype)


def _combine(pos, y_sorted, wts, x, g2, gain, sc, sh, S, h_dtype):
    M, D = x.shape
    tm = _tile(S, 256)
    bpt = S // tm
    vec = pl.BlockSpec((None, 1, D), lambda m, p: (m // bpt, 0, 0))
    return _call(
        functools.partial(_combine_body, tm=tm), grid=(M // tm,), nsp=1,
        in_specs=[pl.BlockSpec(memory_space=pl.ANY),
                  pl.BlockSpec((tm, LANE), lambda m, p: (m, 0)),
                  pl.BlockSpec((tm, D), lambda m, p: (m, 0)),
                  vec, pl.BlockSpec((1, D), lambda m, p: (0, 0)), vec, vec],
        out_specs=[pl.BlockSpec((tm, D), lambda m, p: (m, 0)),
                   pl.BlockSpec((tm, D), lambda m, p: (m, 0))],
        out_shape=[jax.ShapeDtypeStruct((M, D), F32), jax.ShapeDtypeStruct((M, D), h_dtype)],
        scratch=[pltpu.VMEM((2, TOP_K, tm, D), F32), pltpu.SemaphoreType.DMA((2,))],
        name="moe_combine")(pos, y_sorted, wts, x, g2, gain, sc, sh)


def _dispatch_plan(top_i, tm):
    M = top_i.shape[0]
    E = N_EXPERTS
    n_ent = M * TOP_K
    e_flat = top_i.reshape(-1)
    onehot = (e_flat[:, None] == jnp.arange(E, dtype=jnp.int32)[None, :]).astype(jnp.int32)
    csum = jnp.cumsum(onehot, axis=0)
    rank = jnp.sum(onehot * csum, axis=1) - 1
    counts = csum[-1]
    pcounts = ((counts + tm - 1) // tm) * tm
    pend = jnp.cumsum(pcounts)
    pstart = pend - pcounts
    dest = (pstart[e_flat] + rank).astype(jnp.int32)
    n_tiles = n_ent // tm + E
    n_rows = n_tiles * tm
    src_tok = jnp.zeros((n_rows,), jnp.int32).at[dest].set(
        jnp.arange(n_ent, dtype=jnp.int32) // TOP_K)
    tile_start = jnp.arange(n_tiles, dtype=jnp.int32) * tm
    tile_e = jnp.sum((tile_start[:, None] >= pend[None, :]).astype(jnp.int32), axis=1)
    tile_e = jnp.minimum(tile_e, E - 1).astype(jnp.int32)
    n_valid = jnp.clip(pstart[tile_e] + counts[tile_e] - tile_start, 0, tm).astype(jnp.int32)
    n_active = pend[-1] // tm
    tile_e = jnp.where(tile_start < pend[-1], tile_e, tile_e[n_active - 1])
    tidx = jnp.arange(n_tiles, dtype=jnp.int32)
    later_other = (tidx[None, :] > tidx[:, None]) & (tile_e[None, :] != tile_e[:, None])
    nxt = jnp.where(jnp.any(later_other, axis=1), tile_e[jnp.argmax(later_other, axis=1)], -1)
    src_tok = jnp.concatenate([src_tok, n_active[None].astype(jnp.int32)])
    return src_tok, dest, jnp.concatenate([tile_e, n_valid, nxt]).astype(jnp.int32), n_tiles


def _rot_cols(w):
    half = MLA_ROPE // 2
    return jnp.concatenate([-w[..., half:], w[..., :half]], axis=-1)


def _rope_tables(positions):
    inv_freq = 1.0 / (ROPE_THETA ** (jnp.arange(0, MLA_ROPE, 2, dtype=F32) / MLA_ROPE))
    ang = positions.astype(F32)[..., None] * inv_freq
    cos, sin = jnp.cos(ang), jnp.sin(ang)
    M = positions.size
    cos2 = jnp.concatenate([cos, cos], axis=-1).reshape(M, MLA_ROPE)
    sin2 = jnp.concatenate([sin, sin], axis=-1).reshape(M, MLA_ROPE)
    zeros = jnp.zeros((M, LANE - MLA_ROPE), F32)
    cos_k = jnp.concatenate([cos2, zeros], axis=-1)
    sin_k = jnp.concatenate([sin2, zeros], axis=-1)
    return cos_k, sin_k, cos_k * MLA_SCALE, sin_k * MLA_SCALE


def kernel(x, c, positions, w_ada, b_ada, norm_mix_g, norm_ffn_g, w_in, q_norm_g, kv_norm_g, w_uq, w_ukv, w_sb_up, w_mla_up, w_o, w_ffn_gate, w_ffn_up, w_ffn_down, w_router, w_exp_gate, w_exp_up, w_exp_down, final_norm_g):
    B, S, D = x.shape
    L = w_ada.shape[0]
    M = B * S
    x = x.reshape(M, D)

    mod = _modulation(c, w_ada, b_ada)
    mod = mod.reshape(L, B, N_MOD, 1, D)
    sh1, sc1, g1, sh2, sc2, g2 = [mod[:, :, k] for k in range(N_MOD)]
    cos_k, sin_k, cos_q, sin_q = _rope_tables(positions)

    pe0 = 3 * SB_WIDTH + Q_RANK + KV_RANK
    w_in_t = jnp.swapaxes(w_in, 1, 2)
    w_pe = w_in_t[:, pe0:pe0 + MLA_ROPE]
    zpad = jnp.zeros((L, LANE - MLA_ROPE, D), F32)
    half = MLA_ROPE // 2
    w_pe_rot = jnp.concatenate([-w_pe[:, half:], w_pe[:, :half]], axis=1)
    w_pe2 = jnp.concatenate([w_pe, zpad, w_pe_rot, zpad], axis=1)
    c_gain = jnp.concatenate([q_norm_g, kv_norm_g], axis=-1)[:, None, :]

    wq = w_uq.reshape(L, Q_RANK, MLA_HEADS, MLA_QK)
    wq_n, wq_r = wq[..., :MLA_NOPE], wq[..., MLA_NOPE:]
    zq = jnp.zeros(wq_r.shape, F32)
    wq_main = jnp.concatenate([wq_n, wq_r, zq], axis=-1).reshape(
        L, Q_RANK, MLA_HEADS * QPAD).astype(BF16)
    wq_rot = jnp.concatenate([_rot_cols(wq_r), zq], axis=-1).reshape(
        L, Q_RANK, MLA_HEADS * LANE).astype(BF16)

    w_sb_b, w_mla_b, w_o_b = (w.astype(BF16) for w in (w_sb_up, w_mla_up, w_o))
    w_router_pad = jnp.concatenate(
        [w_router, jnp.zeros(w_router.shape[:2] + (LANE - N_EXPERTS,), F32)], axis=-1)

    tm = _tile(S, 1024)
    bpt = S // tm
    row = lambda width: pl.BlockSpec((tm, width), lambda n, m: (m, 0))
    sb_scale = HEAD_DIM ** -0.5 * LOG2E

    def h_dtype_for(layer_is_moe):
        return F32 if layer_is_moe else BF16

    h = _prenorm(x, norm_mix_g[0][None, :], sc1[0], sh1[0], S, BF16)

    for l in range(L):
        tn = _tile(min(SB_WIDTH, 2 * D), 1024)
        nq_blk = SB_WIDTH // tn

        def qkv_epi(accs, n, extra, outs):
            s = jnp.where(n < nq_blk, sb_scale, 1.0)
            outs[0][...] = (accs[0] * s).astype(BF16)

        (qkv,) = _matmul(h, [w_in_t], layer=l, col_blk0=0, n_cols=3 * SB_WIDTH, tn=tn, tm=tm,
                         epilogue=qkv_epi, transposed=True,
                         out_shape=[jax.ShapeDtypeStruct((M, 3 * SB_WIDTH), BF16)],
                         out_specs=[pl.BlockSpec((tm, tn), lambda n, m: (m, n))], name="in_qkv")

        lat, kpe = _latent_in(h, w_in_t, w_pe2, c_gain, cos_k, sin_k, l, tm)

        def gate_epi(accs, n, extra, outs):
            outs[0][...] = jax.nn.sigmoid(accs[0]).astype(BF16)

        gates = _matmul_rows(h, w_in_t, layer=l, row0=pe0 + MLA_ROPE, n_cols=2 * D, tn=tn, tm=tm,
                             epilogue=gate_epi, out_dtype=BF16, name="in_gates")

        q_full, kv = _latent_up(lat, wq_main, wq_rot, w_ukv, l, cos_q, sin_q, tm)

        o_sb = _sb_attention(qkv, B, S)
        o_mla = _mla_attention(q_full, kv, kpe, B, S)

        moe = (l % 2 == 1)
        x, h = _merge(o_sb, o_mla, gates, w_sb_b, w_mla_b, w_o_b, l, x, g1[l],
                      norm_ffn_g[l][None, :], sc2[l], sh2[l], S, h_dtype_for(moe))

        last = (l == L - 1)
        if last:
            n_gain = final_norm_g[None, :]
            n_sc = jnp.zeros_like(sc1[0])
            n_sh = jnp.zeros_like(sh1[0])
            n_dtype = F32
        else:
            n_gain, n_sc, n_sh, n_dtype = norm_mix_g[l + 1][None, :], sc1[l + 1], sh1[l + 1], BF16
        j = l // 2
        if not moe:
            F = w_ffn_gate.shape[-1]
            tf = _tile(F, 512)

            def up_epi(accs, n, extra, outs):
                _up_epilogue(accs, outs)

            (hmid,) = _matmul(h, [w_ffn_gate, w_ffn_up], layer=j, col_blk0=0, n_cols=F, tn=tf,
                              tm=tm, epilogue=up_epi,
                              out_shape=[jax.ShapeDtypeStruct((M, F), BF16)],
                              out_specs=[pl.BlockSpec((tm, tf), lambda n, m: (m, n))],
                              name="ffn_up")
            def down_epi(accs, n, extra, outs):
                outs[0][...] = accs[0]

            tnd = _tile(D, 512)
            (f,) = _matmul(hmid, [w_ffn_down], layer=j, col_blk0=0, n_cols=D, tn=tnd,
                           tm=_tile(S, 512), epilogue=down_epi,
                           out_shape=[jax.ShapeDtypeStruct((M, D), F32)],
                           out_specs=[pl.BlockSpec((_tile(S, 512), tnd), lambda n, m: (m, n))],
                           name="ffn_down")
            x, h = _resnorm(f, x, g2[l], n_gain, n_sc, n_sh, S, n_dtype)
        else:
            F = w_exp_gate.shape[-1]
            tme = max(_tile(M, 2 * MOE_SUB), MOE_SUB)
            idx, wts = _router(h, w_router_pad, j)
            src_tok, dest, tile_e, n_tiles = _dispatch_plan(idx[:, :TOP_K], tme)
            xs = _gather_rows(src_tok, h, n_tiles, tme)
            tfe = F // 4 if F % (4 * LANE) == 0 else _tile(F, 512)
            hs = _grouped_up(tile_e, xs, w_exp_gate, w_exp_up, j, tme, tfe)
            ys = _grouped_down(tile_e, hs, w_exp_down, j, tme, _tile(D, 512))
            x, h = _combine(dest, ys, wts, x, g2[l], n_gain, n_sc, n_sh, S, n_dtype)

    return h.reshape(B, S, D)
```

```python
import functools

import jax
import jax.numpy as jnp
from jax import lax
from jax.experimental import pallas as pl
from jax.experimental.pallas import tpu as pltpu

BF16 = jnp.bfloat16
F32 = jnp.float32

SB_HEADS = 8
HEAD_DIM = 128
SB_WIDTH = SB_HEADS * HEAD_DIM
MLA_HEADS = 8
MLA_NOPE = 128
MLA_ROPE = 64
MLA_V = 128
MLA_QK = MLA_NOPE + MLA_ROPE
Q_RANK = 512
KV_RANK = 512
ROPE_THETA = 10000.0
N_EXPERTS = 8
TOP_K = 2
N_MOD = 6
NORM_EPS = 1e-6

LOG2E = 1.4426950408889634
MLA_SCALE = MLA_QK ** -0.5 * LOG2E

LANE = 128
QPAD = 2 * LANE
VMEM_LIMIT = 56 << 20


def _call(body, *, grid, in_specs, out_specs, out_shape, scratch=(), nsp=0, name, flags=None):
    return pl.pallas_call(
        body,
        grid_spec=pltpu.PrefetchScalarGridSpec(
            num_scalar_prefetch=nsp, grid=grid, in_specs=in_specs,
            out_specs=out_specs, scratch_shapes=scratch),
        out_shape=out_shape,
        compiler_params=pltpu.CompilerParams(
            dimension_semantics=("arbitrary",) * len(grid),
            vmem_limit_bytes=VMEM_LIMIT, flags=flags),
        name=name)


def _tile(n, pref):
    t = min(n, pref)
    while n % t:
        t //= 2
    return t


def _norm_mod(x, gain, sc, sh):
    ms = jnp.mean(x * x, axis=-1, keepdims=True)
    y = x * lax.rsqrt(ms + NORM_EPS)
    return (y * gain) * (1.0 + sc) + sh


def _mm_body(*refs, n_w, n_extra, n_out, epilogue, transposed):
    a_ref = refs[0]
    w_refs = refs[1:1 + n_w]
    extra = refs[1 + n_w:1 + n_w + n_extra]
    outs = refs[1 + n_w + n_extra:1 + n_w + n_extra + n_out]
    wb_refs = refs[1 + n_w + n_extra + n_out:]

    @pl.when(pl.program_id(1) == 0)
    def _():
        for w_ref, wb_ref in zip(w_refs, wb_refs):
            wb_ref[...] = w_ref[...].astype(BF16)

    a = a_ref[...]
    dims = (((1,), (1,)), ((), ())) if transposed else (((1,), (0,)), ((), ()))
    accs = [lax.dot_general(a, wb_ref[...], dims, preferred_element_type=F32)
            for wb_ref in wb_refs]
    epilogue(accs, pl.program_id(0), extra, outs)


def _matmul(a, ws, *, layer, col_blk0, n_cols, tn, tm, epilogue, extras=(), extra_specs=(),
            out_shape, out_specs, name, a_blk=0, transposed=False):
    M = a.shape[0]
    K = ws[0].shape[2 if transposed else 1]
    grid = (n_cols // tn, M // tm)
    in_specs = [pl.BlockSpec((tm, K), lambda n, m: (m, a_blk))]
    for _ in ws:
        if transposed:
            in_specs.append(pl.BlockSpec((None, tn, K), lambda n, m: (layer, n + col_blk0, 0)))
        else:
            in_specs.append(pl.BlockSpec((None, K, tn), lambda n, m: (layer, 0, n + col_blk0)))
    in_specs += list(extra_specs)
    body = functools.partial(_mm_body, n_w=len(ws), n_extra=len(extras), n_out=len(out_shape),
                             epilogue=epilogue, transposed=transposed)
    wb_shape = (tn, K) if transposed else (K, tn)
    return _call(body, grid=grid, in_specs=in_specs, out_specs=out_specs, out_shape=out_shape,
                 scratch=[pltpu.VMEM(wb_shape, BF16) for _ in ws], name=name)(a, *ws, *extras)


def _mm_rows_body(a_ref, w_hbm, o_ref, stage_ref, wb_ref, sem, *, layer, row0, epilogue):
    n, m = pl.program_id(0), pl.program_id(1)
    tn = stage_ref.shape[0]

    def chunk_copy(chunk):
        r0 = pl.multiple_of(row0 + chunk * tn, 8)
        return pltpu.make_async_copy(w_hbm.at[layer, pl.ds(r0, tn), :], stage_ref, sem.at[0])

    @pl.when(m == 0)
    def _():
        @pl.when(n == 0)
        def _():
            chunk_copy(n).start()

        chunk_copy(n).wait()
        wb_ref[...] = stage_ref[...].astype(BF16)

        @pl.when(n + 1 < pl.num_programs(0))
        def _():
            chunk_copy(n + 1).start()

    acc = lax.dot_general(a_ref[...], wb_ref[...], (((1,), (1,)), ((), ())),
                          preferred_element_type=F32)
    epilogue([acc], n, (), [o_ref])


def _matmul_rows(a, w_t, *, layer, row0, n_cols, tn, tm, epilogue, out_dtype, name):
    M, K = a.shape
    body = functools.partial(_mm_rows_body, layer=layer, row0=row0, epilogue=epilogue)
    return _call(body, grid=(n_cols // tn, M // tm),
                 in_specs=[pl.BlockSpec((tm, K), lambda n, m: (m, 0)),
                           pl.BlockSpec(memory_space=pl.ANY)],
                 out_specs=pl.BlockSpec((tm, tn), lambda n, m: (m, n)),
                 out_shape=jax.ShapeDtypeStruct((M, n_cols), out_dtype),
                 scratch=[pltpu.VMEM((tn, K), F32), pltpu.VMEM((tn, K), BF16),
                          pltpu.SemaphoreType.DMA((1,))],
                 name=name)(a, w_t)


def _mod_body(c_ref, w_ref, b_ref, o_ref):
    c = c_ref[...]
    c_act = (c * jax.nn.sigmoid(c)).astype(BF16)
    o_ref[...] = jnp.dot(c_act, w_ref[...].astype(BF16), preferred_element_type=F32) + b_ref[...]


def _modulation(c, w_ada, b_ada):
    L, D, N = w_ada.shape
    B = c.shape[0]
    rows = 8
    c_pad = jnp.zeros((rows, D), F32).at[:B].set(c)
    tn = _tile(N, 1024)
    out = _call(
        _mod_body, grid=(L, N // tn),
        in_specs=[pl.BlockSpec((rows, D), lambda l, n: (0, 0)),
                  pl.BlockSpec((None, D, tn), lambda l, n: (l, 0, n)),
                  pl.BlockSpec((None, 1, tn), lambda l, n: (l, 0, n))],
        out_specs=pl.BlockSpec((None, rows, tn), lambda l, n: (l, 0, n)),
        out_shape=jax.ShapeDtypeStruct((L, rows, N), F32),
        name="adaln_mod")(c_pad, w_ada, b_ada.reshape(L, 1, N))
    return out[:, :B, :]


def _latent_in_body(h_ref, wl_ref, wpe_ref, gain_ref, cos_ref, sin_ref, lat_ref, kpe_ref,
                    wlb_ref, wpeb_ref):
    @pl.when(pl.program_id(0) == 0)
    def _():
        wlb_ref[...] = wl_ref[...].astype(BF16)
        wpeb_ref[...] = wpe_ref[...].astype(BF16)

    h = h_ref[...]
    nt = (((1,), (1,)), ((), ()))
    c = lax.dot_general(h, wlb_ref[...], nt, preferred_element_type=F32)
    for k0 in range(0, c.shape[1], Q_RANK):
        a = c[:, k0:k0 + Q_RANK]
        ms = jnp.mean(a * a, axis=-1, keepdims=True)
        lat_ref[:, k0:k0 + Q_RANK] = (a * lax.rsqrt(ms + NORM_EPS)
                                      * gain_ref[:, k0:k0 + Q_RANK]).astype(lat_ref.dtype)
    p = lax.dot_general(h, wpeb_ref[...], nt, preferred_element_type=F32)
    kpe_ref[...] = (p[:, :LANE] * cos_ref[...] + p[:, LANE:] * sin_ref[...]).astype(kpe_ref.dtype)


def _latent_in(h, w_in_t, w_pe2, c_gain, cos_k, sin_k, layer, tm):
    M, D = h.shape
    n_lat = Q_RANK + KV_RANK
    lat_blk = 3 * SB_WIDTH // n_lat
    rows = lambda n: pl.BlockSpec((tm, n), lambda m: (m, 0))
    return _call(
        _latent_in_body, grid=(M // tm,),
        in_specs=[rows(D),
                  pl.BlockSpec((None, n_lat, D), lambda m: (layer, lat_blk, 0)),
                  pl.BlockSpec((None, 2 * LANE, D), lambda m: (layer, 0, 0)),
                  pl.BlockSpec((None, 1, n_lat), lambda m: (layer, 0, 0)),
                  rows(LANE), rows(LANE)],
        out_specs=[rows(n_lat), rows(LANE)],
        out_shape=[jax.ShapeDtypeStruct((M, n_lat), BF16), jax.ShapeDtypeStruct((M, LANE), BF16)],
        scratch=[pltpu.VMEM((n_lat, D), BF16), pltpu.VMEM((2 * LANE, D), BF16)],
        name="latent_in")(h, w_in_t, w_pe2, c_gain, cos_k, sin_k)


def _latent_up_body(lat_ref, wqm_ref, wqr_ref, wkv_ref, cos_ref, sin_ref, q_ref, kv_ref, wkvb_ref):
    @pl.when(pl.program_id(0) == 0)
    def _():
        wkvb_ref[...] = wkv_ref[...].astype(BF16)

    cq = lat_ref[:, :Q_RANK]
    main = jnp.dot(cq, wqm_ref[...], preferred_element_type=F32)
    rot = jnp.dot(cq, wqr_ref[...], preferred_element_type=F32)
    cos, sin = cos_ref[...], sin_ref[...]
    for h in range(MLA_HEADS):
        c0 = h * QPAD
        q_ref[:, c0:c0 + LANE] = (main[:, c0:c0 + LANE] * MLA_SCALE).astype(q_ref.dtype)
        q_ref[:, c0 + LANE:c0 + QPAD] = (main[:, c0 + LANE:c0 + QPAD] * cos
                                         + rot[:, h * LANE:(h + 1) * LANE] * sin).astype(q_ref.dtype)
    kv_ref[...] = jnp.dot(lat_ref[:, Q_RANK:], wkvb_ref[...],
                          preferred_element_type=F32).astype(kv_ref.dtype)


def _latent_up(lat, wq_main, wq_rot, w_ukv, layer, cos_q, sin_q, tm):
    M = lat.shape[0]
    nq, nr, nkv = wq_main.shape[-1], wq_rot.shape[-1], w_ukv.shape[-1]
    whole = lambda k, n: pl.BlockSpec((None, k, n), lambda m: (layer, 0, 0))
    rows = lambda n: pl.BlockSpec((tm, n), lambda m: (m, 0))
    return _call(
        _latent_up_body, grid=(M // tm,),
        in_specs=[rows(Q_RANK + KV_RANK), whole(Q_RANK, nq), whole(Q_RANK, nr),
                  whole(KV_RANK, nkv), rows(LANE), rows(LANE)],
        out_specs=[rows(nq), rows(nkv)],
        out_shape=[jax.ShapeDtypeStruct((M, nq), BF16), jax.ShapeDtypeStruct((M, nkv), BF16)],
        scratch=[pltpu.VMEM((KV_RANK, nkv), BF16)],
        name="latent_up")(lat, wq_main, wq_rot, w_ukv, cos_q, sin_q)


def _prenorm_body(x_ref, g_ref, sc_ref, sh_ref, o_ref):
    o_ref[...] = _norm_mod(x_ref[...], g_ref[...], sc_ref[...], sh_ref[...]).astype(o_ref.dtype)


def _prenorm(x, gain, sc, sh, S, out_dtype):
    M, D = x.shape
    tm = _tile(S, 512)
    bpt = S // tm
    vec = pl.BlockSpec((None, 1, D), lambda m: (m // bpt, 0, 0))
    return _call(
        _prenorm_body, grid=(M // tm,),
        in_specs=[pl.BlockSpec((tm, D), lambda m: (m, 0)),
                  pl.BlockSpec((1, D), lambda m: (0, 0)), vec, vec],
        out_specs=pl.BlockSpec((tm, D), lambda m: (m, 0)),
        out_shape=jax.ShapeDtypeStruct((M, D), out_dtype),
        name="prenorm")(x, gain, sc, sh)


SUB = 256
ATT_HEADS = 2


def _qk(q, k):
    return lax.dot_general(q, k, (((1,), (1,)), ((), ())), preferred_element_type=F32)


def _sb_chains(ops, tri, states, masks):
    zs = [_qk(q, k) for q, k, v in ops]
    sps = []
    for z, mask in zip(zs, masks):
        sp = jnp.maximum(z, 0.0) + jnp.log2(1.0 + jnp.exp2(jnp.minimum(z, -z)))
        if mask is not None:
            sp = jnp.where(mask, sp, 0.0)
        sps.append(sp.astype(BF16))
    csums, carries = [], []
    for spb, (carry, acc) in zip(sps, states):
        parts = []
        for c in reversed(range(spb.shape[1] // SUB)):
            loc = jnp.dot(spb[:, c * SUB:(c + 1) * SUB], tri, preferred_element_type=F32)
            parts.insert(0, loc + carry)
            carry = carry + loc[:, 0:1]
        csums.append(parts[0] if len(parts) == 1 else jnp.concatenate(parts, axis=1))
        carries.append(carry)
    out = []
    for z, csum, mask, carry, (q, k, v), (_, acc) in zip(zs, csums, masks, carries, ops, states):
        a = jnp.exp2(z - csum)
        if mask is not None:
            a = jnp.where(mask, a, 0.0)
        out.append((carry, acc + jnp.dot(a.astype(BF16), v, preferred_element_type=F32)))
    return tuple(out)


def _sb_body(q_ref, k_ref, v_ref, o_ref, *, t):
    i = pl.program_id(2)
    halves = t // SUB
    r = lax.broadcasted_iota(jnp.int32, (SUB, SUB), 0)
    c = lax.broadcasted_iota(jnp.int32, (SUB, SUB), 1)
    tri = jnp.where(r >= c, 1.0, 0.0).astype(BF16)
    start = pl.multiple_of(i * t, t)
    chains = [(g, a) for g in range(ATT_HEADS) for a in range(halves)]

    def operands(g, a, row0, nrows):
        cols = slice(g * HEAD_DIM, (g + 1) * HEAD_DIM)
        return (q_ref[a * SUB:(a + 1) * SUB, cols], k_ref[pl.ds(row0, nrows), cols],
                v_ref[pl.ds(row0, nrows), cols])

    ops, masks, st0 = [], [], []
    for g, a in chains:
        n = (a + 1) * SUB
        rr = lax.broadcasted_iota(jnp.int32, (SUB, n), 0)
        cc = lax.broadcasted_iota(jnp.int32, (SUB, n), 1)
        ops.append(operands(g, a, start, n))
        masks.append(cc < rr + a * SUB)
        st0.append((jnp.zeros((SUB, 1), F32), jnp.zeros((SUB, HEAD_DIM), F32)))
    state = _sb_chains(ops, tri, st0, masks)

    def step(jj, st):
        row0 = pl.multiple_of((i - 1 - jj) * t, t)
        return _sb_chains([operands(g, a, row0, t) for g, a in chains], tri, st,
                          [None] * len(chains))

    state = lax.fori_loop(0, i, step, state)
    for n, (g, a) in enumerate(chains):
        o_ref[a * SUB:(a + 1) * SUB, g * HEAD_DIM:(g + 1) * HEAD_DIM] = state[n][1].astype(o_ref.dtype)


def _sb_attention(qkv, B, S):
    t = _tile(S, 2 * SUB)
    nq = S // t
    G = ATT_HEADS
    HG = SB_HEADS // G
    W = G * HEAD_DIM
    return _call(
        functools.partial(_sb_body, t=t), grid=(B, HG, nq),
        in_specs=[pl.BlockSpec((t, W), lambda b, h, i: (b * nq + i, h)),
                  pl.BlockSpec((S, W), lambda b, h, i: (b, HG + h)),
                  pl.BlockSpec((S, W), lambda b, h, i: (b, 2 * HG + h))],
        out_specs=pl.BlockSpec((t, W), lambda b, h, i: (b * nq + i, h)),
        out_shape=jax.ShapeDtypeStruct((B * S, SB_WIDTH), BF16),
        name="sb_attention")(qkv, qkv, qkv)


def _mla_chains(ops, states, masks):
    ss = []
    for (q, k, v), mask in zip(ops, masks):
        s = _qk(q, k)
        if mask is not None:
            s = jnp.where(mask, s, -jnp.inf)
        ss.append(s)
    ms = [jnp.maximum(m, jnp.max(s, axis=-1, keepdims=True)) for s, (m, l, acc) in zip(ss, states)]
    ps = [jnp.exp2(s - m_new) for s, m_new in zip(ss, ms)]
    out = []
    for p, m_new, (q, k, v), (m, l, acc) in zip(ps, ms, ops, states):
        alpha = jnp.exp2(m - m_new)
        l = alpha * l + jnp.sum(p, axis=-1, keepdims=True)
        acc = alpha * acc + jnp.dot(p.astype(BF16), v, preferred_element_type=F32)
        out.append((m_new, l, acc))
    return tuple(out)


def _mla_body(q_ref, kv_ref, kpe_ref, o_ref, kf_ref, *, t):
    i = pl.program_id(2)
    halves = t // SUB
    hw = MLA_NOPE + MLA_V

    @pl.when(i == 0)
    def _():
        for g in range(ATT_HEADS):
            kf_ref[g, :, 0:LANE] = kv_ref[:, g * hw:g * hw + MLA_NOPE]
            kf_ref[g, :, LANE:QPAD] = kpe_ref[...]

    start = pl.multiple_of(i * t, t)
    chains = [(g, a) for g in range(ATT_HEADS) for a in range(halves)]

    def operands(g, a, row0, nrows):
        return (q_ref[a * SUB:(a + 1) * SUB, g * QPAD:(g + 1) * QPAD],
                kf_ref[g, pl.ds(row0, nrows), :],
                kv_ref[pl.ds(row0, nrows), g * hw + MLA_NOPE:(g + 1) * hw])

    ops, masks, st0 = [], [], []
    for g, a in chains:
        n = (a + 1) * SUB
        rr = lax.broadcasted_iota(jnp.int32, (SUB, n), 0)
        cc = lax.broadcasted_iota(jnp.int32, (SUB, n), 1)
        ops.append(operands(g, a, start, n))
        masks.append(cc <= rr + a * SUB)
        st0.append((jnp.full((SUB, 1), -jnp.inf, F32), jnp.zeros((SUB, 1), F32),
                    jnp.zeros((SUB, MLA_V), F32)))
    state = _mla_chains(ops, st0, masks)

    def step(j, st):
        row0 = pl.multiple_of(j * t, t)
        return _mla_chains([operands(g, a, row0, t) for g, a in chains], st, [None] * len(chains))

    state = lax.fori_loop(0, i, step, state)
    for n, (g, a) in enumerate(chains):
        m, l, acc = state[n]
        o_ref[a * SUB:(a + 1) * SUB, g * MLA_V:(g + 1) * MLA_V] = (acc / l).astype(o_ref.dtype)


def _mla_attention(q_full, kv, kpe, B, S):
    t = _tile(S, 2 * SUB)
    nq = S // t
    G = ATT_HEADS
    HG = MLA_HEADS // G
    return _call(
        functools.partial(_mla_body, t=t), grid=(B, HG, nq),
        in_specs=[pl.BlockSpec((t, G * QPAD), lambda b, h, i: (b * nq + i, h)),
                  pl.BlockSpec((S, G * (MLA_NOPE + MLA_V)), lambda b, h, i: (b, h)),
                  pl.BlockSpec((S, LANE), lambda b, h, i: (b, 0))],
        out_specs=pl.BlockSpec((t, G * MLA_V), lambda b, h, i: (b * nq + i, h)),
        out_shape=jax.ShapeDtypeStruct((B * S, MLA_HEADS * MLA_V), BF16),
        scratch=[pltpu.VMEM((G, S, QPAD), BF16)],
        name="mla_attention")(q_full, kv, kpe)


def _merge_body(osb_ref, omla_ref, gate_sb_ref, gate_mla_ref, wsb_ref, wmla_ref, wo_ref, x_ref,
                g1_ref, gain_ref, sc_ref, sh_ref, xo_ref, ho_ref):
    y = (gate_sb_ref[...].astype(F32)
         * jnp.dot(osb_ref[...], wsb_ref[...], preferred_element_type=F32)
         + gate_mla_ref[...].astype(F32)
         * jnp.dot(omla_ref[...], wmla_ref[...], preferred_element_type=F32))
    out = jnp.dot(y.astype(BF16), wo_ref[...], preferred_element_type=F32)
    x_new = x_ref[...] + g1_ref[...] * out
    xo_ref[...] = x_new
    ho_ref[...] = _norm_mod(x_new, gain_ref[...], sc_ref[...], sh_ref[...]).astype(ho_ref.dtype)


def _merge(o_sb, o_mla, gates, w_sb, w_mla, w_o, layer, x, g1, gain, sc, sh, S, h_dtype):
    M, D = x.shape
    tm = _tile(S, 256)
    bpt = S // tm
    vec = pl.BlockSpec((None, 1, D), lambda m: (m // bpt, 0, 0))
    res = lambda k: pl.BlockSpec((None, k, D), lambda m: (layer, 0, 0),
                                 pipeline_mode=pl.Buffered(1))
    return _call(
        _merge_body, grid=(M // tm,),
        in_specs=[pl.BlockSpec((tm, SB_WIDTH), lambda m: (m, 0)),
                  pl.BlockSpec((tm, MLA_HEADS * MLA_V), lambda m: (m, 0)),
                  pl.BlockSpec((tm, D), lambda m: (m, 0)),
                  pl.BlockSpec((tm, D), lambda m: (m, 1)),
                  res(SB_WIDTH), res(MLA_HEADS * MLA_V), res(D),
                  pl.BlockSpec((tm, D), lambda m: (m, 0)),
                  vec, pl.BlockSpec((1, D), lambda m: (0, 0)), vec, vec],
        out_specs=[pl.BlockSpec((tm, D), lambda m: (m, 0)),
                   pl.BlockSpec((tm, D), lambda m: (m, 0))],
        out_shape=[jax.ShapeDtypeStruct((M, D), F32), jax.ShapeDtypeStruct((M, D), h_dtype)],
        name="merge_out")(o_sb, o_mla, gates, gates, w_sb, w_mla, w_o, x, g1, gain, sc, sh)


def _resnorm_body(f_ref, x_ref, g_ref, gain_ref, sc_ref, sh_ref, xo_ref, ho_ref):
    x_new = x_ref[...] + g_ref[...] * f_ref[...]
    xo_ref[...] = x_new
    ho_ref[...] = _norm_mod(x_new, gain_ref[...], sc_ref[...], sh_ref[...]).astype(ho_ref.dtype)


def _resnorm(f, x, g, gain, sc, sh, S, h_dtype):
    M, D = x.shape
    tm = _tile(S, 512)
    bpt = S // tm
    vec = pl.BlockSpec((None, 1, D), lambda m: (m // bpt, 0, 0))
    rows = pl.BlockSpec((tm, D), lambda m: (m, 0))
    return _call(
        _resnorm_body, grid=(M // tm,),
        in_specs=[rows, rows, vec, pl.BlockSpec((1, D), lambda m: (0, 0)), vec, vec],
        out_specs=[rows, rows],
        out_shape=[jax.ShapeDtypeStruct((M, D), F32), jax.ShapeDtypeStruct((M, D), h_dtype)],
        name="resnorm")(f, x, g, gain, sc, sh)


def _split_bf16(v):
    hi = v.astype(BF16)
    return hi, (v - hi.astype(F32)).astype(BF16)


def _router_body(h_ref, w_ref, idx_ref, wt_ref):
    hh, hl = _split_bf16(h_ref[...])
    wh, wl = _split_bf16(w_ref[...])
    dot = functools.partial(jnp.dot, preferred_element_type=F32)
    logits = dot(hh, wh) + (dot(hl, wh) + dot(hh, wl))
    lane = lax.broadcasted_iota(jnp.int32, logits.shape, 1).astype(F32)
    logits = jnp.where(lane < N_EXPERTS, logits, -jnp.inf)
    m1 = jnp.max(logits, axis=-1, keepdims=True)
    i1 = jnp.min(jnp.where(logits == m1, lane, float(LANE)), axis=-1, keepdims=True)
    rest = jnp.where(lane == i1, -jnp.inf, logits)
    m2 = jnp.max(rest, axis=-1, keepdims=True)
    i2 = jnp.min(jnp.where(rest == m2, lane, float(LANE)), axis=-1, keepdims=True)
    e = jnp.exp(m2 - m1)
    w1 = 1.0 / (1.0 + e)
    w2 = e * w1
    idx_ref[...] = jnp.where(lane == 0.0, i1, jnp.where(lane == 1.0, i2, 0.0)).astype(jnp.int32)
    wt_ref[...] = jnp.where(lane == 0.0, w1, jnp.where(lane == 1.0, w2, 0.0))


def _router(h, w_router_pad, j):
    M, D = h.shape
    tm = _tile(M, 512)
    return _call(
        _router_body, grid=(M // tm,),
        in_specs=[pl.BlockSpec((tm, D), lambda m: (m, 0)),
                  pl.BlockSpec((None, D, LANE), lambda m: (j, 0, 0))],
        out_specs=[pl.BlockSpec((tm, LANE), lambda m: (m, 0)),
                   pl.BlockSpec((tm, LANE), lambda m: (m, 0))],
        out_shape=[jax.ShapeDtypeStruct((M, LANE), jnp.int32),
                   jax.ShapeDtypeStruct((M, LANE), F32)],
        name="router")(h, w_router_pad)


MOE_SUB = 256
ROW_UNROLL = 8


def _gather_body(src_ref, h_hbm, o_ref, buf_ref, sem, *, tm):
    t = pl.program_id(0)

    def issue(tile):
        slot = tile % 2

        def blk(b, _):
            for u in range(ROW_UNROLL):
                r = b * ROW_UNROLL + u
                pltpu.make_async_copy(h_hbm.at[pl.ds(src_ref[tile * tm + r], 1)],
                                      buf_ref.at[slot, pl.ds(r, 1)],
                                      sem.at[slot]).start(priority=u % 2)
            return 0

        lax.fori_loop(0, tm // ROW_UNROLL, blk, 0)

    n_active = src_ref[pl.num_programs(0) * tm]

    @pl.when(t == 0)
    def _():
        issue(t)

    @pl.when(t + 1 < n_active)
    def _():
        issue(t + 1)

    @pl.when(t < n_active)
    def _():
        slot = t % 2
        pltpu.make_async_copy(h_hbm.at[pl.ds(0, tm)], buf_ref.at[slot], sem.at[slot]).wait()
        o_ref[...] = buf_ref[slot].astype(o_ref.dtype)

    @pl.when(t >= n_active)
    def _():
        o_ref[...] = jnp.zeros_like(o_ref)


def _gather_rows(src_tok, h, n_tiles, tm):
    M, D = h.shape
    return _call(
        functools.partial(_gather_body, tm=tm), grid=(n_tiles,), nsp=1,
        in_specs=[pl.BlockSpec(memory_space=pl.ANY)],
        out_specs=pl.BlockSpec((tm, D), lambda t, src: (t, 0)),
        out_shape=jax.ShapeDtypeStruct((n_tiles * tm, D), BF16),
        scratch=[pltpu.VMEM((2, tm, D), F32), pltpu.SemaphoreType.DMA((2,))],
        name="moe_gather")(src_tok, h)


def _grouped_body(te_ref, a_ref, *refs, n_w, layer, epilogue):
    w_hbm = refs[:n_w]
    n_rest = len(refs) - 3 * n_w - 1
    rest = refs[n_w:n_w + n_rest]
    stage = refs[n_w + n_rest:2 * n_w + n_rest]
    wb_refs = refs[2 * n_w + n_rest:3 * n_w + n_rest]
    sem = refs[-1]
    c, t = pl.program_id(0), pl.program_id(1)
    nc, nt = pl.num_programs(0), pl.num_programs(1)
    expert = te_ref[t]
    n_valid = te_ref[nt + t]
    nxt = te_ref[2 * nt + t]
    changed = jnp.logical_or(t == 0, expert != te_ref[jnp.maximum(t - 1, 0)])
    tn = stage[0].shape[1]

    def block_copy(i, chunk, e):
        col0 = pl.multiple_of(chunk * tn, tn)
        return pltpu.make_async_copy(w_hbm[i].at[layer, e, :, pl.ds(col0, tn)], stage[i],
                                     sem.at[i])

    @pl.when(changed)
    def _():
        @pl.when(jnp.logical_and(c == 0, t == 0))
        def _():
            for i in range(n_w):
                block_copy(i, c, expert).start()

        for i in range(n_w):
            block_copy(i, c, expert).wait()
            wb_refs[i][...] = stage[i][...].astype(BF16)

        @pl.when(nxt >= 0)
        def _():
            for i in range(n_w):
                block_copy(i, c, nxt).start()

        @pl.when(jnp.logical_and(nxt < 0, c + 1 < nc))
        def _():
            for i in range(n_w):
                block_copy(i, c + 1, te_ref[0]).start()

    for s in range(a_ref.shape[0] // MOE_SUB):
        rows = slice(s * MOE_SUB, (s + 1) * MOE_SUB)

        @pl.when(s * MOE_SUB < n_valid)
        def _():
            a = a_ref[rows, :]
            accs = [jnp.dot(a, wb_ref[...], preferred_element_type=F32) for wb_ref in wb_refs]
            epilogue(accs, rest[:n_rest], rows)

        @pl.when(s * MOE_SUB >= n_valid)
        def _():
            for o_ref in rest[:n_rest]:
                o_ref[rows, :] = jnp.zeros((MOE_SUB, o_ref.shape[1]), o_ref.dtype)


def _up_epilogue(accs, refs, rows=slice(None)):
    g, u = accs
    refs[0][rows, :] = (g * jax.nn.sigmoid(g) * u).astype(refs[0].dtype)


def _grouped_up(tile_e, xs, w_gate, w_up, j, tm, tf):
    NP, D = xs.shape
    F = w_gate.shape[-1]
    hbm = pl.BlockSpec(memory_space=pl.ANY)
    return _call(
        functools.partial(_grouped_body, n_w=2, layer=j, epilogue=_up_epilogue),
        grid=(F // tf, NP // tm), nsp=1,
        in_specs=[pl.BlockSpec((tm, D), lambda f, t, te: (t, 0)), hbm, hbm],
        out_specs=pl.BlockSpec((tm, tf), lambda f, t, te: (t, f)),
        out_shape=jax.ShapeDtypeStruct((NP, F), BF16),
        scratch=[pltpu.VMEM((D, tf), F32), pltpu.VMEM((D, tf), F32),
                 pltpu.VMEM((D, tf), BF16), pltpu.VMEM((D, tf), BF16),
                 pltpu.SemaphoreType.DMA((2,))],
        name="moe_up")(tile_e, xs, w_gate, w_up)


def _down_epilogue(accs, refs, rows):
    refs[0][rows, :] = accs[0]


def _grouped_down(tile_e, hs, w_down, j, tm, tn):
    NP, F = hs.shape
    D = w_down.shape[-1]
    return _call(
        functools.partial(_grouped_body, n_w=1, layer=j, epilogue=_down_epilogue),
        grid=(D // tn, NP // tm), nsp=1,
        in_specs=[pl.BlockSpec((tm, F), lambda n, t, te: (t, 0)),
                  pl.BlockSpec(memory_space=pl.ANY)],
        out_specs=pl.BlockSpec((tm, tn), lambda n, t, te: (t, n)),
        out_shape=jax.ShapeDtypeStruct((NP, D), F32),
        scratch=[pltpu.VMEM((F, tn), F32), pltpu.VMEM((F, tn), BF16),
                 pltpu.SemaphoreType.DMA((1,))],
        name="moe_down")(tile_e, hs, w_down)


def _combine_body(pos_ref, y_hbm, wt_ref, x_ref, g2_ref, gain_ref, sc_ref, sh_ref, xo_ref, ho_ref,
                  buf_ref, sem, *, tm):
    t = pl.program_id(0)

    def issue(tile):
        slot = tile % 2

        def blk(b, _):
            for u in range(ROW_UNROLL):
                r = b * ROW_UNROLL + u
                for k in range(TOP_K):
                    pltpu.make_async_copy(
                        y_hbm.at[pl.ds(pos_ref[(tile * tm + r) * TOP_K + k], 1)],
                        buf_ref.at[slot, k, pl.ds(r, 1)], sem.at[slot]).start(priority=k % 2)
            return 0

        lax.fori_loop(0, tm // ROW_UNROLL, blk, 0)

    @pl.when(t == 0)
    def _():
        issue(t)

    @pl.when(t + 1 < pl.num_programs(0))
    def _():
        issue(t + 1)

    slot = t % 2
    for k in range(TOP_K):
        pltpu.make_async_copy(y_hbm.at[pl.ds(0, tm)], buf_ref.at[slot, k], sem.at[slot]).wait()
    wt = wt_ref[...]
    f = buf_ref[slot, 0] * wt[:, 0:1]
    for k in range(1, TOP_K):
        f = f + buf_ref[slot, k] * wt[:, k:k + 1]
    x_new = x_ref[...] + g2_ref[...] * f
    xo_ref[...] = x_new
    ho_ref[...] = _norm_mod(x_new, gain_ref[...], sc_ref[...], sh_ref[...]).astype(ho_ref.dtype)


def _combine(pos, y_sorted, wts, x, g2, gain, sc, sh, S, h_dtype):
    M, D = x.shape
    tm = _tile(S, 256)
    bpt = S // tm
    vec = pl.BlockSpec((None, 1, D), lambda m, p: (m // bpt, 0, 0))
    return _call(
        functools.partial(_combine_body, tm=tm), grid=(M // tm,), nsp=1,
        in_specs=[pl.BlockSpec(memory_space=pl.ANY),
                  pl.BlockSpec((tm, LANE), lambda m, p: (m, 0)),
                  pl.BlockSpec((tm, D), lambda m, p: (m, 0)),
                  vec, pl.BlockSpec((1, D), lambda m, p: (0, 0)), vec, vec],
        out_specs=[pl.BlockSpec((tm, D), lambda m, p: (m, 0)),
                   pl.BlockSpec((tm, D), lambda m, p: (m, 0))],
        out_shape=[jax.ShapeDtypeStruct((M, D), F32), jax.ShapeDtypeStruct((M, D), h_dtype)],
        scratch=[pltpu.VMEM((2, TOP_K, tm, D), F32), pltpu.SemaphoreType.DMA((2,))],
        name="moe_combine")(pos, y_sorted, wts, x, g2, gain, sc, sh)


def _dispatch_plan(top_i, tm):
    M = top_i.shape[0]
    E = N_EXPERTS
    n_ent = M * TOP_K
    e_flat = top_i.reshape(-1)
    onehot = (e_flat[:, None] == jnp.arange(E, dtype=jnp.int32)[None, :]).astype(jnp.int32)
    csum = jnp.cumsum(onehot, axis=0)
    rank = jnp.sum(onehot * csum, axis=1) - 1
    counts = csum[-1]
    pcounts = ((counts + tm - 1) // tm) * tm
    pend = jnp.cumsum(pcounts)
    pstart = pend - pcounts
    dest = (pstart[e_flat] + rank).astype(jnp.int32)
    n_tiles = n_ent // tm + E
    n_rows = n_tiles * tm
    src_tok = (jnp.arange(n_rows, dtype=jnp.int32) % M).at[dest].set(
        jnp.arange(n_ent, dtype=jnp.int32) // TOP_K)
    tile_start = jnp.arange(n_tiles, dtype=jnp.int32) * tm
    tile_e = jnp.sum((tile_start[:, None] >= pend[None, :]).astype(jnp.int32), axis=1)
    tile_e = jnp.minimum(tile_e, E - 1).astype(jnp.int32)
    n_valid = jnp.clip(pstart[tile_e] + counts[tile_e] - tile_start, 0, tm).astype(jnp.int32)
    n_active = pend[-1] // tm
    tile_e = jnp.where(tile_start < pend[-1], tile_e, tile_e[n_active - 1])
    tidx = jnp.arange(n_tiles, dtype=jnp.int32)
    later_other = (tidx[None, :] > tidx[:, None]) & (tile_e[None, :] != tile_e[:, None])
    nxt = jnp.where(jnp.any(later_other, axis=1), tile_e[jnp.argmax(later_other, axis=1)], -1)
    src_tok = jnp.concatenate([src_tok, n_active[None].astype(jnp.int32)])
    return src_tok, dest, jnp.concatenate([tile_e, n_valid, nxt]).astype(jnp.int32), n_tiles


def _rot_cols(w):
    half = MLA_ROPE // 2
    return jnp.concatenate([-w[..., half:], w[..., :half]], axis=-1)


def _rope_tables(positions):
    inv_freq = 1.0 / (ROPE_THETA ** (jnp.arange(0, MLA_ROPE, 2, dtype=F32) / MLA_ROPE))
    ang = positions.astype(F32)[..., None] * inv_freq
    cos, sin = jnp.cos(ang), jnp.sin(ang)
    M = positions.size
    cos2 = jnp.concatenate([cos, cos], axis=-1).reshape(M, MLA_ROPE)
    sin2 = jnp.concatenate([sin, sin], axis=-1).reshape(M, MLA_ROPE)
    zeros = jnp.zeros((M, LANE - MLA_ROPE), F32)
    cos_k = jnp.concatenate([cos2, zeros], axis=-1)
    sin_k = jnp.concatenate([sin2, zeros], axis=-1)
    return cos_k, sin_k, cos_k * MLA_SCALE, sin_k * MLA_SCALE


def kernel(x, c, positions, w_ada, b_ada, norm_mix_g, norm_ffn_g, w_in, q_norm_g, kv_norm_g, w_uq, w_ukv, w_sb_up, w_mla_up, w_o, w_ffn_gate, w_ffn_up, w_ffn_down, w_router, w_exp_gate, w_exp_up, w_exp_down, final_norm_g):
    B, S, D = x.shape
    L = w_ada.shape[0]
    M = B * S
    x = x.reshape(M, D)

    mod = _modulation(c, w_ada, b_ada)
    mod = mod.reshape(L, B, N_MOD, 1, D)
    sh1, sc1, g1, sh2, sc2, g2 = [mod[:, :, k] for k in range(N_MOD)]
    cos_k, sin_k, cos_q, sin_q = _rope_tables(positions)

    pe0 = 3 * SB_WIDTH + Q_RANK + KV_RANK
    w_in_t = jnp.swapaxes(w_in, 1, 2)
    w_pe = w_in_t[:, pe0:pe0 + MLA_ROPE]
    zpad = jnp.zeros((L, LANE - MLA_ROPE, D), F32)
    half = MLA_ROPE // 2
    w_pe_rot = jnp.concatenate([-w_pe[:, half:], w_pe[:, :half]], axis=1)
    w_pe2 = jnp.concatenate([w_pe, zpad, w_pe_rot, zpad], axis=1)
    c_gain = jnp.concatenate([q_norm_g, kv_norm_g], axis=-1)[:, None, :]

    wq = w_uq.reshape(L, Q_RANK, MLA_HEADS, MLA_QK)
    wq_n, wq_r = wq[..., :MLA_NOPE], wq[..., MLA_NOPE:]
    zq = jnp.zeros(wq_r.shape, F32)
    wq_main = jnp.concatenate([wq_n, wq_r, zq], axis=-1).reshape(
        L, Q_RANK, MLA_HEADS * QPAD).astype(BF16)
    wq_rot = jnp.concatenate([_rot_cols(wq_r), zq], axis=-1).reshape(
        L, Q_RANK, MLA_HEADS * LANE).astype(BF16)

    w_sb_b, w_mla_b, w_o_b = (w.astype(BF16) for w in (w_sb_up, w_mla_up, w_o))
    w_router_pad = jnp.concatenate(
        [w_router, jnp.zeros(w_router.shape[:2] + (LANE - N_EXPERTS,), F32)], axis=-1)

    tm = _tile(S, 1024)
    bpt = S // tm
    row = lambda width: pl.BlockSpec((tm, width), lambda n, m: (m, 0))
    sb_scale = HEAD_DIM ** -0.5 * LOG2E

    def h_dtype_for(layer_is_moe):
        return F32 if layer_is_moe else BF16

    h = _prenorm(x, norm_mix_g[0][None, :], sc1[0], sh1[0], S, BF16)

    for l in range(L):
        tn = _tile(min(SB_WIDTH, 2 * D), 1024)
        nq_blk = SB_WIDTH // tn

        def qkv_epi(accs, n, extra, outs):
            s = jnp.where(n < nq_blk, sb_scale, 1.0)
            outs[0][...] = (accs[0] * s).astype(BF16)

        (qkv,) = _matmul(h, [w_in_t], layer=l, col_blk0=0, n_cols=3 * SB_WIDTH, tn=tn, tm=tm,
                         epilogue=qkv_epi, transposed=True,
                         out_shape=[jax.ShapeDtypeStruct((M, 3 * SB_WIDTH), BF16)],
                         out_specs=[pl.BlockSpec((tm, tn), lambda n, m: (m, n))], name="in_qkv")

        lat, kpe = _latent_in(h, w_in_t, w_pe2, c_gain, cos_k, sin_k, l, tm)

        def gate_epi(accs, n, extra, outs):
            outs[0][...] = jax.nn.sigmoid(accs[0]).astype(BF16)

        gates = _matmul_rows(h, w_in_t, layer=l, row0=pe0 + MLA_ROPE, n_cols=2 * D, tn=tn, tm=tm,
                             epilogue=gate_epi, out_dtype=BF16, name="in_gates")

        q_full, kv = _latent_up(lat, wq_main, wq_rot, w_ukv, l, cos_q, sin_q, tm)

        o_sb = _sb_attention(qkv, B, S)
        o_mla = _mla_attention(q_full, kv, kpe, B, S)

        moe = (l % 2 == 1)
        x, h = _merge(o_sb, o_mla, gates, w_sb_b, w_mla_b, w_o_b, l, x, g1[l],
                      norm_ffn_g[l][None, :], sc2[l], sh2[l], S, h_dtype_for(moe))

        last = (l == L - 1)
        if last:
            n_gain = final_norm_g[None, :]
            n_sc = jnp.zeros_like(sc1[0])
            n_sh = jnp.zeros_like(sh1[0])
            n_dtype = F32
        else:
            n_gain, n_sc, n_sh, n_dtype = norm_mix_g[l + 1][None, :], sc1[l + 1], sh1[l + 1], BF16
        j = l // 2
        if not moe:
            F = w_ffn_gate.shape[-1]
            tf = _tile(F, 512)

            def up_epi(accs, n, extra, outs):
                _up_epilogue(accs, outs)

            (hmid,) = _matmul(h, [w_ffn_gate, w_ffn_up], layer=j, col_blk0=0, n_cols=F, tn=tf,
                              tm=tm, epilogue=up_epi,
                              out_shape=[jax.ShapeDtypeStruct((M, F), BF16)],
                              out_specs=[pl.BlockSpec((tm, tf), lambda n, m: (m, n))],
                              name="ffn_up")
            def down_epi(accs, n, extra, outs):
                outs[0][...] = accs[0]

            tnd = _tile(D, 512)
            (f,) = _matmul(hmid, [w_ffn_down], layer=j, col_blk0=0, n_cols=D, tn=tnd,
                           tm=_tile(S, 512), epilogue=down_epi,
                           out_shape=[jax.ShapeDtypeStruct((M, D), F32)],
                           out_specs=[pl.BlockSpec((_tile(S, 512), tnd), lambda n, m: (m, n))],
                           name="ffn_down")
            x, h = _resnorm(f, x, g2[l], n_gain, n_sc, n_sh, S, n_dtype)
        else:
            F = w_exp_gate.shape[-1]
            tme = max(_tile(M, 2 * MOE_SUB), MOE_SUB)
            idx, wts = _router(h, w_router_pad, j)
            src_tok, dest, tile_e, n_tiles = _dispatch_plan(idx[:, :TOP_K], tme)
            xs = _gather_rows(src_tok, h, n_tiles, tme)
            tfe = F // 4 if F % (4 * LANE) == 0 else _tile(F, 512)
            hs = _grouped_up(tile_e, xs, w_exp_gate, w_exp_up, j, tme, tfe)
            ys = _grouped_down(tile_e, hs, w_exp_down, j, tme, _tile(D, 512))
            x, h = _combine(dest, ys, wts, x, g2[l], n_gain, n_sc, n_sh, S, n_dtype)

    return h.reshape(B, S, D)
```

```python
import functools

import jax
import jax.numpy as jnp
from jax import lax
from jax.experimental import pallas as pl
from jax.experimental.pallas import tpu as pltpu

BF16 = jnp.bfloat16
F32 = jnp.float32

SB_HEADS = 8
HEAD_DIM = 128
SB_WIDTH = SB_HEADS * HEAD_DIM
MLA_HEADS = 8
MLA_NOPE = 128
MLA_ROPE = 64
MLA_V = 128
MLA_QK = MLA_NOPE + MLA_ROPE
Q_RANK = 512
KV_RANK = 512
ROPE_THETA = 10000.0
N_EXPERTS = 8
TOP_K = 2
N_MOD = 6
NORM_EPS = 1e-6

LOG2E = 1.4426950408889634
MLA_SCALE = MLA_QK ** -0.5 * LOG2E

LANE = 128
QPAD = 2 * LANE
VMEM_LIMIT = 56 << 20


def _call(body, *, grid, in_specs, out_specs, out_shape, scratch=(), nsp=0, name, flags=None):
    return pl.pallas_call(
        body,
        grid_spec=pltpu.PrefetchScalarGridSpec(
            num_scalar_prefetch=nsp, grid=grid, in_specs=in_specs,
            out_specs=out_specs, scratch_shapes=scratch),
        out_shape=out_shape,
        compiler_params=pltpu.CompilerParams(
            dimension_semantics=("arbitrary",) * len(grid),
            vmem_limit_bytes=VMEM_LIMIT, flags=flags),
        name=name)


def _tile(n, pref):
    t = min(n, pref)
    while n % t:
        t //= 2
    return t


def _norm_mod(x, gain, sc, sh):
    ms = jnp.mean(x * x, axis=-1, keepdims=True)
    y = x * lax.rsqrt(ms + NORM_EPS)
    return (y * gain) * (1.0 + sc) + sh


def _mm_body(*refs, n_w, n_extra, n_out, epilogue, transposed):
    a_ref = refs[0]
    w_refs = refs[1:1 + n_w]
    extra = refs[1 + n_w:1 + n_w + n_extra]
    outs = refs[1 + n_w + n_extra:1 + n_w + n_extra + n_out]
    wb_refs = refs[1 + n_w + n_extra + n_out:]

    @pl.when(pl.program_id(1) == 0)
    def _():
        for w_ref, wb_ref in zip(w_refs, wb_refs):
            wb_ref[...] = w_ref[...].astype(BF16)

    a = a_ref[...]
    dims = (((1,), (1,)), ((), ())) if transposed else (((1,), (0,)), ((), ()))
    accs = [lax.dot_general(a, wb_ref[...], dims, preferred_element_type=F32)
            for wb_ref in wb_refs]
    epilogue(accs, pl.program_id(0), extra, outs)


def _matmul(a, ws, *, layer, col_blk0, n_cols, tn, tm, epilogue, extras=(), extra_specs=(),
            out_shape, out_specs, name, a_blk=0, transposed=False):
    M = a.shape[0]
    K = ws[0].shape[2 if transposed else 1]
    grid = (n_cols // tn, M // tm)
    in_specs = [pl.BlockSpec((tm, K), lambda n, m: (m, a_blk))]
    for _ in ws:
        if transposed:
            in_specs.append(pl.BlockSpec((None, tn, K), lambda n, m: (layer, n + col_blk0, 0)))
        else:
            in_specs.append(pl.BlockSpec((None, K, tn), lambda n, m: (layer, 0, n + col_blk0)))
    in_specs += list(extra_specs)
    body = functools.partial(_mm_body, n_w=len(ws), n_extra=len(extras), n_out=len(out_shape),
                             epilogue=epilogue, transposed=transposed)
    wb_shape = (tn, K) if transposed else (K, tn)
    return _call(body, grid=grid, in_specs=in_specs, out_specs=out_specs, out_shape=out_shape,
                 scratch=[pltpu.VMEM(wb_shape, BF16) for _ in ws], name=name)(a, *ws, *extras)


def _mm_rows_body(a_ref, w_hbm, o_ref, stage_ref, wb_ref, sem, *, layer, row0, epilogue):
    n, m = pl.program_id(0), pl.program_id(1)
    tn = stage_ref.shape[0]

    def chunk_copy(chunk):
        r0 = pl.multiple_of(row0 + chunk * tn, 8)
        return pltpu.make_async_copy(w_hbm.at[layer, pl.ds(r0, tn), :], stage_ref, sem.at[0])

    @pl.when(m == 0)
    def _():
        @pl.when(n == 0)
        def _():
            chunk_copy(n).start()

        chunk_copy(n).wait()
        wb_ref[...] = stage_ref[...].astype(BF16)

        @pl.when(n + 1 < pl.num_programs(0))
        def _():
            chunk_copy(n + 1).start()

    acc = lax.dot_general(a_ref[...], wb_ref[...], (((1,), (1,)), ((), ())),
                          preferred_element_type=F32)
    epilogue([acc], n, (), [o_ref])


def _matmul_rows(a, w_t, *, layer, row0, n_cols, tn, tm, epilogue, out_dtype, name):
    M, K = a.shape
    body = functools.partial(_mm_rows_body, layer=layer, row0=row0, epilogue=epilogue)
    return _call(body, grid=(n_cols // tn, M // tm),
                 in_specs=[pl.BlockSpec((tm, K), lambda n, m: (m, 0)),
                           pl.BlockSpec(memory_space=pl.ANY)],
                 out_specs=pl.BlockSpec((tm, tn), lambda n, m: (m, n)),
                 out_shape=jax.ShapeDtypeStruct((M, n_cols), out_dtype),
                 scratch=[pltpu.VMEM((tn, K), F32), pltpu.VMEM((tn, K), BF16),
                          pltpu.SemaphoreType.DMA((1,))],
                 name=name)(a, w_t)


def _mod_body(c_ref, w_ref, b_ref, o_ref):
    c = c_ref[...]
    c_act = (c * jax.nn.sigmoid(c)).astype(BF16)
    o_ref[...] = jnp.dot(c_act, w_ref[...].astype(BF16), preferred_element_type=F32) + b_ref[...]


def _modulation(c, w_ada, b_ada):
    L, D, N = w_ada.shape
    B = c.shape[0]
    rows = 8
    c_pad = jnp.zeros((rows, D), F32).at[:B].set(c)
    tn = _tile(N, 1024)
    out = _call(
        _mod_body, grid=(L, N // tn),
        in_specs=[pl.BlockSpec((rows, D), lambda l, n: (0, 0)),
                  pl.BlockSpec((None, D, tn), lambda l, n: (l, 0, n)),
                  pl.BlockSpec((None, 1, tn), lambda l, n: (l, 0, n))],
        out_specs=pl.BlockSpec((None, rows, tn), lambda l, n: (l, 0, n)),
        out_shape=jax.ShapeDtypeStruct((L, rows, N), F32),
        name="adaln_mod")(c_pad, w_ada, b_ada.reshape(L, 1, N))
    return out[:, :B, :]


def _latent_in_body(h_ref, wl_ref, wpe_ref, gain_ref, cos_ref, sin_ref, lat_ref, kpe_ref,
                    wlb_ref, wpeb_ref):
    @pl.when(pl.program_id(0) == 0)
    def _():
        wlb_ref[...] = wl_ref[...].astype(BF16)
        wpeb_ref[...] = wpe_ref[...].astype(BF16)

    h = h_ref[...]
    nt = (((1,), (1,)), ((), ()))
    c = lax.dot_general(h, wlb_ref[...], nt, preferred_element_type=F32)
    for k0 in range(0, c.shape[1], Q_RANK):
        a = c[:, k0:k0 + Q_RANK]
        ms = jnp.mean(a * a, axis=-1, keepdims=True)
        lat_ref[:, k0:k0 + Q_RANK] = (a * lax.rsqrt(ms + NORM_EPS)
                                      * gain_ref[:, k0:k0 + Q_RANK]).astype(lat_ref.dtype)
    p = lax.dot_general(h, wpeb_ref[...], nt, preferred_element_type=F32)
    kpe_ref[...] = (p[:, :LANE] * cos_ref[...] + p[:, LANE:] * sin_ref[...]).astype(kpe_ref.dtype)


def _latent_in(h, w_in_t, w_pe2, c_gain, cos_k, sin_k, layer, tm):
    M, D = h.shape
    n_lat = Q_RANK + KV_RANK
    lat_blk = 3 * SB_WIDTH // n_lat
    rows = lambda n: pl.BlockSpec((tm, n), lambda m: (m, 0))
    return _call(
        _latent_in_body, grid=(M // tm,),
        in_specs=[rows(D),
                  pl.BlockSpec((None, n_lat, D), lambda m: (layer, lat_blk, 0)),
                  pl.BlockSpec((None, 2 * LANE, D), lambda m: (layer, 0, 0)),
                  pl.BlockSpec((None, 1, n_lat), lambda m: (layer, 0, 0)),
                  rows(LANE), rows(LANE)],
        out_specs=[rows(n_lat), rows(LANE)],
        out_shape=[jax.ShapeDtypeStruct((M, n_lat), BF16), jax.ShapeDtypeStruct((M, LANE), BF16)],
        scratch=[pltpu.VMEM((n_lat, D), BF16), pltpu.VMEM((2 * LANE, D), BF16)],
        name="latent_in")(h, w_in_t, w_pe2, c_gain, cos_k, sin_k)


def _latent_up_body(lat_ref, wqm_ref, wqr_ref, wkv_ref, cos_ref, sin_ref, q_ref, kv_ref, wkvb_ref):
    @pl.when(pl.program_id(0) == 0)
    def _():
        wkvb_ref[...] = wkv_ref[...].astype(BF16)

    cq = lat_ref[:, :Q_RANK]
    main = jnp.dot(cq, wqm_ref[...], preferred_element_type=F32)
    rot = jnp.dot(cq, wqr_ref[...], preferred_element_type=F32)
    cos, sin = cos_ref[...], sin_ref[...]
    for h in range(MLA_HEADS):
        c0 = h * QPAD
        q_ref[:, c0:c0 + LANE] = (main[:, c0:c0 + LANE] * MLA_SCALE).astype(q_ref.dtype)
        q_ref[:, c0 + LANE:c0 + QPAD] = (main[:, c0 + LANE:c0 + QPAD] * cos
                                         + rot[:, h * LANE:(h + 1) * LANE] * sin).astype(q_ref.dtype)
    kv_ref[...] = jnp.dot(lat_ref[:, Q_RANK:], wkvb_ref[...],
                          preferred_element_type=F32).astype(kv_ref.dtype)


def _latent_up(lat, wq_main, wq_rot, w_ukv, layer, cos_q, sin_q, tm):
    M = lat.shape[0]
    nq, nr, nkv = wq_main.shape[-1], wq_rot.shape[-1], w_ukv.shape[-1]
    whole = lambda k, n: pl.BlockSpec((None, k, n), lambda m: (layer, 0, 0))
    rows = lambda n: pl.BlockSpec((tm, n), lambda m: (m, 0))
    return _call(
        _latent_up_body, grid=(M // tm,),
        in_specs=[rows(Q_RANK + KV_RANK), whole(Q_RANK, nq), whole(Q_RANK, nr),
                  whole(KV_RANK, nkv), rows(LANE), rows(LANE)],
        out_specs=[rows(nq), rows(nkv)],
        out_shape=[jax.ShapeDtypeStruct((M, nq), BF16), jax.ShapeDtypeStruct((M, nkv), BF16)],
        scratch=[pltpu.VMEM((KV_RANK, nkv), BF16)],
        name="latent_up")(lat, wq_main, wq_rot, w_ukv, cos_q, sin_q)


def _prenorm_body(x_ref, g_ref, sc_ref, sh_ref, o_ref):
    gain, sc, sh = g_ref[...], sc_ref[...], sh_ref[...]

    def chunk(i, _):
        rows = pl.ds(pl.multiple_of(i * NORM_ROWS, NORM_ROWS), NORM_ROWS)
        o_ref[rows, :] = _norm_mod(x_ref[rows, :], gain, sc, sh).astype(o_ref.dtype)
        return 0

    lax.fori_loop(0, x_ref.shape[0] // NORM_ROWS, chunk, 0)


def _prenorm(x, gain, sc, sh, S, out_dtype):
    M, D = x.shape
    tm = _tile(S, 512)
    bpt = S // tm
    vec = pl.BlockSpec((None, 1, D), lambda m: (m // bpt, 0, 0))
    return _call(
        _prenorm_body, grid=(M // tm,),
        in_specs=[pl.BlockSpec((tm, D), lambda m: (m, 0)),
                  pl.BlockSpec((1, D), lambda m: (0, 0)), vec, vec],
        out_specs=pl.BlockSpec((tm, D), lambda m: (m, 0)),
        out_shape=jax.ShapeDtypeStruct((M, D), out_dtype),
        name="prenorm")(x, gain, sc, sh)


SUB = 256
ATT_HEADS = 2


def _qk(q, k):
    return lax.dot_general(q, k, (((1,), (1,)), ((), ())), preferred_element_type=F32)


def _sb_chains(ops, tri, states, masks):
    zs = [_qk(q, k) for q, k, v in ops]
    sps = []
    for z, mask in zip(zs, masks):
        sp = jnp.maximum(z, 0.0) + jnp.log2(1.0 + jnp.exp2(jnp.minimum(z, -z)))
        if mask is not None:
            sp = jnp.where(mask, sp, 0.0)
        sps.append(sp.astype(BF16))
    csums, carries = [], []
    for spb, (carry, acc) in zip(sps, states):
        parts = []
        for c in reversed(range(spb.shape[1] // SUB)):
            loc = jnp.dot(spb[:, c * SUB:(c + 1) * SUB], tri, preferred_element_type=F32)
            parts.insert(0, loc + carry)
            carry = carry + loc[:, 0:1]
        csums.append(parts[0] if len(parts) == 1 else jnp.concatenate(parts, axis=1))
        carries.append(carry)
    out = []
    for z, csum, mask, carry, (q, k, v), (_, acc) in zip(zs, csums, masks, carries, ops, states):
        a = jnp.exp2(z - csum)
        if mask is not None:
            a = jnp.where(mask, a, 0.0)
        out.append((carry, acc + jnp.dot(a.astype(BF16), v, preferred_element_type=F32)))
    return tuple(out)


def _sb_body(q_ref, k_ref, v_ref, o_ref, *, t):
    i = pl.program_id(2)
    halves = t // SUB
    r = lax.broadcasted_iota(jnp.int32, (SUB, SUB), 0)
    c = lax.broadcasted_iota(jnp.int32, (SUB, SUB), 1)
    tri = jnp.where(r >= c, 1.0, 0.0).astype(BF16)
    start = pl.multiple_of(i * t, t)
    chains = [(g, a) for g in range(ATT_HEADS) for a in range(halves)]

    def operands(g, a, row0, nrows):
        cols = slice(g * HEAD_DIM, (g + 1) * HEAD_DIM)
        return (q_ref[a * SUB:(a + 1) * SUB, cols], k_ref[pl.ds(row0, nrows), cols],
                v_ref[pl.ds(row0, nrows), cols])

    ops, masks, st0 = [], [], []
    for g, a in chains:
        n = (a + 1) * SUB
        rr = lax.broadcasted_iota(jnp.int32, (SUB, n), 0)
        cc = lax.broadcasted_iota(jnp.int32, (SUB, n), 1)
        ops.append(operands(g, a, start, n))
        masks.append(cc < rr + a * SUB)
        st0.append((jnp.zeros((SUB, 1), F32), jnp.zeros((SUB, HEAD_DIM), F32)))
    state = _sb_chains(ops, tri, st0, masks)

    def step(jj, st):
        row0 = pl.multiple_of((i - 1 - jj) * t, t)
        return _sb_chains([operands(g, a, row0, t) for g, a in chains], tri, st,
                          [None] * len(chains))

    state = lax.fori_loop(0, i, step, state)
    for n, (g, a) in enumerate(chains):
        o_ref[a * SUB:(a + 1) * SUB, g * HEAD_DIM:(g + 1) * HEAD_DIM] = state[n][1].astype(o_ref.dtype)


def _sb_attention(qkv, B, S):
    t = _tile(S, 2 * SUB)
    nq = S // t
    G = ATT_HEADS
    HG = SB_HEADS // G
    W = G * HEAD_DIM
    return _call(
        functools.partial(_sb_body, t=t), grid=(B, HG, nq),
        in_specs=[pl.BlockSpec((t, W), lambda b, h, i: (b * nq + i, h)),
                  pl.BlockSpec((S, W), lambda b, h, i: (b, HG + h)),
                  pl.BlockSpec((S, W), lambda b, h, i: (b, 2 * HG + h))],
        out_specs=pl.BlockSpec((t, W), lambda b, h, i: (b * nq + i, h)),
        out_shape=jax.ShapeDtypeStruct((B * S, SB_WIDTH), BF16),
        name="sb_attention")(qkv, qkv, qkv)


def _mla_chains(ops, states, masks):
    ss = []
    for (q, k, v), mask in zip(ops, masks):
        s = _qk(q, k)
        if mask is not None:
            s = jnp.where(mask, s, -jnp.inf)
        ss.append(s)
    ms = [jnp.maximum(m, jnp.max(s, axis=-1, keepdims=True)) for s, (m, l, acc) in zip(ss, states)]
    ps = [jnp.exp2(s - m_new) for s, m_new in zip(ss, ms)]
    out = []
    for p, m_new, (q, k, v), (m, l, acc) in zip(ps, ms, ops, states):
        alpha = jnp.exp2(m - m_new)
        l = alpha * l + jnp.sum(p, axis=-1, keepdims=True)
        acc = alpha * acc + jnp.dot(p.astype(BF16), v, preferred_element_type=F32)
        out.append((m_new, l, acc))
    return tuple(out)


def _mla_body(q_ref, kv_ref, kpe_ref, o_ref, kf_ref, *, t):
    i = pl.program_id(2)
    halves = t // SUB
    hw = MLA_NOPE + MLA_V

    @pl.when(i == 0)
    def _():
        for g in range(ATT_HEADS):
            kf_ref[g, :, 0:LANE] = kv_ref[:, g * hw:g * hw + MLA_NOPE]
            kf_ref[g, :, LANE:QPAD] = kpe_ref[...]

    start = pl.multiple_of(i * t, t)
    chains = [(g, a) for g in range(ATT_HEADS) for a in range(halves)]

    def operands(g, a, row0, nrows):
        return (q_ref[a * SUB:(a + 1) * SUB, g * QPAD:(g + 1) * QPAD],
                kf_ref[g, pl.ds(row0, nrows), :],
                kv_ref[pl.ds(row0, nrows), g * hw + MLA_NOPE:(g + 1) * hw])

    ops, masks, st0 = [], [], []
    for g, a in chains:
        n = (a + 1) * SUB
        rr = lax.broadcasted_iota(jnp.int32, (SUB, n), 0)
        cc = lax.broadcasted_iota(jnp.int32, (SUB, n), 1)
        ops.append(operands(g, a, start, n))
        masks.append(cc <= rr + a * SUB)
        st0.append((jnp.full((SUB, 1), -jnp.inf, F32), jnp.zeros((SUB, 1), F32),
                    jnp.zeros((SUB, MLA_V), F32)))
    state = _mla_chains(ops, st0, masks)

    def step(j, st):
        row0 = pl.multiple_of(j * t, t)
        return _mla_chains([operands(g, a, row0, t) for g, a in chains], st, [None] * len(chains))

    state = lax.fori_loop(0, i, step, state)
    for n, (g, a) in enumerate(chains):
        m, l, acc = state[n]
        o_ref[a * SUB:(a + 1) * SUB, g * MLA_V:(g + 1) * MLA_V] = (acc / l).astype(o_ref.dtype)


def _mla_attention(q_full, kv, kpe, B, S):
    t = _tile(S, 2 * SUB)
    nq = S // t
    G = ATT_HEADS
    HG = MLA_HEADS // G
    return _call(
        functools.partial(_mla_body, t=t), grid=(B, HG, nq),
        in_specs=[pl.BlockSpec((t, G * QPAD), lambda b, h, i: (b * nq + i, h)),
                  pl.BlockSpec((S, G * (MLA_NOPE + MLA_V)), lambda b, h, i: (b, h)),
                  pl.BlockSpec((S, LANE), lambda b, h, i: (b, 0))],
        out_specs=pl.BlockSpec((t, G * MLA_V), lambda b, h, i: (b * nq + i, h)),
        out_shape=jax.ShapeDtypeStruct((B * S, MLA_HEADS * MLA_V), BF16),
        scratch=[pltpu.VMEM((G, S, QPAD), BF16)],
        name="mla_attention")(q_full, kv, kpe)


def _merge_body(osb_ref, omla_ref, gate_sb_ref, gate_mla_ref, wsb_ref, wmla_ref, wo_ref, x_ref,
                g1_ref, gain_ref, sc_ref, sh_ref, xo_ref, ho_ref):
    y = (gate_sb_ref[...].astype(F32)
         * jnp.dot(osb_ref[...], wsb_ref[...], preferred_element_type=F32)
         + gate_mla_ref[...].astype(F32)
         * jnp.dot(omla_ref[...], wmla_ref[...], preferred_element_type=F32))
    out = jnp.dot(y.astype(BF16), wo_ref[...], preferred_element_type=F32)
    x_new = x_ref[...] + g1_ref[...] * out
    xo_ref[...] = x_new
    ho_ref[...] = _norm_mod(x_new, gain_ref[...], sc_ref[...], sh_ref[...]).astype(ho_ref.dtype)


def _merge(o_sb, o_mla, gates, w_sb, w_mla, w_o, layer, x, g1, gain, sc, sh, S, h_dtype):
    M, D = x.shape
    tm = _tile(S, 256)
    bpt = S // tm
    vec = pl.BlockSpec((None, 1, D), lambda m: (m // bpt, 0, 0))
    res = lambda k: pl.BlockSpec((None, k, D), lambda m: (layer, 0, 0),
                                 pipeline_mode=pl.Buffered(1))
    return _call(
        _merge_body, grid=(M // tm,),
        in_specs=[pl.BlockSpec((tm, SB_WIDTH), lambda m: (m, 0)),
                  pl.BlockSpec((tm, MLA_HEADS * MLA_V), lambda m: (m, 0)),
                  pl.BlockSpec((tm, D), lambda m: (m, 0)),
                  pl.BlockSpec((tm, D), lambda m: (m, 1)),
                  res(SB_WIDTH), res(MLA_HEADS * MLA_V), res(D),
                  pl.BlockSpec((tm, D), lambda m: (m, 0)),
                  vec, pl.BlockSpec((1, D), lambda m: (0, 0)), vec, vec],
        out_specs=[pl.BlockSpec((tm, D), lambda m: (m, 0)),
                   pl.BlockSpec((tm, D), lambda m: (m, 0))],
        out_shape=[jax.ShapeDtypeStruct((M, D), F32), jax.ShapeDtypeStruct((M, D), h_dtype)],
        name="merge_out")(o_sb, o_mla, gates, gates, w_sb, w_mla, w_o, x, g1, gain, sc, sh)


NORM_ROWS = 64


def _resnorm_body(f_ref, x_ref, g_ref, gain_ref, sc_ref, sh_ref, xo_ref, ho_ref):
    g, gain, sc, sh = g_ref[...], gain_ref[...], sc_ref[...], sh_ref[...]

    def chunk(i, _):
        rows = pl.ds(pl.multiple_of(i * NORM_ROWS, NORM_ROWS), NORM_ROWS)
        x_new = x_ref[rows, :] + g * f_ref[rows, :]
        xo_ref[rows, :] = x_new
        ho_ref[rows, :] = _norm_mod(x_new, gain, sc, sh).astype(ho_ref.dtype)
        return 0

    lax.fori_loop(0, x_ref.shape[0] // NORM_ROWS, chunk, 0)


def _resnorm(f, x, g, gain, sc, sh, S, h_dtype):
    M, D = x.shape
    tm = _tile(S, 512)
    bpt = S // tm
    vec = pl.BlockSpec((None, 1, D), lambda m: (m // bpt, 0, 0))
    rows = pl.BlockSpec((tm, D), lambda m: (m, 0))
    return _call(
        _resnorm_body, grid=(M // tm,),
        in_specs=[rows, rows, vec, pl.BlockSpec((1, D), lambda m: (0, 0)), vec, vec],
        out_specs=[rows, rows],
        out_shape=[jax.ShapeDtypeStruct((M, D), F32), jax.ShapeDtypeStruct((M, D), h_dtype)],
        name="resnorm")(f, x, g, gain, sc, sh)


def _split_bf16(v):
    hi = v.astype(BF16)
    return hi, (v - hi.astype(F32)).astype(BF16)


def _router_body(h_ref, w_ref, idx_ref, wt_ref):
    hh, hl = _split_bf16(h_ref[...])
    wh, wl = _split_bf16(w_ref[...])
    dot = functools.partial(jnp.dot, preferred_element_type=F32)
    logits = dot(hh, wh) + (dot(hl, wh) + dot(hh, wl))
    lane = lax.broadcasted_iota(jnp.int32, logits.shape, 1).astype(F32)
    logits = jnp.where(lane < N_EXPERTS, logits, -jnp.inf)
    m1 = jnp.max(logits, axis=-1, keepdims=True)
    i1 = jnp.min(jnp.where(logits == m1, lane, float(LANE)), axis=-1, keepdims=True)
    rest = jnp.where(lane == i1, -jnp.inf, logits)
    m2 = jnp.max(rest, axis=-1, keepdims=True)
    i2 = jnp.min(jnp.where(rest == m2, lane, float(LANE)), axis=-1, keepdims=True)
    e = jnp.exp(m2 - m1)
    w1 = 1.0 / (1.0 + e)
    w2 = e * w1
    idx_ref[...] = jnp.where(lane == 0.0, i1, jnp.where(lane == 1.0, i2, 0.0)).astype(jnp.int32)
    wt_ref[...] = jnp.where(lane == 0.0, w1, jnp.where(lane == 1.0, w2, 0.0))


def _router(h, w_router_pad, j):
    M, D = h.shape
    tm = _tile(M, 512)
    return _call(
        _router_body, grid=(M // tm,),
        in_specs=[pl.BlockSpec((tm, D), lambda m: (m, 0)),
                  pl.BlockSpec((None, D, LANE), lambda m: (j, 0, 0))],
        out_specs=[pl.BlockSpec((tm, LANE), lambda m: (m, 0)),
                   pl.BlockSpec((tm, LANE), lambda m: (m, 0))],
        out_shape=[jax.ShapeDtypeStruct((M, LANE), jnp.int32),
                   jax.ShapeDtypeStruct((M, LANE), F32)],
        name="router")(h, w_router_pad)


MOE_SUB = 256
ROW_UNROLL = 8


def _gather_body(src_ref, h_hbm, o_ref, buf_ref, sem, *, tm):
    t = pl.program_id(0)

    def issue(tile):
        slot = tile % 2

        def blk(b, _):
            for u in range(ROW_UNROLL):
                r = b * ROW_UNROLL + u
                pltpu.make_async_copy(h_hbm.at[pl.ds(src_ref[tile * tm + r], 1)],
                                      buf_ref.at[slot, pl.ds(r, 1)],
                                      sem.at[slot]).start(priority=u % 2)
            return 0

        lax.fori_loop(0, tm // ROW_UNROLL, blk, 0)

    n_active = src_ref[pl.num_programs(0) * tm]

    @pl.when(t == 0)
    def _():
        issue(t)

    @pl.when(t + 1 < n_active)
    def _():
        issue(t + 1)

    @pl.when(t < n_active)
    def _():
        slot = t % 2
        pltpu.make_async_copy(h_hbm.at[pl.ds(0, tm)], buf_ref.at[slot], sem.at[slot]).wait()
        o_ref[...] = buf_ref[slot].astype(o_ref.dtype)

    @pl.when(t >= n_active)
    def _():
        o_ref[...] = jnp.zeros_like(o_ref)


def _gather_rows(src_tok, h, n_tiles, tm):
    M, D = h.shape
    return _call(
        functools.partial(_gather_body, tm=tm), grid=(n_tiles,), nsp=1,
        in_specs=[pl.BlockSpec(memory_space=pl.ANY)],
        out_specs=pl.BlockSpec((tm, D), lambda t, src: (t, 0)),
        out_shape=jax.ShapeDtypeStruct((n_tiles * tm, D), BF16),
        scratch=[pltpu.VMEM((2, tm, D), F32), pltpu.SemaphoreType.DMA((2,))],
        name="moe_gather")(src_tok, h)


def _grouped_body(te_ref, a_ref, *refs, n_w, layer, epilogue):
    w_hbm = refs[:n_w]
    n_rest = len(refs) - 3 * n_w - 1
    rest = refs[n_w:n_w + n_rest]
    stage = refs[n_w + n_rest:2 * n_w + n_rest]
    wb_refs = refs[2 * n_w + n_rest:3 * n_w + n_rest]
    sem = refs[-1]
    c, t = pl.program_id(0), pl.program_id(1)
    nc, nt = pl.num_programs(0), pl.num_programs(1)
    expert = te_ref[t]
    n_valid = te_ref[nt + t]
    nxt = te_ref[2 * nt + t]
    changed = jnp.logical_or(t == 0, expert != te_ref[jnp.maximum(t - 1, 0)])
    tn = stage[0].shape[1]

    def block_copy(i, chunk, e):
        col0 = pl.multiple_of(chunk * tn, tn)
        return pltpu.make_async_copy(w_hbm[i].at[layer, e, :, pl.ds(col0, tn)], stage[i],
                                     sem.at[i])

    @pl.when(changed)
    def _():
        @pl.when(jnp.logical_and(c == 0, t == 0))
        def _():
            for i in range(n_w):
                block_copy(i, c, expert).start()

        for i in range(n_w):
            block_copy(i, c, expert).wait()
            wb_refs[i][...] = stage[i][...].astype(BF16)

        @pl.when(nxt >= 0)
        def _():
            for i in range(n_w):
                block_copy(i, c, nxt).start()

        @pl.when(jnp.logical_and(nxt < 0, c + 1 < nc))
        def _():
            for i in range(n_w):
                block_copy(i, c + 1, te_ref[0]).start()

    for s in range(a_ref.shape[0] // MOE_SUB):
        rows = slice(s * MOE_SUB, (s + 1) * MOE_SUB)

        @pl.when(s * MOE_SUB < n_valid)
        def _():
            a = a_ref[rows, :]
            accs = [jnp.dot(a, wb_ref[...], preferred_element_type=F32) for wb_ref in wb_refs]
            epilogue(accs, rest[:n_rest], rows)

        @pl.when(s * MOE_SUB >= n_valid)
        def _():
            for o_ref in rest[:n_rest]:
                o_ref[rows, :] = jnp.zeros((MOE_SUB, o_ref.shape[1]), o_ref.dtype)


def _up_epilogue(accs, refs, rows=slice(None)):
    g, u = accs
    refs[0][rows, :] = (g * jax.nn.sigmoid(g) * u).astype(refs[0].dtype)


def _grouped_up(tile_e, xs, w_gate, w_up, j, tm, tf):
    NP, D = xs.shape
    F = w_gate.shape[-1]
    hbm = pl.BlockSpec(memory_space=pl.ANY)
    return _call(
        functools.partial(_grouped_body, n_w=2, layer=j, epilogue=_up_epilogue),
        grid=(F // tf, NP // tm), nsp=1,
        in_specs=[pl.BlockSpec((tm, D), lambda f, t, te: (t, 0)), hbm, hbm],
        out_specs=pl.BlockSpec((tm, tf), lambda f, t, te: (t, f)),
        out_shape=jax.ShapeDtypeStruct((NP, F), BF16),
        scratch=[pltpu.VMEM((D, tf), F32), pltpu.VMEM((D, tf), F32),
                 pltpu.VMEM((D, tf), BF16), pltpu.VMEM((D, tf), BF16),
                 pltpu.SemaphoreType.DMA((2,))],
        name="moe_up")(tile_e, xs, w_gate, w_up)


def _down_epilogue(accs, refs, rows):
    refs[0][rows, :] = accs[0]


def _grouped_down(tile_e, hs, w_down, j, tm, tn):
    NP, F = hs.shape
    D = w_down.shape[-1]
    return _call(
        functools.partial(_grouped_body, n_w=1, layer=j, epilogue=_down_epilogue),
        grid=(D // tn, NP // tm), nsp=1,
        in_specs=[pl.BlockSpec((tm, F), lambda n, t, te: (t, 0)),
                  pl.BlockSpec(memory_space=pl.ANY)],
        out_specs=pl.BlockSpec((tm, tn), lambda n, t, te: (t, n)),
        out_shape=jax.ShapeDtypeStruct((NP, D), F32),
        scratch=[pltpu.VMEM((F, tn), F32), pltpu.VMEM((F, tn), BF16),
                 pltpu.SemaphoreType.DMA((1,))],
        name="moe_down")(tile_e, hs, w_down)


def _combine_body(pos_ref, y_hbm, wt_ref, x_ref, g2_ref, gain_ref, sc_ref, sh_ref, xo_ref, ho_ref,
                  buf_ref, sem, *, tm):
    t = pl.program_id(0)

    def issue(tile):
        slot = tile % 2

        def blk(b, _):
            for u in range(ROW_UNROLL):
                r = b * ROW_UNROLL + u
                for k in range(TOP_K):
                    pltpu.make_async_copy(
                        y_hbm.at[pl.ds(pos_ref[(tile * tm + r) * TOP_K + k], 1)],
                        buf_ref.at[slot, k, pl.ds(r, 1)], sem.at[slot]).start(priority=k % 2)
            return 0

        lax.fori_loop(0, tm // ROW_UNROLL, blk, 0)

    @pl.when(t == 0)
    def _():
        issue(t)

    @pl.when(t + 1 < pl.num_programs(0))
    def _():
        issue(t + 1)

    slot = t % 2
    for k in range(TOP_K):
        pltpu.make_async_copy(y_hbm.at[pl.ds(0, tm)], buf_ref.at[slot, k], sem.at[slot]).wait()
    wt = wt_ref[...]
    f = buf_ref[slot, 0] * wt[:, 0:1]
    for k in range(1, TOP_K):
        f = f + buf_ref[slot, k] * wt[:, k:k + 1]
    x_new = x_ref[...] + g2_ref[...] * f
    xo_ref[...] = x_new
    ho_ref[...] = _norm_mod(x_new, gain_ref[...], sc_ref[...], sh_ref[...]).astype(ho_ref.dtype)


def _combine(pos, y_sorted, wts, x, g2, gain, sc, sh, S, h_dtype):
    M, D = x.shape
    tm = _tile(S, 256)
    bpt = S // tm
    vec = pl.BlockSpec((None, 1, D), lambda m, p: (m // bpt, 0, 0))
    return _call(
        functools.partial(_combine_body, tm=tm), grid=(M // tm,), nsp=1,
        in_specs=[pl.BlockSpec(memory_space=pl.ANY),
                  pl.BlockSpec((tm, LANE), lambda m, p: (m, 0)),
                  pl.BlockSpec((tm, D), lambda m, p: (m, 0)),
                  vec, pl.BlockSpec((1, D), lambda m, p: (0, 0)), vec, vec],
        out_specs=[pl.BlockSpec((tm, D), lambda m, p: (m, 0)),
                   pl.BlockSpec((tm, D), lambda m, p: (m, 0))],
        out_shape=[jax.ShapeDtypeStruct((M, D), F32), jax.ShapeDtypeStruct((M, D), h_dtype)],
        scratch=[pltpu.VMEM((2, TOP_K, tm, D), F32), pltpu.SemaphoreType.DMA((2,))],
        name="moe_combine")(pos, y_sorted, wts, x, g2, gain, sc, sh)


def _dispatch_plan(top_i, tm):
    M = top_i.shape[0]
    E = N_EXPERTS
    n_ent = M * TOP_K
    e_flat = top_i.reshape(-1)
    onehot = (e_flat[:, None] == jnp.arange(E, dtype=jnp.int32)[None, :]).astype(jnp.int32)
    csum = jnp.cumsum(onehot, axis=0)
    rank = jnp.sum(onehot * csum, axis=1) - 1
    counts = csum[-1]
    pcounts = ((counts + tm - 1) // tm) * tm
    pend = jnp.cumsum(pcounts)
    pstart = pend - pcounts
    dest = (pstart[e_flat] + rank).astype(jnp.int32)
    n_tiles = n_ent // tm + E
    n_rows = n_tiles * tm
    src_tok = (jnp.arange(n_rows, dtype=jnp.int32) % M).at[dest].set(
        jnp.arange(n_ent, dtype=jnp.int32) // TOP_K)
    tile_start = jnp.arange(n_tiles, dtype=jnp.int32) * tm
    tile_e = jnp.sum((tile_start[:, None] >= pend[None, :]).astype(jnp.int32), axis=1)
    tile_e = jnp.minimum(tile_e, E - 1).astype(jnp.int32)
    n_valid = jnp.clip(pstart[tile_e] + counts[tile_e] - tile_start, 0, tm).astype(jnp.int32)
    n_active = pend[-1] // tm
    tile_e = jnp.where(tile_start < pend[-1], tile_e, tile_e[n_active - 1])
    tidx = jnp.arange(n_tiles, dtype=jnp.int32)
    later_other = (tidx[None, :] > tidx[:, None]) & (tile_e[None, :] != tile_e[:, None])
    nxt = jnp.where(jnp.any(later_other, axis=1), tile_e[jnp.argmax(later_other, axis=1)], -1)
    src_tok = jnp.concatenate([src_tok, n_active[None].astype(jnp.int32)])
    return src_tok, dest, jnp.concatenate([tile_e, n_valid, nxt]).astype(jnp.int32), n_tiles


def _rot_cols(w):
    half = MLA_ROPE // 2
    return jnp.concatenate([-w[..., half:], w[..., :half]], axis=-1)


def _rope_tables(positions):
    inv_freq = 1.0 / (ROPE_THETA ** (jnp.arange(0, MLA_ROPE, 2, dtype=F32) / MLA_ROPE))
    ang = positions.astype(F32)[..., None] * inv_freq
    cos, sin = jnp.cos(ang), jnp.sin(ang)
    M = positions.size
    cos2 = jnp.concatenate([cos, cos], axis=-1).reshape(M, MLA_ROPE)
    sin2 = jnp.concatenate([sin, sin], axis=-1).reshape(M, MLA_ROPE)
    zeros = jnp.zeros((M, LANE - MLA_ROPE), F32)
    cos_k = jnp.concatenate([cos2, zeros], axis=-1)
    sin_k = jnp.concatenate([sin2, zeros], axis=-1)
    return cos_k, sin_k, cos_k * MLA_SCALE, sin_k * MLA_SCALE


def kernel(x, c, positions, w_ada, b_ada, norm_mix_g, norm_ffn_g, w_in, q_norm_g, kv_norm_g, w_uq, w_ukv, w_sb_up, w_mla_up, w_o, w_ffn_gate, w_ffn_up, w_ffn_down, w_router, w_exp_gate, w_exp_up, w_exp_down, final_norm_g):
    B, S, D = x.shape
    L = w_ada.shape[0]
    M = B * S
    x = x.reshape(M, D)

    mod = _modulation(c, w_ada, b_ada)
    mod = mod.reshape(L, B, N_MOD, 1, D)
    sh1, sc1, g1, sh2, sc2, g2 = [mod[:, :, k] for k in range(N_MOD)]
    cos_k, sin_k, cos_q, sin_q = _rope_tables(positions)

    pe0 = 3 * SB_WIDTH + Q_RANK + KV_RANK
    w_in_t = jnp.swapaxes(w_in, 1, 2)
    w_pe = w_in_t[:, pe0:pe0 + MLA_ROPE]
    zpad = jnp.zeros((L, LANE - MLA_ROPE, D), F32)
    half = MLA_ROPE // 2
    w_pe_rot = jnp.concatenate([-w_pe[:, half:], w_pe[:, :half]], axis=1)
    w_pe2 = jnp.concatenate([w_pe, zpad, w_pe_rot, zpad], axis=1)
    c_gain = jnp.concatenate([q_norm_g, kv_norm_g], axis=-1)[:, None, :]

    wq = w_uq.reshape(L, Q_RANK, MLA_HEADS, MLA_QK)
    wq_n, wq_r = wq[..., :MLA_NOPE], wq[..., MLA_NOPE:]
    zq = jnp.zeros(wq_r.shape, F32)
    wq_main = jnp.concatenate([wq_n, wq_r, zq], axis=-1).reshape(
        L, Q_RANK, MLA_HEADS * QPAD).astype(BF16)
    wq_rot = jnp.concatenate([_rot_cols(wq_r), zq], axis=-1).reshape(
        L, Q_RANK, MLA_HEADS * LANE).astype(BF16)

    w_sb_b, w_mla_b, w_o_b = (w.astype(BF16) for w in (w_sb_up, w_mla_up, w_o))
    w_router_pad = jnp.concatenate(
        [w_router, jnp.zeros(w_router.shape[:2] + (LANE - N_EXPERTS,), F32)], axis=-1)

    tm = _tile(S, 1024)
    bpt = S // tm
    row = lambda width: pl.BlockSpec((tm, width), lambda n, m: (m, 0))
    sb_scale = HEAD_DIM ** -0.5 * LOG2E

    def h_dtype_for(layer_is_moe):
        return F32 if layer_is_moe else BF16

    h = _prenorm(x, norm_mix_g[0][None, :], sc1[0], sh1[0], S, BF16)

    for l in range(L):
        tn = _tile(min(SB_WIDTH, 2 * D), 1024)
        nq_blk = SB_WIDTH // tn

        def qkv_epi(accs, n, extra, outs):
            s = jnp.where(n < nq_blk, sb_scale, 1.0)
            outs[0][...] = (accs[0] * s).astype(BF16)

        (qkv,) = _matmul(h, [w_in_t], layer=l, col_blk0=0, n_cols=3 * SB_WIDTH, tn=tn, tm=tm,
                         epilogue=qkv_epi, transposed=True,
                         out_shape=[jax.ShapeDtypeStruct((M, 3 * SB_WIDTH), BF16)],
                         out_specs=[pl.BlockSpec((tm, tn), lambda n, m: (m, n))], name="in_qkv")

        lat, kpe = _latent_in(h, w_in_t, w_pe2, c_gain, cos_k, sin_k, l, tm)

        def gate_epi(accs, n, extra, outs):
            outs[0][...] = jax.nn.sigmoid(accs[0]).astype(BF16)

        gates = _matmul_rows(h, w_in_t, layer=l, row0=pe0 + MLA_ROPE, n_cols=2 * D, tn=tn, tm=tm,
                             epilogue=gate_epi, out_dtype=BF16, name="in_gates")

        q_full, kv = _latent_up(lat, wq_main, wq_rot, w_ukv, l, cos_q, sin_q, tm)

        o_sb = _sb_attention(qkv, B, S)
        o_mla = _mla_attention(q_full, kv, kpe, B, S)

        moe = (l % 2 == 1)
        x, h = _merge(o_sb, o_mla, gates, w_sb_b, w_mla_b, w_o_b, l, x, g1[l],
                      norm_ffn_g[l][None, :], sc2[l], sh2[l], S, h_dtype_for(moe))

        last = (l == L - 1)
        if last:
            n_gain = final_norm_g[None, :]
            n_sc = jnp.zeros_like(sc1[0])
            n_sh = jnp.zeros_like(sh1[0])
            n_dtype = F32
        else:
            n_gain, n_sc, n_sh, n_dtype = norm_mix_g[l + 1][None, :], sc1[l + 1], sh1[l + 1], BF16
        j = l // 2
        if not moe:
            F = w_ffn_gate.shape[-1]
            tf = _tile(F, 512)

            def up_epi(accs, n, extra, outs):
                _up_epilogue(accs, outs)

            (hmid,) = _matmul(h, [w_ffn_gate, w_ffn_up], layer=j, col_blk0=0, n_cols=F, tn=tf,
                              tm=tm, epilogue=up_epi,
                              out_shape=[jax.ShapeDtypeStruct((M, F), BF16)],
                              out_specs=[pl.BlockSpec((tm, tf), lambda n, m: (m, n))],
                              name="ffn_up")
            def down_epi(accs, n, extra, outs):
                outs[0][...] = accs[0]

            tnd = _tile(D, 512)
            (f,) = _matmul(hmid, [w_ffn_down], layer=j, col_blk0=0, n_cols=D, tn=tnd,
                           tm=_tile(S, 512), epilogue=down_epi,
                           out_shape=[jax.ShapeDtypeStruct((M, D), F32)],
                           out_specs=[pl.BlockSpec((_tile(S, 512), tnd), lambda n, m: (m, n))],
                           name="ffn_down")
            x, h = _resnorm(f, x, g2[l], n_gain, n_sc, n_sh, S, n_dtype)
        else:
            F = w_exp_gate.shape[-1]
            tme = max(_tile(M, 2 * MOE_SUB), MOE_SUB)
            idx, wts = _router(h, w_router_pad, j)
            src_tok, dest, tile_e, n_tiles = _dispatch_plan(idx[:, :TOP_K], tme)
            xs = _gather_rows(src_tok, h, n_tiles, tme)
            tfe = F // 4 if F % (4 * LANE) == 0 else _tile(F, 512)
            hs = _grouped_up(tile_e, xs, w_exp_gate, w_exp_up, j, tme, tfe)
            ys = _grouped_down(tile_e, hs, w_exp_down, j, tme, _tile(D, 512))
            x, h = _combine(dest, ys, wts, x, g2[l], n_gain, n_sc, n_sh, S, n_dtype)

    return h.reshape(B, S, D)
```
